```python
import jax
import jax.numpy as jnp
from jax import lax
import numpy as np

D_MODEL = 1024
BATCH = 1
SEQ = 16384
DEPTH = 1
DEC_BATCH = 128
DEC_SEQ = 8
PAST_LEN = 8192
PAGE_SIZE = 128

N_HEADS = 16
HEAD_DIM = D_MODEL // N_HEADS
N_KV_HEADS = 4
KV_WIDTH = N_KV_HEADS * HEAD_DIM
Q_WIDTH = N_HEADS * HEAD_DIM
N_KV_SETS = 6
N_NSA_BRANCH = 3
CMP_LEN = 32
CMP_STRIDE = 16
PHI_HIDDEN = 2 * HEAD_DIM
SEL_LEN = 64
N_SEL = 16
N_LOCAL = 2
WINDOW = 512
Q_BLOCK = 128
D_CONV = D_MODEL // 2
CONV_WIDTH = 31
N_EXPERTS = 32
TOP_K = 4
D_EXPERT = D_MODEL
SWIGLU_LIMIT = 7.0
SWIGLU_ALPHA = 1.702
MOE_BLOCK = 128
DN_ALPHA = (2.0 * DEPTH) ** 0.25
DN_BETA = (8.0 * DEPTH) ** -0.25
LN_EPS = 1e-5
NEG_INF = -1e30
FORCE_SCORE = 1e9
OFF_KV = Q_WIDTH
OFF_GLU = OFF_KV + N_KV_SETS * KV_WIDTH
OFF_NSA_G = OFF_GLU + 2 * D_CONV
OFF_MERGE = OFF_NSA_G + N_HEADS * N_NSA_BRANCH
D_IN = OFF_MERGE + 2 * D_MODEL

kernel_name = 'hybrid_conformer_nsa_moe_decode_step'


def layer_norm(x, g, b):
    xf = x.astype(jnp.float32)
    mu = jnp.mean(xf, axis=-1, keepdims=True)
    var = jnp.mean(jnp.square(xf - mu), axis=-1, keepdims=True)
    return ((xf - mu) * lax.rsqrt(var + LN_EPS)).astype(x.dtype) * g + b


def masked_softmax(s, mask):
    p = jax.nn.softmax(jnp.where(mask, s, NEG_INF), axis=-1)
    return p * mask


def alibi_slopes():
    return 2.0 ** (-8.0 * jnp.arange(1, N_HEADS + 1, dtype=jnp.float32) / N_HEADS)


def adaln(c, w, b):
    mod = (c @ w + b)[:, None, :]
    return jnp.split(mod, 6, axis=-1)


def split_in(z):
    B, T = z.shape[:2]
    q = z[..., :OFF_KV].reshape(B, T, N_HEADS, HEAD_DIM)
    kv = z[..., OFF_KV:OFF_GLU].reshape(B, T, N_KV_SETS, N_KV_HEADS, HEAD_DIM)
    glu_in = z[..., OFF_GLU:OFF_NSA_G]
    nsa_g = jax.nn.sigmoid(z[..., OFF_NSA_G:OFF_MERGE]).reshape(B, T, N_HEADS, N_NSA_BRANCH)
    merge_g = jax.nn.sigmoid(z[..., OFF_MERGE:])
    return q, kv[:, :, :4], kv[:, :, 4:], glu_in, nsa_g, merge_g


def glu(z):
    a, b = jnp.split(z, 2, axis=-1)
    return a * jax.nn.sigmoid(b)


def conv_branch(u_ext, dw_w, dw_b, ln_g, ln_b, w_pw):
    y = lax.conv_general_dilated(u_ext, dw_w[:, None, :], (1,), 'VALID',
                                 dimension_numbers=('NWC', 'WIO', 'NWC'),
                                 feature_group_count=D_CONV) + dw_b
    return jax.nn.silu(layer_norm(y, ln_g, ln_b)) @ w_pw


def compress(rows, pe, w1, w2):
    B, T = rows.shape[:2]
    n_ch = CMP_LEN // CMP_STRIDE
    nc = T // CMP_STRIDE - n_ch + 1
    ch = rows.reshape(B, T // CMP_STRIDE, CMP_STRIDE, N_KV_HEADS, HEAD_DIM)
    pe = pe.reshape(n_ch, CMP_STRIDE, 1, HEAD_DIM)
    w1 = w1.reshape(n_ch, CMP_STRIDE, HEAD_DIM, PHI_HIDDEN)
    hid = sum(jnp.einsum('bcsgd,sde->bcge', ch + pe[j], w1[j])[:, j:j + nc] for j in range(n_ch))
    return jax.nn.gelu(hid) @ w2


def sel_blocks(rows):
    B, T = rows.shape[:2]
    return rows.reshape(B, T // SEL_LEN, SEL_LEN, N_KV_HEADS, HEAD_DIM).transpose(0, 3, 1, 2, 4)


def nsa_attend(q, tq, g, kc, vc, ksb, vsb, kw, vw, pw, slopes):
    f32 = jnp.float32
    B, Q = q.shape[:2]
    R = N_HEADS // N_KV_HEADS
    qg = q.astype(f32).reshape(B, Q, N_KV_HEADS, R, HEAD_DIM) * (HEAD_DIM ** -0.5)
    m = slopes.reshape(1, N_KV_HEADS, R, 1, 1)
    tqf = tq.astype(f32)
    nc = kc.shape[1]
    ends = jnp.arange(nc, dtype=jnp.int32) * CMP_STRIDE + (CMP_LEN - 1)
    s = jnp.einsum('bqgrd,bngd->bgrqn', qg, kc.astype(f32)) - m * (tqf[:, None] - ends.astype(f32)[None, :])
    p_cmp = masked_softmax(s, ends[None, :] <= tq[:, None])
    o_cmp = jnp.einsum('bgrqn,bngd->bqgrd', p_cmp, vc.astype(f32))
    ns = ksb.shape[2]
    ratio = SEL_LEN // CMP_STRIDE
    lead = CMP_LEN // CMP_STRIDE - 1
    imp = jnp.pad(p_cmp.sum(axis=2), ((0, 0), (0, 0), (0, 0), (lead, ratio * ns - nc)))
    score = sum(imp[..., k:k + ratio * ns:ratio] for k in range(ratio + lead))
    blk = jnp.arange(ns, dtype=jnp.int32)
    back = (tq // SEL_LEN)[:, None] - blk[None, :]
    forced = (blk[None, :] == 0) | ((back >= 0) & (back < N_LOCAL))
    score = jnp.where(forced, FORCE_SCORE, jnp.where(back >= 0, score, -1.0))
    n_sel = min(N_SEL, ns)
    _, idx = lax.top_k(score, n_sel)
    bi = jnp.arange(B)[:, None, None, None]
    gi = jnp.arange(N_KV_HEADS)[None, :, None, None]
    kg = ksb[bi, gi, idx].astype(f32).reshape(B, N_KV_HEADS, Q, n_sel * SEL_LEN, HEAD_DIM)
    vg = vsb[bi, gi, idx].astype(f32).reshape(B, N_KV_HEADS, Q, n_sel * SEL_LEN, HEAD_DIM)
    pos = (idx[..., None] * SEL_LEN + jnp.arange(SEL_LEN, dtype=jnp.int32)).reshape(B, N_KV_HEADS, Q, n_sel * SEL_LEN)
    dist = tq[None, None, :, None] - pos
    s = jnp.einsum('bqgrd,bgqkd->bgrqk', qg, kg) - m * dist[:, :, None].astype(f32)
    p = masked_softmax(s, (dist >= 0)[:, :, None])
    o_slc = jnp.einsum('bgrqk,bgqkd->bqgrd', p, vg)
    dw = tq[:, None] - pw[None, :]
    s = jnp.einsum('bqgrd,bwgd->bgrqw', qg, kw.astype(f32)) - m * dw.astype(f32)
    p = masked_softmax(s, (dw >= 0) & (dw < WINDOW) & (pw[None, :] >= 0))
    o_win = jnp.einsum('bgrqw,bwgd->bqgrd', p, vw.astype(f32))
    o = jnp.stack([o_cmp, o_slc, o_win], axis=-1).reshape(B, Q, N_HEADS, HEAD_DIM, N_NSA_BRANCH)
    o = jnp.einsum('bqhdn,bqhn->bqhd', o, g.astype(f32))
    return o.reshape(B, Q, Q_WIDTH).astype(q.dtype)


def nsa_prompt(q, g, kv4, win2, cmp_w, slopes):
    B, T = q.shape[:2]
    kc = compress(kv4[:, :, 0], *cmp_w[:3])
    vc = compress(kv4[:, :, 1], *cmp_w[3:])
    ksb = sel_blocks(kv4[:, :, 2])
    vsb = sel_blocks(kv4[:, :, 3])
    pad = ((0, 0), (WINDOW, 0), (0, 0), (0, 0))
    kw_pad = jnp.pad(win2[:, :, 0], pad)
    vw_pad = jnp.pad(win2[:, :, 1], pad)
    nqb = T // Q_BLOCK
    qb = q.reshape(B, nqb, Q_BLOCK, N_HEADS, HEAD_DIM).swapaxes(0, 1)
    gb = g.reshape(B, nqb, Q_BLOCK, N_HEADS, N_NSA_BRANCH).swapaxes(0, 1)
    starts = jnp.arange(nqb, dtype=jnp.int32) * Q_BLOCK

    def block(args):
        qi, gi_, s0 = args
        tq = s0 + jnp.arange(Q_BLOCK, dtype=jnp.int32)
        pw = s0 - WINDOW + jnp.arange(WINDOW + Q_BLOCK, dtype=jnp.int32)
        kw = lax.dynamic_slice_in_dim(kw_pad, s0, WINDOW + Q_BLOCK, axis=1)
        vw = lax.dynamic_slice_in_dim(vw_pad, s0, WINDOW + Q_BLOCK, axis=1)
        return nsa_attend(qi, tq, gi_, kc, vc, ksb, vsb, kw, vw, pw, slopes)

    o = lax.map(block, (qb, gb, starts))
    return o.swapaxes(0, 1).reshape(B, T, Q_WIDTH)


def nsa_sample(q, g, kv4_new, win2_new, cache_kv_l, cache_win_l, page_table, cmp_w, slopes):
    n_new = q.shape[1]
    t_real = PAST_LEN + n_new
    t_pad = -(-t_real // SEL_LEN) * SEL_LEN
    wb = cache_win_l.shape[1]
    tq = PAST_LEN + jnp.arange(n_new, dtype=jnp.int32)
    pw = PAST_LEN - wb + jnp.arange(wb + n_new, dtype=jnp.int32)

    def one(args):
        qb, gb, kvb, wnb, wbuf, pages = args
        past = cache_kv_l[pages].reshape(PAST_LEN, 4, N_KV_HEADS, HEAD_DIM)
        rows = jnp.concatenate([past, kvb.astype(past.dtype)], axis=0)
        rows = jnp.pad(rows, ((0, t_pad - t_real), (0, 0), (0, 0), (0, 0)))[None]
        kc = compress(rows[:, :, 0], *cmp_w[:3])
        vc = compress(rows[:, :, 1], *cmp_w[3:])
        wrows = jnp.concatenate([wbuf, wnb.astype(wbuf.dtype)], axis=0)[None]
        o = nsa_attend(qb[None], tq, gb[None], kc, vc, sel_blocks(rows[:, :, 2]), sel_blocks(rows[:, :, 3]),
                       wrows[:, :, 0], wrows[:, :, 1], pw, slopes)
        return o[0]

    return lax.map(one, (q, g, kv4_new, win2_new, cache_win_l, page_table))


def merge_branches(conv_y, nsa_y, merge_g, w_out):
    return (merge_g[..., :D_MODEL] * conv_y + merge_g[..., D_MODEL:] * nsa_y) @ w_out


def moe(h, w_router, b_router, w_up, b_up, w_down, b_down):
    B, T, D = h.shape
    n = B * T
    xf = h.reshape(n, D)
    logits = (xf @ w_router + b_router).astype(jnp.float32)
    top_v, top_e = lax.top_k(logits, TOP_K)
    wts = jax.nn.softmax(top_v, axis=-1)
    nk = n * TOP_K
    e_flat = top_e.reshape(nk)
    tok_flat = jnp.arange(nk, dtype=jnp.int32) // TOP_K
    order = jnp.argsort(e_flat)
    e_sorted = e_flat[order]
    counts = jnp.bincount(e_flat, length=N_EXPERTS)
    starts = jnp.cumsum(counts) - counts
    padded = (counts + MOE_BLOCK - 1) // MOE_BLOCK * MOE_BLOCK
    pends = jnp.cumsum(padded)
    dest = (pends - padded)[e_sorted] + (jnp.arange(nk, dtype=jnp.int32) - starts[e_sorted])
    n_rows = -(-nk // MOE_BLOCK) * MOE_BLOCK + N_EXPERTS * MOE_BLOCK
    row_tok = jnp.full((n_rows,), n, jnp.int32).at[dest].set(tok_flat[order])
    row_w = jnp.zeros((n_rows,), jnp.float32).at[dest].set(wts.reshape(nk)[order])
    nb = n_rows // MOE_BLOCK
    blk_exp = jnp.minimum(jnp.searchsorted(pends, jnp.arange(nb, dtype=jnp.int32) * MOE_BLOCK, side='right'), N_EXPERTS - 1)
    x_pad = jnp.concatenate([xf, jnp.zeros((1, D), xf.dtype)], axis=0)
    xb = x_pad[row_tok].reshape(nb, MOE_BLOCK, D)

    def expert_block(args):
        xi, e = args
        u = xi @ w_up[e] + b_up[e]
        x_glu, x_lin = u[:, :D_EXPERT], u[:, D_EXPERT:]
        x_glu = jnp.minimum(x_glu, SWIGLU_LIMIT)
        x_lin = jnp.clip(x_lin, -SWIGLU_LIMIT, SWIGLU_LIMIT)
        a = x_glu * jax.nn.sigmoid(SWIGLU_ALPHA * x_glu) * (x_lin + 1.0)
        return a @ w_down[e] + b_down[e]

    yb = lax.map(expert_block, (xb, blk_exp)).reshape(n_rows, D)
    out = jnp.zeros((n + 1, D), yb.dtype).at[row_tok].add(yb * row_w[:, None].astype(yb.dtype))
    return out[:n].reshape(B, T, D)


def setup_inputs(seed: int = 0) -> dict:
    key = jax.random.key(seed)
    ks = jax.random.split(key, 40)
    f32 = jnp.float32

    def nrm(k, shape, s):
        return jax.random.normal(k, shape, f32) * s

    L = DEPTH
    n_pages = PAST_LEN // PAGE_SIZE
    n_used = DEC_BATCH * n_pages
    n_phys = n_used + (n_used + 3) // 4
    win_buf = min(WINDOW, PAST_LEN)
    sd = D_MODEL ** -0.5
    kv_scale = jnp.array([1.0, DN_BETA, 1.0, DN_BETA, 1.0, DN_BETA], f32)[None, None, :, None]
    w_in = jnp.concatenate([
        nrm(ks[8], (L, D_MODEL, Q_WIDTH), sd),
        (nrm(ks[9], (L, D_MODEL, N_KV_SETS, KV_WIDTH), sd) * kv_scale).reshape(L, D_MODEL, N_KV_SETS * KV_WIDTH),
        nrm(ks[10], (L, D_MODEL, 2 * D_CONV), sd),
        nrm(ks[11], (L, D_MODEL, N_HEADS * N_NSA_BRANCH), sd),
        nrm(ks[12], (L, D_MODEL, 2 * D_MODEL), sd),
    ], axis=-1)
    return {
        'x_prompt': nrm(ks[0], (BATCH, SEQ, D_MODEL), 1.0),
        'x_sample': nrm(ks[1], (DEC_BATCH, DEC_SEQ, D_MODEL), 1.0),
        'c_prompt': nrm(ks[2], (BATCH, D_MODEL), 1.0),
        'c_sample': nrm(ks[3], (DEC_BATCH, D_MODEL), 1.0),
        'cache_kv': nrm(ks[4], (L, n_phys, PAGE_SIZE, 4, N_KV_HEADS, HEAD_DIM), 1.0),
        'cache_win': nrm(ks[5], (L, DEC_BATCH, win_buf, 2, N_KV_HEADS, HEAD_DIM), 1.0),
        'state_conv': nrm(ks[6], (L, DEC_BATCH, CONV_WIDTH - 1, D_CONV), 1.0),
        'page_table': jax.random.permutation(ks[7], n_phys)[:n_used].reshape(DEC_BATCH, n_pages).astype(jnp.int32),
        'w_ada': nrm(ks[13], (L, D_MODEL, 6 * D_MODEL), 0.5 * sd),
        'b_ada': nrm(ks[14], (L, 6 * D_MODEL), 0.02),
        'w_in': w_in,
        'conv_dw_w': nrm(ks[15], (L, CONV_WIDTH, D_CONV), CONV_WIDTH ** -0.5),
        'conv_dw_b': nrm(ks[16], (L, D_CONV), 0.02),
        'conv_ln_g': 1.0 + nrm(ks[17], (L, D_CONV), 0.02),
        'conv_ln_b': nrm(ks[18], (L, D_CONV), 0.02),
        'w_conv_out': nrm(ks[19], (L, D_CONV, D_MODEL), DN_BETA * D_CONV ** -0.5),
        'cmp_pe_k': nrm(ks[20], (L, CMP_LEN, HEAD_DIM), 0.1),
        'cmp_w1_k': nrm(ks[21], (L, CMP_LEN, HEAD_DIM, PHI_HIDDEN), (CMP_LEN * HEAD_DIM) ** -0.5),
        'cmp_w2_k': nrm(ks[22], (L, PHI_HIDDEN, HEAD_DIM), PHI_HIDDEN ** -0.5),
        'cmp_pe_v': nrm(ks[23], (L, CMP_LEN, HEAD_DIM), 0.1),
        'cmp_w1_v': nrm(ks[24], (L, CMP_LEN, HEAD_DIM, PHI_HIDDEN), (CMP_LEN * HEAD_DIM) ** -0.5),
        'cmp_w2_v': nrm(ks[25], (L, PHI_HIDDEN, HEAD_DIM), PHI_HIDDEN ** -0.5),
        'w_out': nrm(ks[26], (L, D_MODEL, D_MODEL), DN_BETA * sd),
        'ln1_g': 1.0 + nrm(ks[27], (L, D_MODEL), 0.02),
        'ln1_b': nrm(ks[28], (L, D_MODEL), 0.02),
        'w_router': nrm(ks[29], (L, D_MODEL, N_EXPERTS), sd),
        'b_router': nrm(ks[30], (L, N_EXPERTS), 0.01),
        'w_up': nrm(ks[31], (L, N_EXPERTS, D_MODEL, 2 * D_EXPERT), sd),
        'b_up': nrm(ks[32], (L, N_EXPERTS, 2 * D_EXPERT), 0.02),
        'w_down': nrm(ks[33], (L, N_EXPERTS, D_EXPERT, D_MODEL), DN_BETA * D_EXPERT ** -0.5),
        'b_down': nrm(ks[34], (L, N_EXPERTS, D_MODEL), 0.02),
        'ln2_g': 1.0 + nrm(ks[35], (L, D_MODEL), 0.02),
        'ln2_b': nrm(ks[36], (L, D_MODEL), 0.02),
    }


def reference(x_prompt, x_sample, c_prompt, c_sample, cache_kv, cache_win, state_conv, page_table,
              w_ada, b_ada, w_in, conv_dw_w, conv_dw_b, conv_ln_g, conv_ln_b, w_conv_out,
              cmp_pe_k, cmp_w1_k, cmp_w2_k, cmp_pe_v, cmp_w1_v, cmp_w2_v,
              w_out, ln1_g, ln1_b, w_router, b_router, w_up, b_up, w_down, b_down, ln2_g, ln2_b):
    slopes = alibi_slopes()
    wb = cache_win.shape[2]
    xp, xs = x_prompt, x_sample
    kv_p, kv_s, win_p, win_s, conv_p, conv_s = [], [], [], [], [], []
    for l in range(DEPTH):
        cmp_w = (cmp_pe_k[l], cmp_w1_k[l], cmp_w2_k[l], cmp_pe_v[l], cmp_w1_v[l], cmp_w2_v[l])
        moe_w = (w_router[l], b_router[l], w_up[l], b_up[l], w_down[l], b_down[l])
        conv_w = (conv_dw_w[l], conv_dw_b[l], conv_ln_g[l], conv_ln_b[l], w_conv_out[l])

        sh1, sc1, g1, sh2, sc2, g2 = adaln(c_prompt, w_ada[l], b_ada[l])
        q, kv4, win2, glu_in, nsa_g, merge_g = split_in((xp * (1.0 + sc1) + sh1) @ w_in[l])
        u = glu(glu_in)
        conv_y = conv_branch(jnp.pad(u, ((0, 0), (CONV_WIDTH - 1, 0), (0, 0))), *conv_w)
        nsa_y = nsa_prompt(q, nsa_g, kv4, win2, cmp_w, slopes)
        xp = layer_norm(DN_ALPHA * xp + g1 * merge_branches(conv_y, nsa_y, merge_g, w_out[l]), ln1_g[l], ln1_b[l])
        xp = layer_norm(DN_ALPHA * xp + g2 * moe(xp * (1.0 + sc2) + sh2, *moe_w), ln2_g[l], ln2_b[l])
        kv_p.append(kv4)
        if SEQ >= wb:
            win_p.append(win2[:, SEQ - wb:])
        else:
            win_p.append(jnp.pad(win2, ((0, 0), (wb - SEQ, 0), (0, 0), (0, 0), (0, 0))))
        conv_p.append(u[:, SEQ - (CONV_WIDTH - 1):])

        sh1, sc1, g1, sh2, sc2, g2 = adaln(c_sample, w_ada[l], b_ada[l])
        q, kv4, win2, glu_in, nsa_g, merge_g = split_in((xs * (1.0 + sc1) + sh1) @ w_in[l])
        u_ext = jnp.concatenate([state_conv[l].astype(glu_in.dtype), glu(glu_in)], axis=1)
        conv_y = conv_branch(u_ext, *conv_w)
        nsa_y = nsa_sample(q, nsa_g, kv4, win2, cache_kv[l], cache_win[l], page_table, cmp_w, slopes)
        xs = layer_norm(DN_ALPHA * xs + g1 * merge_branches(conv_y, nsa_y, merge_g, w_out[l]), ln1_g[l], ln1_b[l])
        xs = layer_norm(DN_ALPHA * xs + g2 * moe(xs * (1.0 + sc2) + sh2, *moe_w), ln2_g[l], ln2_b[l])
        kv_s.append(kv4)
        win_s.append(jnp.concatenate([cache_win[l], win2.astype(cache_win.dtype)], axis=1)[:, DEC_SEQ:])
        conv_s.append(u_ext[:, DEC_SEQ:])

    return (xp, xs, jnp.stack(kv_p), jnp.stack(kv_s), jnp.stack(win_p), jnp.stack(win_s), jnp.stack(conv_p), jnp.stack(conv_s))
```

```python
import functools

import jax
import jax.numpy as jnp
from jax import lax
from jax.experimental import pallas as pl
from jax.experimental.pallas import tpu as pltpu

D_MODEL = 1024
SEQ = 16384
DEC_BATCH = 128
DEC_SEQ = 8
PAST_LEN = 8192
PAGE_SIZE = 128
N_HEADS = 16
HEAD_DIM = 64
N_KV_HEADS = 4
KV_WIDTH = N_KV_HEADS * HEAD_DIM
Q_WIDTH = N_HEADS * HEAD_DIM
N_KV_SETS = 6
N_NSA_BRANCH = 3
CMP_LEN = 32
CMP_STRIDE = 16
PHI_HIDDEN = 2 * HEAD_DIM
SEL_LEN = 64
N_SEL = 16
N_LOCAL = 2
WINDOW = 512
D_CONV = D_MODEL // 2
CONV_WIDTH = 31
N_EXPERTS = 32
TOP_K = 4
D_EXPERT = D_MODEL
SWIGLU_LIMIT = 7.0
SWIGLU_ALPHA = 1.702
DN_ALPHA = 2.0 ** 0.25
LN_EPS = 1e-5
NEG_INF = -1e30
FORCE_SCORE = 1e9
OFF_KV = Q_WIDTH
OFF_GLU = OFF_KV + N_KV_SETS * KV_WIDTH
OFF_NSA_G = OFF_GLU + 2 * D_CONV
OFF_MERGE = OFF_NSA_G + N_HEADS * N_NSA_BRANCH

LANES = 128
GATE_PAD = LANES
CONV_HALO = 32
VMEM_LIMIT = 56 * 1024 * 1024

F32 = jnp.float32
BF16 = jnp.bfloat16


def _params(*sem):
    return pltpu.CompilerParams(dimension_semantics=sem, vmem_limit_bytes=VMEM_LIMIT)


def _dot(a, b):
    return jnp.dot(a, b, preferred_element_type=F32)


def _dot_nt(a, b):
    return lax.dot_general(a, b, (((1,), (1,)), ((), ())), preferred_element_type=F32)


def _dot_tn(a, b):
    return lax.dot_general(a, b, (((0,), (0,)), ((), ())), preferred_element_type=F32)


def _split(x):
    hi = x.astype(BF16)
    lo = (x - hi.astype(F32)).astype(BF16)
    return hi, lo


def _dot3(a, b, dot=_dot):
    ah, al = _split(a)
    bh, bl = _split(b)
    return dot(ah, bh) + (dot(ah, bl) + dot(al, bh))


def _sigmoid(x):
    return 1.0 / (1.0 + jnp.exp(-x))


def _layer_norm(x, g, b):
    mu = jnp.mean(x, axis=-1, keepdims=True)
    xc = x - mu
    var = jnp.mean(xc * xc, axis=-1, keepdims=True)
    return xc * lax.rsqrt(var + LN_EPS) * g + b


def _const_spec(shape):
    return pl.BlockSpec(shape, lambda *_: (0,) * len(shape))


def _ada_kernel(c_ref, w_ref, b_ref, o_ref):
    o_ref[...] = _dot3(c_ref[...], w_ref[...]) + b_ref[...]


def _adaln(c_all, w_ada, b_ada):
    n = c_all.shape[0]
    tn = 1536
    return pl.pallas_call(
        _ada_kernel,
        grid=(6 * D_MODEL // tn,),
        in_specs=[pl.BlockSpec((n, D_MODEL), lambda j: (0, 0)),
                  pl.BlockSpec((D_MODEL, tn), lambda j: (0, j)),
                  pl.BlockSpec((1, tn), lambda j: (0, j))],
        out_specs=pl.BlockSpec((n, tn), lambda j: (0, j)),
        out_shape=jax.ShapeDtypeStruct((n, 6 * D_MODEL), F32),
        compiler_params=_params("parallel"),
    )(c_all, w_ada, b_ada.reshape(1, -1))


def _proj_kernel(x_ref, sc_ref, sh_ref, wq_ref, wkv_ref, wglu_ref, wg_ref, wm_ref,
                 q_ref, kv_ref, win_ref, kvb_ref, u_ref, g_ref, m_ref):
    h = (x_ref[...] * (1.0 + sc_ref[...]) + sh_ref[...]).astype(BF16)
    q_ref[...] = _dot(h, wq_ref[...]).astype(BF16)
    kv = _dot(h, wkv_ref[...])
    kv_ref[...] = kv[:, :4 * KV_WIDTH]
    win_ref[...] = kv[:, 4 * KV_WIDTH:]
    kvb_ref[...] = kv.astype(BF16)
    glu = _dot(h, wglu_ref[...])
    u_ref[...] = glu[:, :D_CONV] * _sigmoid(glu[:, D_CONV:])
    g_ref[...] = _sigmoid(_dot(h, wg_ref[...]))
    m_ref[...] = _sigmoid(_dot(h, wm_ref[...]))


def _project(x, sc, sh, wts, tm):
    n = x.shape[0]
    per_row = sc.shape[0] != 1
    mod_rows = tm if per_row else 1

    def mod_spec(col):
        return pl.BlockSpec((mod_rows, D_MODEL), (lambda i: (i, col)) if per_row else (lambda i: (0, col)))

    widths = (Q_WIDTH, 4 * KV_WIDTH, 2 * KV_WIDTH, N_KV_SETS * KV_WIDTH, D_CONV, GATE_PAD, 2 * D_MODEL)
    dtypes = (BF16, F32, F32, BF16, F32, F32, F32)
    return pl.pallas_call(
        _proj_kernel,
        grid=(n // tm,),
        in_specs=[pl.BlockSpec((tm, D_MODEL), lambda i: (i, 0)), mod_spec(1), mod_spec(0)]
                 + [_const_spec(w.shape) for w in wts],
        out_specs=[pl.BlockSpec((tm, w), lambda i: (i, 0)) for w in widths],
        out_shape=[jax.ShapeDtypeStruct((n, w), d) for w, d in zip(widths, dtypes)],
        compiler_params=_params("parallel"),
    )(x, sc, sh, *wts)


def _conv_kernel(cur_ref, halo_ref, dw_ref, dwb_ref, lg_ref, lb_ref, wpw_ref, o_ref, ext_ref):
    bb, tm, _ = cur_ref.shape
    ext_ref[:, :CONV_HALO, :] = halo_ref[...]
    ext_ref[:, CONV_HALO:, :] = cur_ref[...]
    first = CONV_HALO - (CONV_WIDTH - 1)
    acc = jnp.zeros((bb, tm, D_CONV), F32)
    for j in range(CONV_WIDTH):
        acc = acc + ext_ref[:, first + j:first + j + tm, :] * dw_ref[j:j + 1, :]
    y = _layer_norm(acc + dwb_ref[...], lg_ref[...], lb_ref[...])
    y = (y * _sigmoid(y)).reshape(bb * tm, D_CONV).astype(BF16)
    o_ref[...] = _dot(y, wpw_ref[...]).reshape(bb, tm, D_MODEL)


def _conv_branch(cur, halo, dw_w, dw_b, ln_g, ln_b, w_pw, bb):
    b, tm, _ = cur.shape
    return pl.pallas_call(
        _conv_kernel,
        grid=(b // bb,),
        in_specs=[pl.BlockSpec((bb, tm, D_CONV), lambda i: (i, 0, 0)),
                  pl.BlockSpec((bb, CONV_HALO, D_CONV), lambda i: (i, 0, 0)),
                  _const_spec((CONV_WIDTH, D_CONV)), _const_spec((1, D_CONV)),
                  _const_spec((1, D_CONV)), _const_spec((1, D_CONV)),
                  _const_spec((D_CONV, D_MODEL))],
        out_specs=pl.BlockSpec((bb, tm, D_MODEL), lambda i: (i, 0, 0)),
        out_shape=jax.ShapeDtypeStruct((b, tm, D_MODEL), F32),
        scratch_shapes=[pltpu.VMEM((bb, CONV_HALO + tm, D_CONV), F32)],
        compiler_params=_params("parallel"),
    )(cur, halo, dw_w, dw_b.reshape(1, -1), ln_g.reshape(1, -1), ln_b.reshape(1, -1), w_pw)


MOE_TILE = 512
MOE_CHUNK = 128


def _merge_kernel(x_ref, cy_ref, oc_ref, os_ref, ow_ref, g_ref, mg_ref, g1_ref, sc2_ref, sh2_ref,
                  exp_ref, wo_ref, lg_ref, lb_ref, wr_ref, br_ref,
                  x1_ref, h2_ref, wt_ref, rank_ref, cnt_ref):
    tm = x_ref.shape[0]
    gh, gl = _split(g_ref[...])
    nsa = jnp.zeros((tm, D_MODEL), F32)
    for n, o_ref in enumerate((oc_ref, os_ref, ow_ref)):
        e = exp_ref[n]
        nsa = nsa + o_ref[...] * (_dot(gh, e) + _dot(gl, e))
    mix = mg_ref[:, :D_MODEL] * cy_ref[...] + mg_ref[:, D_MODEL:] * nsa
    y = _dot(mix.astype(BF16), wo_ref[...])
    x1 = _layer_norm(DN_ALPHA * x_ref[...] + g1_ref[...] * y, lg_ref[...], lb_ref[...])
    x1_ref[...] = x1
    h2 = x1 * (1.0 + sc2_ref[...]) + sh2_ref[...]
    h2_ref[...] = h2.astype(BF16)
    logits = _dot3(wr_ref[...], h2, _dot_nt) + br_ref[...]
    eidx = lax.broadcasted_iota(jnp.int32, logits.shape, 0)
    picked = jnp.zeros(logits.shape, F32)
    wsum = jnp.zeros((1, tm), F32)
    wts = jnp.zeros(logits.shape, F32)
    v0 = None
    for k in range(TOP_K):
        v = jnp.max(logits, axis=0, keepdims=True)
        first = jnp.min(jnp.where(logits == v, eidx, N_EXPERTS), axis=0, keepdims=True)
        hit = eidx == first
        if k == 0:
            v0 = v
        ev = jnp.exp(v - v0)
        wts = wts + jnp.where(hit, ev, 0.0)
        wsum = wsum + ev
        picked = picked + jnp.where(hit, 1.0, 0.0)
        logits = jnp.where(hit, -jnp.inf, logits)
    wt_ref[0] = wts / wsum
    r = lax.broadcasted_iota(jnp.int32, (tm, tm), 0)
    c = lax.broadcasted_iota(jnp.int32, (tm, tm), 1)
    upper = jnp.where(r < c, 1.0, 0.0).astype(BF16)
    rank_ref[0] = _dot(picked.astype(BF16), upper)
    cnt = jnp.sum(picked, axis=1, keepdims=True)
    cnt_ref[0] = jnp.broadcast_to(cnt, (N_EXPERTS, LANES)).astype(jnp.int32)


def _merge(x, conv_y, o_cmp, o_slc, o_win, g_nsa, merge_g, mod, gate_expand, w_out, ln_g, ln_b,
           w_router_t, b_router):
    n = x.shape[0]
    tm = MOE_TILE
    nt = n // tm
    per_row = mod.shape[0] != 1

    def row_spec(w):
        return pl.BlockSpec((tm, w), lambda i: (i, 0))

    def mod_spec(col):
        return pl.BlockSpec((tm if per_row else 1, D_MODEL),
                            (lambda i: (i, col)) if per_row else (lambda i: (0, col)))

    tile_spec = pl.BlockSpec((1, N_EXPERTS, tm), lambda i: (i, 0, 0))
    return pl.pallas_call(
        _merge_kernel,
        grid=(nt,),
        in_specs=[row_spec(D_MODEL)] * 5 + [row_spec(GATE_PAD), row_spec(2 * D_MODEL),
                  mod_spec(2), mod_spec(4), mod_spec(3),
                  _const_spec(gate_expand.shape), _const_spec(w_out.shape),
                  _const_spec((1, D_MODEL)), _const_spec((1, D_MODEL)),
                  _const_spec(w_router_t.shape), _const_spec((N_EXPERTS, 1))],
        out_specs=[row_spec(D_MODEL), row_spec(D_MODEL), tile_spec, tile_spec,
                   pl.BlockSpec((1, N_EXPERTS, LANES), lambda i: (i, 0, 0))],
        out_shape=[jax.ShapeDtypeStruct((n, D_MODEL), F32), jax.ShapeDtypeStruct((n, D_MODEL), BF16),
                   jax.ShapeDtypeStruct((nt, N_EXPERTS, tm), F32),
                   jax.ShapeDtypeStruct((nt, N_EXPERTS, tm), F32),
                   jax.ShapeDtypeStruct((nt, N_EXPERTS, LANES), jnp.int32)],
        compiler_params=_params("parallel"),
    )(x, conv_y, o_cmp, o_slc, o_win, g_nsa, merge_g, mod, mod, mod, gate_expand, w_out,
      ln_g.reshape(1, -1), ln_b.reshape(1, -1), w_router_t, b_router.reshape(-1, 1))


def _moe_kernel(cnt_ref, x1_ref, h2_ref, wt_ref, rank_ref, g2_ref, wup_ref, bup_ref, wdn_ref, bdn_ref,
                lg_ref, lb_ref, o_ref, acc_ref):
    i = pl.program_id(0)
    e = pl.program_id(1)
    tm = x1_ref.shape[0]

    @pl.when(e == 0)
    def _():
        acc_ref[...] = jnp.zeros_like(acc_ref)

    cnt = cnt_ref[i * N_EXPERTS + e]
    for ch in range(tm // MOE_CHUNK):
        @pl.when(cnt > ch * MOE_CHUNK)
        def _():
            w_e = wt_ref[0, pl.ds(e, 1), :]
            slot = rank_ref[0, pl.ds(e, 1), :] - float(ch * MOE_CHUNK)
            rows = lax.broadcasted_iota(jnp.int32, (MOE_CHUNK, tm), 0).astype(F32)
            onehot = jnp.where((rows == slot) & (w_e > 0.0), 1.0, 0.0)
            wc = jnp.sum(onehot * w_e, axis=1, keepdims=True)
            sel = onehot.astype(BF16)
            xc = _dot(sel, h2_ref[...]).astype(BF16)
            u = _dot(xc, wup_ref[0]) + bup_ref[0]
            x_glu = jnp.minimum(u[:, :D_EXPERT], SWIGLU_LIMIT)
            x_lin = jnp.clip(u[:, D_EXPERT:], -SWIGLU_LIMIT, SWIGLU_LIMIT)
            a = x_glu * _sigmoid(SWIGLU_ALPHA * x_glu) * (x_lin + 1.0)
            y = (_dot(a.astype(BF16), wdn_ref[0]) + bdn_ref[0]) * wc
            yh, yl = _split(y)
            acc_ref[...] += _dot_tn(sel, yh) + _dot_tn(sel, yl)

    @pl.when(e == N_EXPERTS - 1)
    def _():
        o_ref[...] = _layer_norm(DN_ALPHA * x1_ref[...] + g2_ref[...] * acc_ref[...],
                                 lg_ref[...], lb_ref[...])


def _moe(x1, h2, wt, rank, cnt, mod, w_up, b_up, w_down, b_down, ln_g, ln_b):
    n = x1.shape[0]
    tm = MOE_TILE
    nt = n // tm
    per_row = mod.shape[0] != 1
    g2_spec = pl.BlockSpec((tm if per_row else 1, D_MODEL),
                           (lambda i, e, c: (i, 5)) if per_row else (lambda i, e, c: (0, 5)))
    tile_spec = pl.BlockSpec((1, N_EXPERTS, tm), lambda i, e, c: (i, 0, 0))
    grid_spec = pltpu.PrefetchScalarGridSpec(
        num_scalar_prefetch=1,
        grid=(nt, N_EXPERTS),
        in_specs=[pl.BlockSpec((tm, D_MODEL), lambda i, e, c: (i, 0)),
                  pl.BlockSpec((tm, D_MODEL), lambda i, e, c: (i, 0)),
                  tile_spec, tile_spec, g2_spec,
                  pl.BlockSpec((1, D_MODEL, 2 * D_EXPERT), lambda i, e, c: (e, 0, 0)),
                  pl.BlockSpec((1, 1, 2 * D_EXPERT), lambda i, e, c: (e, 0, 0)),
                  pl.BlockSpec((1, D_EXPERT, D_MODEL), lambda i, e, c: (e, 0, 0)),
                  pl.BlockSpec((1, 1, D_MODEL), lambda i, e, c: (e, 0, 0)),
                  pl.BlockSpec((1, D_MODEL), lambda i, e, c: (0, 0)),
                  pl.BlockSpec((1, D_MODEL), lambda i, e, c: (0, 0))],
        out_specs=pl.BlockSpec((tm, D_MODEL), lambda i, e, c: (i, 0)),
        scratch_shapes=[pltpu.VMEM((tm, D_MODEL), F32)],
    )
    return pl.pallas_call(
        _moe_kernel,
        grid_spec=grid_spec,
        out_shape=jax.ShapeDtypeStruct((n, D_MODEL), F32),
        compiler_params=_params("parallel", "arbitrary"),
    )(cnt, x1, h2, wt, rank, mod, w_up, b_up.reshape(N_EXPERTS, 1, -1), w_down,
      b_down.reshape(N_EXPERTS, 1, -1), ln_g.reshape(1, -1), ln_b.reshape(1, -1))


def _masked_softmax(s, mask):
    p = jax.nn.softmax(jnp.where(mask, s, NEG_INF), axis=-1)
    return p * mask


def _compress_j(rows, pe, w1, w2):
    B, T = rows.shape[:2]
    n_ch = CMP_LEN // CMP_STRIDE
    nc = T // CMP_STRIDE - n_ch + 1
    ch = rows.reshape(B, T // CMP_STRIDE, CMP_STRIDE, N_KV_HEADS, HEAD_DIM)
    pe = pe.reshape(n_ch, CMP_STRIDE, 1, HEAD_DIM)
    w1 = w1.reshape(n_ch, CMP_STRIDE, HEAD_DIM, PHI_HIDDEN)
    hid = sum(jnp.einsum('bcsgd,sde->bcge', ch + pe[j], w1[j])[:, j:j + nc] for j in range(n_ch))
    return jax.nn.gelu(hid) @ w2


def _sel_blocks(rows):
    B, T = rows.shape[:2]
    return rows.reshape(B, T // SEL_LEN, SEL_LEN, N_KV_HEADS, HEAD_DIM).transpose(0, 3, 1, 2, 4)


def _nsa_attend_j(q, tq, g, kc, vc, ksb, vsb, kw, vw, pw, slopes):
    f32 = jnp.float32
    B, Q = q.shape[:2]
    R = N_HEADS // N_KV_HEADS
    qg = q.astype(f32).reshape(B, Q, N_KV_HEADS, R, HEAD_DIM) * (HEAD_DIM ** -0.5)
    m = slopes.reshape(1, N_KV_HEADS, R, 1, 1)
    tqf = tq.astype(f32)
    nc = kc.shape[1]
    ends = jnp.arange(nc, dtype=jnp.int32) * CMP_STRIDE + (CMP_LEN - 1)
    s = jnp.einsum('bqgrd,bngd->bgrqn', qg, kc.astype(f32)) - m * (tqf[:, None] - ends.astype(f32)[None, :])
    p_cmp = _masked_softmax(s, ends[None, :] <= tq[:, None])
    o_cmp = jnp.einsum('bgrqn,bngd->bqgrd', p_cmp, vc.astype(f32))
    ns = ksb.shape[2]
    ratio = SEL_LEN // CMP_STRIDE
    lead = CMP_LEN // CMP_STRIDE - 1
    imp = jnp.pad(p_cmp.sum(axis=2), ((0, 0), (0, 0), (0, 0), (lead, ratio * ns - nc)))
    score = sum(imp[..., k:k + ratio * ns:ratio] for k in range(ratio + lead))
    blk = jnp.arange(ns, dtype=jnp.int32)
    back = (tq // SEL_LEN)[:, None] - blk[None, :]
    forced = (blk[None, :] == 0) | ((back >= 0) & (back < N_LOCAL))
    score = jnp.where(forced, FORCE_SCORE, jnp.where(back >= 0, score, -1.0))
    n_sel = min(N_SEL, ns)
    _, idx = lax.top_k(score, n_sel)
    bi = jnp.arange(B)[:, None, None, None]
    gi = jnp.arange(N_KV_HEADS)[None, :, None, None]
    kg = ksb[bi, gi, idx].astype(f32).reshape(B, N_KV_HEADS, Q, n_sel * SEL_LEN, HEAD_DIM)
    vg = vsb[bi, gi, idx].astype(f32).reshape(B, N_KV_HEADS, Q, n_sel * SEL_LEN, HEAD_DIM)
    pos = (idx[..., None] * SEL_LEN + jnp.arange(SEL_LEN, dtype=jnp.int32)).reshape(B, N_KV_HEADS, Q, n_sel * SEL_LEN)
    dist = tq[None, None, :, None] - pos
    s = jnp.einsum('bqgrd,bgqkd->bgrqk', qg, kg) - m * dist[:, :, None].astype(f32)
    p = _masked_softmax(s, (dist >= 0)[:, :, None])
    o_slc = jnp.einsum('bgrqk,bgqkd->bqgrd', p, vg)
    dw = tq[:, None] - pw[None, :]
    s = jnp.einsum('bqgrd,bwgd->bgrqw', qg, kw.astype(f32)) - m * dw.astype(f32)
    p = _masked_softmax(s, (dw >= 0) & (dw < WINDOW) & (pw[None, :] >= 0))
    o_win = jnp.einsum('bgrqw,bwgd->bqgrd', p, vw.astype(f32))
    return tuple(o.reshape(B, Q, Q_WIDTH) for o in (o_cmp, o_slc, o_win))


def _nsa_prompt_j(q, kv4, win2, cmp_w, slopes):
    B, T = q.shape[:2]
    kc = _compress_j(kv4[:, :, 0], *cmp_w[:3])
    vc = _compress_j(kv4[:, :, 1], *cmp_w[3:])
    ksb = _sel_blocks(kv4[:, :, 2])
    vsb = _sel_blocks(kv4[:, :, 3])
    pad = ((0, 0), (WINDOW, 0), (0, 0), (0, 0))
    kw_pad = jnp.pad(win2[:, :, 0], pad)
    vw_pad = jnp.pad(win2[:, :, 1], pad)
    nqb = T // 128
    qb = q.reshape(B, nqb, 128, N_HEADS, HEAD_DIM).swapaxes(0, 1)
    starts = jnp.arange(nqb, dtype=jnp.int32) * 128

    def block(args):
        qi, s0 = args
        tq = s0 + jnp.arange(128, dtype=jnp.int32)
        pw = s0 - WINDOW + jnp.arange(WINDOW + 128, dtype=jnp.int32)
        kw = lax.dynamic_slice_in_dim(kw_pad, s0, WINDOW + 128, axis=1)
        vw = lax.dynamic_slice_in_dim(vw_pad, s0, WINDOW + 128, axis=1)
        return _nsa_attend_j(qi, tq, None, kc, vc, ksb, vsb, kw, vw, pw, slopes)

    o = lax.map(block, (qb, starts))
    return tuple(x.swapaxes(0, 1).reshape(B * T, Q_WIDTH) for x in o)


def _nsa_sample_j(q, kv4_new, win2_new, cache_kv_l, cache_win_l, page_table, cmp_w, slopes):
    n_new = q.shape[1]
    t_real = PAST_LEN + n_new
    t_pad = -(-t_real // SEL_LEN) * SEL_LEN
    wb = cache_win_l.shape[1]
    tq = PAST_LEN + jnp.arange(n_new, dtype=jnp.int32)
    pw = PAST_LEN - wb + jnp.arange(wb + n_new, dtype=jnp.int32)

    def one(args):
        qb, kvb, wnb, wbuf, pages = args
        past = cache_kv_l[pages].reshape(PAST_LEN, 4, N_KV_HEADS, HEAD_DIM)
        rows = jnp.concatenate([past, kvb.astype(past.dtype)], axis=0)
        rows = jnp.pad(rows, ((0, t_pad - t_real), (0, 0), (0, 0), (0, 0)))[None]
        kc = _compress_j(rows[:, :, 0], *cmp_w[:3])
        vc = _compress_j(rows[:, :, 1], *cmp_w[3:])
        wrows = jnp.concatenate([wbuf, wnb.astype(wbuf.dtype)], axis=0)[None]
        o = _nsa_attend_j(qb[None], tq, None, kc, vc, _sel_blocks(rows[:, :, 2]), _sel_blocks(rows[:, :, 3]),
                          wrows[:, :, 0], wrows[:, :, 1], pw, slopes)
        return tuple(x[0] for x in o)

    o = lax.map(one, (q, kv4_new, win2_new, cache_win_l, page_table))
    return tuple(x.reshape(-1, Q_WIDTH) for x in o)


def _gate_expand():
    rows = jnp.arange(GATE_PAD)[:, None]
    cols = jnp.arange(D_MODEL)[None, :]
    return jnp.stack([(rows == (cols // HEAD_DIM) * N_NSA_BRANCH + n) for n in range(N_NSA_BRANCH)]
                     ).astype(BF16)


def kernel(x_prompt, x_sample, c_prompt, c_sample, cache_kv, cache_win, state_conv, page_table, w_ada, b_ada, w_in, conv_dw_w, conv_dw_b, conv_ln_g, conv_ln_b, w_conv_out, cmp_pe_k, cmp_w1_k, cmp_w2_k, cmp_pe_v, cmp_w1_v, cmp_w2_v, w_out, ln1_g, ln1_b, w_router, b_router, w_up, b_up, w_down, b_down, ln2_g, ln2_b):
    l = 0
    slopes = 2.0 ** (-8.0 * jnp.arange(1, N_HEADS + 1, dtype=F32) / N_HEADS)
    n_p = SEQ
    n_s = DEC_BATCH * DEC_SEQ

    c_all = jnp.concatenate([c_prompt, c_sample, jnp.zeros((7, D_MODEL), F32)], axis=0)
    mod = _adaln(c_all, w_ada[l], b_ada[l])
    mod_p = mod[0:1]
    mod_s = jnp.repeat(mod[1:1 + DEC_BATCH], DEC_SEQ, axis=0)

    wi = w_in[l]
    wts = ((wi[:, :OFF_KV] * (HEAD_DIM ** -0.5)).astype(BF16),
           wi[:, OFF_KV:OFF_GLU].astype(BF16),
           wi[:, OFF_GLU:OFF_NSA_G].astype(BF16),
           jnp.pad(wi[:, OFF_NSA_G:OFF_MERGE], ((0, 0), (0, GATE_PAD - N_HEADS * N_NSA_BRANCH))).astype(BF16),
           wi[:, OFF_MERGE:].astype(BF16))
    gate_expand = _gate_expand()
    w_out_b = w_out[l].astype(BF16)
    w_pw_b = w_conv_out[l].astype(BF16)
    w_up_b = w_up[l].astype(BF16)
    w_down_b = w_down[l].astype(BF16)
    w_router_t = w_router[l].T
    cmp_w = (cmp_pe_k[l], cmp_w1_k[l], cmp_w2_k[l], cmp_pe_v[l], cmp_w1_v[l], cmp_w2_v[l])
    conv_w = (conv_dw_w[l], conv_dw_b[l], conv_ln_g[l], conv_ln_b[l], w_pw_b)

    def tail(x, conv_y, o3, g_nsa, merge_g, m):
        x1, h2, wt, rank, cnt = _merge(x, conv_y, *o3, g_nsa, merge_g, m, gate_expand, w_out_b,
                                       ln1_g[l], ln1_b[l], w_router_t, b_router[l])
        return _moe(x1, h2, wt, rank, cnt[:, :, 0].reshape(-1), m, w_up_b, b_up[l], w_down_b, b_down[l],
                    ln2_g[l], ln2_b[l])

    xp = x_prompt.reshape(n_p, D_MODEL)
    q, kv4, win2, kvb, u, g_nsa, merge_g = _project(xp, mod_p, mod_p, wts, 256)
    ct = 512
    u3 = u.reshape(n_p // ct, ct, D_CONV)
    halo = jnp.concatenate([jnp.zeros((1, CONV_HALO, D_CONV), F32), u3[:-1, ct - CONV_HALO:]], axis=0)
    conv_y = _conv_branch(u3, halo, *conv_w, 1).reshape(n_p, D_MODEL)
    o3 = _nsa_prompt_j(q.astype(F32).reshape(1, n_p, N_HEADS, HEAD_DIM) * 8.0,
                       kv4.reshape(1, n_p, 4, N_KV_HEADS, HEAD_DIM),
                       win2.reshape(1, n_p, 2, N_KV_HEADS, HEAD_DIM), cmp_w, slopes)
    y_p = tail(xp, conv_y, o3, g_nsa, merge_g, mod_p)
    out_kv_p = kv4.reshape(1, 1, n_p, 4, N_KV_HEADS, HEAD_DIM)
    out_win_p = win2[n_p - WINDOW:].reshape(1, 1, WINDOW, 2, N_KV_HEADS, HEAD_DIM)
    out_conv_p = u[n_p - (CONV_WIDTH - 1):].reshape(1, 1, CONV_WIDTH - 1, D_CONV)

    xs = x_sample.reshape(n_s, D_MODEL)
    q, kv4, win2, kvb, u, g_nsa, merge_g = _project(xs, mod_s, mod_s, wts, 256)
    u3 = u.reshape(DEC_BATCH, DEC_SEQ, D_CONV)
    st = state_conv[l]
    halo = jnp.concatenate([jnp.zeros((DEC_BATCH, CONV_HALO - (CONV_WIDTH - 1), D_CONV), F32), st], axis=1)
    conv_y = _conv_branch(u3, halo, *conv_w, DEC_BATCH).reshape(n_s, D_MODEL)
    o3 = _nsa_sample_j(q.astype(F32).reshape(DEC_BATCH, DEC_SEQ, N_HEADS, HEAD_DIM) * 8.0,
                       kv4.reshape(DEC_BATCH, DEC_SEQ, 4, N_KV_HEADS, HEAD_DIM),
                       win2.reshape(DEC_BATCH, DEC_SEQ, 2, N_KV_HEADS, HEAD_DIM),
                       cache_kv[l], cache_win[l], page_table, cmp_w, slopes)
    y_s = tail(xs, conv_y, o3, g_nsa, merge_g, mod_s)
    out_kv_s = kv4.reshape(1, DEC_BATCH, DEC_SEQ, 4, N_KV_HEADS, HEAD_DIM)
    win_new = win2.reshape(DEC_BATCH, DEC_SEQ, 2, N_KV_HEADS, HEAD_DIM)
    out_win_s = jnp.concatenate([cache_win[l], win_new], axis=1)[:, DEC_SEQ:][None]
    out_conv_s = jnp.concatenate([st, u3], axis=1)[:, DEC_SEQ:][None]

    return (y_p.reshape(1, n_p, D_MODEL), y_s.reshape(DEC_BATCH, DEC_SEQ, D_MODEL),
            out_kv_p, out_kv_s, out_win_p, out_win_s, out_conv_p, out_conv_s)
```

```python
import functools

import jax
import jax.numpy as jnp
from jax import lax
from jax.experimental import pallas as pl
from jax.experimental.pallas import tpu as pltpu

D_MODEL = 1024
SEQ = 16384
DEC_BATCH = 128
DEC_SEQ = 8
PAST_LEN = 8192
PAGE_SIZE = 128
N_PAGES = PAST_LEN // PAGE_SIZE
N_HEADS = 16
HEAD_DIM = 64
N_KV_HEADS = 4
GROUP = N_HEADS // N_KV_HEADS
KV_WIDTH = N_KV_HEADS * HEAD_DIM
Q_WIDTH = N_HEADS * HEAD_DIM
N_KV_SETS = 6
N_NSA_BRANCH = 3
CMP_LEN = 32
CMP_STRIDE = 16
PHI_HIDDEN = 2 * HEAD_DIM
SEL_LEN = 64
SEL_RATIO = SEL_LEN // CMP_STRIDE
N_SEL = 16
N_LOCAL = 2
WINDOW = 512
D_CONV = D_MODEL // 2
CONV_WIDTH = 31
N_EXPERTS = 32
TOP_K = 4
D_EXPERT = D_MODEL
SWIGLU_LIMIT = 7.0
SWIGLU_ALPHA = 1.702
DN_ALPHA = 2.0 ** 0.25
LN_EPS = 1e-5
NEG_INF = -1e30
FORCE_SCORE = 1e9
OFF_KV = Q_WIDTH
OFF_GLU = OFF_KV + N_KV_SETS * KV_WIDTH
OFF_NSA_G = OFF_GLU + 2 * D_CONV
OFF_MERGE = OFF_NSA_G + N_HEADS * N_NSA_BRANCH

LANES = 128
GATE_PAD = LANES
CONV_HALO = 32
VMEM_LIMIT = 56 * 1024 * 1024
Q_TILE = 128
KV_TILE = 256
SEL_LANES = 256
HEAD_PAIRS = KV_WIDTH // LANES
SLOPES = tuple(2.0 ** (-8.0 * (h + 1) / N_HEADS) for h in range(N_HEADS))

F32 = jnp.float32
BF16 = jnp.bfloat16


def _params(*sem):
    return pltpu.CompilerParams(dimension_semantics=sem, vmem_limit_bytes=VMEM_LIMIT)


def _dot(a, b):
    return jnp.dot(a, b, preferred_element_type=F32)


def _dot_nt(a, b):
    return lax.dot_general(a, b, (((1,), (1,)), ((), ())), preferred_element_type=F32)


def _dot_tn(a, b):
    return lax.dot_general(a, b, (((0,), (0,)), ((), ())), preferred_element_type=F32)


def _split(x):
    hi = x.astype(BF16)
    lo = (x - hi.astype(F32)).astype(BF16)
    return hi, lo


def _dot3(a, b, dot=_dot):
    ah, al = _split(a)
    bh, bl = _split(b)
    return dot(ah, bh) + (dot(ah, bl) + dot(al, bh))


def _sigmoid(x):
    return 1.0 / (1.0 + jnp.exp(-x))


def _layer_norm(x, g, b):
    mu = jnp.mean(x, axis=-1, keepdims=True)
    xc = x - mu
    var = jnp.mean(xc * xc, axis=-1, keepdims=True)
    return xc * lax.rsqrt(var + LN_EPS) * g + b


def _const_spec(shape):
    return pl.BlockSpec(shape, lambda *_: (0,) * len(shape))


def _iota(shape, axis):
    return lax.broadcasted_iota(jnp.int32, shape, axis)


def _ada_kernel(c_ref, w_ref, b_ref, o_ref):
    o_ref[...] = _dot3(c_ref[...], w_ref[...]) + b_ref[...]


def _adaln(c_all, w_ada, b_ada):
    n = c_all.shape[0]
    tn = 1536
    return pl.pallas_call(
        _ada_kernel,
        grid=(6 * D_MODEL // tn,),
        in_specs=[pl.BlockSpec((n, D_MODEL), lambda j: (0, 0)),
                  pl.BlockSpec((D_MODEL, tn), lambda j: (0, j)),
                  pl.BlockSpec((1, tn), lambda j: (0, j))],
        out_specs=pl.BlockSpec((n, tn), lambda j: (0, j)),
        out_shape=jax.ShapeDtypeStruct((n, 6 * D_MODEL), F32),
        compiler_params=_params("parallel"),
    )(c_all, w_ada, b_ada.reshape(1, -1))


def _proj_kernel(x_ref, sc_ref, sh_ref, wq_ref, wkv_ref, wglu_ref, wg_ref, wm_ref,
                 q_ref, kv_ref, win_ref, kvb_ref, u_ref, g_ref, m_ref):
    h = (x_ref[...] * (1.0 + sc_ref[...]) + sh_ref[...]).astype(BF16)
    q_ref[...] = _dot(h, wq_ref[...]).astype(BF16)
    kv = _dot(h, wkv_ref[...])
    kv_ref[...] = kv[:, :4 * KV_WIDTH]
    win_ref[...] = kv[:, 4 * KV_WIDTH:]
    kvb_ref[...] = kv.astype(BF16)
    glu = _dot(h, wglu_ref[...])
    u_ref[...] = glu[:, :D_CONV] * _sigmoid(glu[:, D_CONV:])
    g_ref[...] = _sigmoid(_dot(h, wg_ref[...]))
    m_ref[...] = _sigmoid(_dot(h, wm_ref[...]))


def _project(x, sc, sh, wts, tm):
    n = x.shape[0]
    per_row = sc.shape[0] != 1
    mod_rows = tm if per_row else 1

    def mod_spec(col):
        return pl.BlockSpec((mod_rows, D_MODEL), (lambda i: (i, col)) if per_row else (lambda i: (0, col)))

    widths = (Q_WIDTH, 4 * KV_WIDTH, 2 * KV_WIDTH, N_KV_SETS * KV_WIDTH, D_CONV, GATE_PAD, 2 * D_MODEL)
    dtypes = (BF16, F32, F32, BF16, F32, F32, F32)
    return pl.pallas_call(
        _proj_kernel,
        grid=(n // tm,),
        in_specs=[pl.BlockSpec((tm, D_MODEL), lambda i: (i, 0)), mod_spec(1), mod_spec(0)]
                 + [_const_spec(w.shape) for w in wts],
        out_specs=[pl.BlockSpec((tm, w), lambda i: (i, 0)) for w in widths],
        out_shape=[jax.ShapeDtypeStruct((n, w), d) for w, d in zip(widths, dtypes)],
        compiler_params=_params("parallel"),
    )(x, sc, sh, *wts)


def _conv_kernel(cur_ref, halo_ref, dw_ref, dwb_ref, lg_ref, lb_ref, wpw_ref, o_ref, ext_ref):
    bb, tm, _ = cur_ref.shape
    ext_ref[:, :CONV_HALO, :] = halo_ref[...]
    ext_ref[:, CONV_HALO:, :] = cur_ref[...]
    first = CONV_HALO - (CONV_WIDTH - 1)
    acc = jnp.zeros((bb, tm, D_CONV), F32)
    for j in range(CONV_WIDTH):
        acc = acc + ext_ref[:, first + j:first + j + tm, :] * dw_ref[j:j + 1, :]
    y = _layer_norm(acc + dwb_ref[...], lg_ref[...], lb_ref[...])
    y = (y * _sigmoid(y)).reshape(bb * tm, D_CONV).astype(BF16)
    o_ref[...] = _dot(y, wpw_ref[...]).reshape(bb, tm, D_MODEL)


def _conv_branch(cur, halo, dw_w, dw_b, ln_g, ln_b, w_pw, bb):
    b, tm, _ = cur.shape
    return pl.pallas_call(
        _conv_kernel,
        grid=(b // bb,),
        in_specs=[pl.BlockSpec((bb, tm, D_CONV), lambda i: (i, 0, 0)),
                  pl.BlockSpec((bb, CONV_HALO, D_CONV), lambda i: (i, 0, 0)),
                  _const_spec((CONV_WIDTH, D_CONV)), _const_spec((1, D_CONV)),
                  _const_spec((1, D_CONV)), _const_spec((1, D_CONV)),
                  _const_spec((D_CONV, D_MODEL))],
        out_specs=pl.BlockSpec((bb, tm, D_MODEL), lambda i: (i, 0, 0)),
        out_shape=jax.ShapeDtypeStruct((b, tm, D_MODEL), F32),
        scratch_shapes=[pltpu.VMEM((bb, CONV_HALO + tm, D_CONV), F32)],
        compiler_params=_params("parallel"),
    )(cur, halo, dw_w, dw_b.reshape(1, -1), ln_g.reshape(1, -1), ln_b.reshape(1, -1), w_pw)


MOE_TILE = 512
MOE_CHUNK = 128


def _merge_kernel(x_ref, cy_ref, oc_ref, os_ref, ow_ref, g_ref, mg_ref, g1_ref, sc2_ref, sh2_ref,
                  exp_ref, wo_ref, lg_ref, lb_ref, wr_ref, br_ref,
                  x1_ref, h2_ref, wt_ref, rank_ref, cnt_ref):
    tm = x_ref.shape[0]
    gh, gl = _split(g_ref[...])
    nsa = jnp.zeros((tm, D_MODEL), F32)
    for n, o_ref in enumerate((oc_ref, os_ref, ow_ref)):
        e = exp_ref[n]
        nsa = nsa + o_ref[...] * (_dot(gh, e) + _dot(gl, e))
    mix = mg_ref[:, :D_MODEL] * cy_ref[...] + mg_ref[:, D_MODEL:] * nsa
    y = _dot(mix.astype(BF16), wo_ref[...])
    x1 = _layer_norm(DN_ALPHA * x_ref[...] + g1_ref[...] * y, lg_ref[...], lb_ref[...])
    x1_ref[...] = x1
    h2 = x1 * (1.0 + sc2_ref[...]) + sh2_ref[...]
    h2_ref[...] = h2.astype(BF16)
    logits = _dot3(wr_ref[...], h2, _dot_nt) + br_ref[...]
    eidx = _iota(logits.shape, 0)
    picked = jnp.zeros(logits.shape, F32)
    wsum = jnp.zeros((1, tm), F32)
    wts = jnp.zeros(logits.shape, F32)
    v0 = None
    for k in range(TOP_K):
        v = jnp.max(logits, axis=0, keepdims=True)
        first = jnp.min(jnp.where(logits == v, eidx, N_EXPERTS), axis=0, keepdims=True)
        hit = eidx == first
        if k == 0:
            v0 = v
        ev = jnp.exp(v - v0)
        wts = wts + jnp.where(hit, ev, 0.0)
        wsum = wsum + ev
        picked = picked + jnp.where(hit, 1.0, 0.0)
        logits = jnp.where(hit, -jnp.inf, logits)
    wt_ref[0] = wts / wsum
    upper = jnp.where(_iota((tm, tm), 0) < _iota((tm, tm), 1), 1.0, 0.0).astype(BF16)
    rank_ref[0] = _dot(picked.astype(BF16), upper)
    cnt = jnp.sum(picked, axis=1, keepdims=True)
    cnt_ref[0] = jnp.broadcast_to(cnt, (N_EXPERTS, LANES)).astype(jnp.int32)


def _merge(x, conv_y, o_cmp, o_slc, o_win, g_nsa, merge_g, mod, gate_expand, w_out, ln_g, ln_b,
           w_router_t, b_router):
    n = x.shape[0]
    tm = MOE_TILE
    nt = n // tm
    per_row = mod.shape[0] != 1

    def row_spec(w):
        return pl.BlockSpec((tm, w), lambda i: (i, 0))

    def mod_spec(col):
        return pl.BlockSpec((tm if per_row else 1, D_MODEL),
                            (lambda i: (i, col)) if per_row else (lambda i: (0, col)))

    tile_spec = pl.BlockSpec((1, N_EXPERTS, tm), lambda i: (i, 0, 0))
    return pl.pallas_call(
        _merge_kernel,
        grid=(nt,),
        in_specs=[row_spec(D_MODEL)] * 5 + [row_spec(GATE_PAD), row_spec(2 * D_MODEL),
                  mod_spec(2), mod_spec(4), mod_spec(3),
                  _const_spec(gate_expand.shape), _const_spec(w_out.shape),
                  _const_spec((1, D_MODEL)), _const_spec((1, D_MODEL)),
                  _const_spec(w_router_t.shape), _const_spec((N_EXPERTS, 1))],
        out_specs=[row_spec(D_MODEL), row_spec(D_MODEL), tile_spec, tile_spec,
                   pl.BlockSpec((1, N_EXPERTS, LANES), lambda i: (i, 0, 0))],
        out_shape=[jax.ShapeDtypeStruct((n, D_MODEL), F32), jax.ShapeDtypeStruct((n, D_MODEL), BF16),
                   jax.ShapeDtypeStruct((nt, N_EXPERTS, tm), F32),
                   jax.ShapeDtypeStruct((nt, N_EXPERTS, tm), F32),
                   jax.ShapeDtypeStruct((nt, N_EXPERTS, LANES), jnp.int32)],
        compiler_params=_params("parallel"),
    )(x, conv_y, o_cmp, o_slc, o_win, g_nsa, merge_g, mod, mod, mod, gate_expand, w_out,
      ln_g.reshape(1, -1), ln_b.reshape(1, -1), w_router_t, b_router.reshape(-1, 1))


def _moe_kernel(cnt_ref, x1_ref, h2_ref, wt_ref, rank_ref, g2_ref, wup_ref, bup_ref, wdn_ref, bdn_ref,
                lg_ref, lb_ref, o_ref, acc_ref):
    i = pl.program_id(0)
    e = pl.program_id(1)
    tm = x1_ref.shape[0]

    @pl.when(e == 0)
    def _():
        acc_ref[...] = jnp.zeros_like(acc_ref)

    cnt = cnt_ref[i * N_EXPERTS + e]
    for ch in range(tm // MOE_CHUNK):
        @pl.when(cnt > ch * MOE_CHUNK)
        def _():
            w_e = wt_ref[0, pl.ds(e, 1), :]
            slot = rank_ref[0, pl.ds(e, 1), :] - float(ch * MOE_CHUNK)
            rows = _iota((MOE_CHUNK, tm), 0).astype(F32)
            onehot = jnp.where((rows == slot) & (w_e > 0.0), 1.0, 0.0)
            wc = jnp.sum(onehot * w_e, axis=1, keepdims=True)
            sel = onehot.astype(BF16)
            xc = _dot(sel, h2_ref[...]).astype(BF16)
            u = _dot(xc, wup_ref[0]) + bup_ref[0]
            x_glu = jnp.minimum(u[:, :D_EXPERT], SWIGLU_LIMIT)
            x_lin = jnp.clip(u[:, D_EXPERT:], -SWIGLU_LIMIT, SWIGLU_LIMIT)
            a = x_glu * _sigmoid(SWIGLU_ALPHA * x_glu) * (x_lin + 1.0)
            y = (_dot(a.astype(BF16), wdn_ref[0]) + bdn_ref[0]) * wc
            yh, yl = _split(y)
            acc_ref[...] += _dot_tn(sel, yh) + _dot_tn(sel, yl)

    @pl.when(e == N_EXPERTS - 1)
    def _():
        o_ref[...] = _layer_norm(DN_ALPHA * x1_ref[...] + g2_ref[...] * acc_ref[...],
                                 lg_ref[...], lb_ref[...])


def _moe(x1, h2, wt, rank, cnt, mod, w_up, b_up, w_down, b_down, ln_g, ln_b):
    n = x1.shape[0]
    tm = MOE_TILE
    nt = n // tm
    per_row = mod.shape[0] != 1
    g2_spec = pl.BlockSpec((tm if per_row else 1, D_MODEL),
                           (lambda i, e, c: (i, 5)) if per_row else (lambda i, e, c: (0, 5)))
    tile_spec = pl.BlockSpec((1, N_EXPERTS, tm), lambda i, e, c: (i, 0, 0))
    grid_spec = pltpu.PrefetchScalarGridSpec(
        num_scalar_prefetch=1,
        grid=(nt, N_EXPERTS),
        in_specs=[pl.BlockSpec((tm, D_MODEL), lambda i, e, c: (i, 0)),
                  pl.BlockSpec((tm, D_MODEL), lambda i, e, c: (i, 0)),
                  tile_spec, tile_spec, g2_spec,
                  pl.BlockSpec((1, D_MODEL, 2 * D_EXPERT), lambda i, e, c: (e, 0, 0)),
                  pl.BlockSpec((1, 1, 2 * D_EXPERT), lambda i, e, c: (e, 0, 0)),
                  pl.BlockSpec((1, D_EXPERT, D_MODEL), lambda i, e, c: (e, 0, 0)),
                  pl.BlockSpec((1, 1, D_MODEL), lambda i, e, c: (e, 0, 0)),
                  pl.BlockSpec((1, D_MODEL), lambda i, e, c: (0, 0)),
                  pl.BlockSpec((1, D_MODEL), lambda i, e, c: (0, 0))],
        out_specs=pl.BlockSpec((tm, D_MODEL), lambda i, e, c: (i, 0)),
        scratch_shapes=[pltpu.VMEM((tm, D_MODEL), F32)],
    )
    return pl.pallas_call(
        _moe_kernel,
        grid_spec=grid_spec,
        out_shape=jax.ShapeDtypeStruct((n, D_MODEL), F32),
        compiler_params=_params("parallel", "arbitrary"),
    )(cnt, x1, h2, wt, rank, mod, w_up, b_up.reshape(N_EXPERTS, 1, -1), w_down,
      b_down.reshape(N_EXPERTS, 1, -1), ln_g.reshape(1, -1), ln_b.reshape(1, -1))


def _softmax_rows(s, valid):
    s = jnp.where(valid, s, NEG_INF)
    m = jnp.max(s, axis=-1, keepdims=True)
    e = jnp.where(valid, jnp.exp(s - m), 0.0)
    l = jnp.sum(e, axis=-1, keepdims=True)
    return e / jnp.where(l > 0.0, l, 1.0)


def _flash_update(s, valid, v, m_old, l_old, acc_old):
    s = jnp.where(valid, s, NEG_INF)
    m_new = jnp.maximum(m_old, jnp.max(s, axis=-1, keepdims=True))
    p = jnp.where(valid, jnp.exp(s - m_new), 0.0)
    alpha = jnp.exp(m_old - m_new)
    l_new = alpha * l_old + jnp.sum(p, axis=-1, keepdims=True)
    acc_new = alpha * acc_old + _dot(p.astype(BF16), v)
    return m_new, l_new, acc_new


def _shift_right_one(x):
    return jnp.where(_iota(x.shape, 1) == 0, 0.0, pltpu.roll(x, 1, 1))


def _block_scores(imp, nj):
    parts = [imp[:, k * nj:(k + 1) * nj] for k in range(SEL_RATIO)]
    return (parts[0] + parts[1]) + (parts[2] + parts[3]) + _shift_right_one(parts[3])


def _select_blocks(score, qblk, n_blocks):
    j = _iota(score.shape, 1)
    back = qblk - j
    forced = (j == 0) | ((back >= 0) & (back < N_LOCAL))
    score = jnp.where(forced, FORCE_SCORE, jnp.where(back >= 0, score, -1.0))
    score = jnp.where(j < n_blocks, score, -jnp.inf)
    picked = jnp.zeros(score.shape, F32)
    for _ in range(N_SEL):
        m = jnp.max(score, axis=-1, keepdims=True)
        first = jnp.min(jnp.where(score == m, j, SEL_LANES), axis=-1, keepdims=True)
        hit = j == first
        picked = jnp.where(hit, 1.0, picked)
        score = jnp.where(hit, -jnp.inf, score)
    return picked


def _gelu_tanh(x):
    return x * (0.5 * (1.0 + jnp.tanh(0.7978845608028654 * (x + 0.044715 * (x * x * x)))))


def _compress_rows(buf_ref, pe_ref, w1_ref, w2_ref, nj):
    outs = []
    for pair in range(HEAD_PAIRS):
        acc = jnp.zeros((SEL_RATIO * nj, 2 * PHI_HIDDEN), F32)
        for s in range(CMP_LEN):
            parts = [buf_ref[pair, pl.ds(CMP_STRIDE * k + s, nj, stride=SEL_LEN), :] for k in range(SEL_RATIO)]
            x = (jnp.concatenate(parts, axis=0) + pe_ref[s:s + 1, :]).astype(BF16)
            acc = acc + _dot(x, w1_ref[s])
        outs.append(_dot(_gelu_tanh(acc).astype(BF16), w2_ref[...]))
    return jnp.concatenate(outs, axis=1)


CMP_TILE = 4096


def _cmp_prompt_kernel(x_ref, halo_ref, pe_ref, w1_ref, w2_ref, o_ref, buf_ref):
    t = pl.program_id(0)
    last = t == pl.num_programs(0) - 1
    for pair in range(HEAD_PAIRS):
        cols = slice(pair * LANES, (pair + 1) * LANES)
        buf_ref[pair, :CMP_TILE, :] = x_ref[:, cols]
        buf_ref[pair, CMP_TILE:, :] = jnp.where(last, 0.0, halo_ref[:, cols])
    nj = CMP_TILE // SEL_LEN
    out = _compress_rows(buf_ref, pe_ref, w1_ref, w2_ref, nj)
    o_ref[...] = out.reshape(SEL_RATIO, nj, KV_WIDTH).astype(BF16)


def _compress_prompt(kv4, kv_set, pe, w1, w2):
    t_len = kv4.shape[0]
    nt = t_len // CMP_TILE
    nj = CMP_TILE // SEL_LEN
    halo_blocks = CMP_TILE // SEL_LEN
    last_halo = t_len // SEL_LEN - 1
    out = pl.pallas_call(
        _cmp_prompt_kernel,
        grid=(nt,),
        in_specs=[pl.BlockSpec((CMP_TILE, KV_WIDTH), lambda t: (t, kv_set)),
                  pl.BlockSpec((SEL_LEN, KV_WIDTH),
                               lambda t: (jnp.minimum((t + 1) * halo_blocks, last_halo), kv_set)),
                  _const_spec(pe.shape), _const_spec(w1.shape), _const_spec(w2.shape)],
        out_specs=pl.BlockSpec((SEL_RATIO, nj, KV_WIDTH), lambda t: (0, t, 0)),
        out_shape=jax.ShapeDtypeStruct((SEL_RATIO, t_len // SEL_LEN, KV_WIDTH), BF16),
        scratch_shapes=[pltpu.VMEM((HEAD_PAIRS, CMP_TILE + SEL_LEN, LANES), F32)],
        compiler_params=_params("arbitrary"),
    )(kv4, kv4, pe, w1, w2)
    return out.reshape(t_len // CMP_STRIDE, KV_WIDTH)


def _cmp_attn_prompt_kernel(q_ref, kc_ref, vc_ref, o_ref, sel_ref):
    nc = kc_ref.shape[0]
    nj = nc // SEL_RATIO
    s0 = pl.program_id(0) * Q_TILE
    tq = s0 + _iota((Q_TILE, 1), 0)
    col = _iota((1, nc), 1)
    ends = (SEL_RATIO * (col % nj) + col // nj) * CMP_STRIDE + (CMP_LEN - 1)
    dist = (tq - ends).astype(F32)
    valid = ends <= tq
    for g in range(N_KV_HEADS):
        kc = kc_ref[:, g * HEAD_DIM:(g + 1) * HEAD_DIM]
        vc = vc_ref[:, g * HEAD_DIM:(g + 1) * HEAD_DIM]
        imp = jnp.zeros((Q_TILE, nc), F32)
        for r in range(GROUP):
            h = g * GROUP + r
            cols = slice(h * HEAD_DIM, (h + 1) * HEAD_DIM)
            p = _softmax_rows(_dot_nt(q_ref[:, cols], kc) - SLOPES[h] * dist, valid)
            o_ref[:, cols] = _dot(p.astype(BF16), vc)
            imp = imp + p
        picked = _select_blocks(_block_scores(imp, nj), tq // SEL_LEN, nj)
        sel_ref[g] = picked.astype(BF16)


def _cmp_attn_prompt(q, kc, vc):
    t_len = q.shape[0]
    return pl.pallas_call(
        _cmp_attn_prompt_kernel,
        grid=(t_len // Q_TILE,),
        in_specs=[pl.BlockSpec((Q_TILE, Q_WIDTH), lambda i: (i, 0)),
                  _const_spec(kc.shape), _const_spec(vc.shape)],
        out_specs=[pl.BlockSpec((Q_TILE, Q_WIDTH), lambda i: (i, 0)),
                   pl.BlockSpec((N_KV_HEADS, Q_TILE, SEL_LANES), lambda i: (0, i, 0))],
        out_shape=[jax.ShapeDtypeStruct((t_len, Q_WIDTH), F32),
                   jax.ShapeDtypeStruct((N_KV_HEADS, t_len, SEL_LANES), BF16)],
        compiler_params=_params("parallel"),
    )(q, kc, vc)


def _slc_attn_prompt_kernel(q_ref, k_ref, v_ref, sel_ref, o_ref, m_ref, l_ref, acc_ref):
    i = pl.program_id(0)
    tq = i * Q_TILE + _iota((Q_TILE, 1), 0)
    m_ref[...] = jnp.full(m_ref.shape, NEG_INF, F32)
    l_ref[...] = jnp.zeros(l_ref.shape, F32)
    acc_ref[...] = jnp.zeros(acc_ref.shape, F32)

    def tile(t, carry):
        base = pl.multiple_of(t * KV_TILE, KV_TILE)
        dist_i = tq - (base + _iota((1, KV_TILE), 1))
        dist = dist_i.astype(F32)
        causal = dist_i >= 0
        expand = jnp.where(_iota((SEL_LANES, KV_TILE), 0)
                           == t * SEL_RATIO + _iota((SEL_LANES, KV_TILE), 1) // SEL_LEN, 1.0, 0.0).astype(BF16)
        for g in range(N_KV_HEADS):
            valid = (_dot(sel_ref[g], expand) > 0.5) & causal
            kt = k_ref[pl.ds(base, KV_TILE), g * HEAD_DIM:(g + 1) * HEAD_DIM]
            vt = v_ref[pl.ds(base, KV_TILE), g * HEAD_DIM:(g + 1) * HEAD_DIM]
            for r in range(GROUP):
                h = g * GROUP + r
                s = _dot_nt(q_ref[:, h * HEAD_DIM:(h + 1) * HEAD_DIM], kt) - SLOPES[h] * dist
                m_ref[h], l_ref[h], acc_ref[h] = _flash_update(s, valid, vt, m_ref[h], l_ref[h], acc_ref[h])
        return carry

    lax.fori_loop(0, (i * Q_TILE) // KV_TILE + 1, tile, 0)
    for h in range(N_HEADS):
        l = l_ref[h]
        o_ref[:, h * HEAD_DIM:(h + 1) * HEAD_DIM] = acc_ref[h] / jnp.where(l > 0.0, l, 1.0)


def _slc_attn_prompt(q, kvb, sel):
    t_len = q.shape[0]
    resident = functools.partial(pl.BlockSpec, pipeline_mode=pl.Buffered(1))
    return pl.pallas_call(
        _slc_attn_prompt_kernel,
        grid=(t_len // Q_TILE,),
        in_specs=[pl.BlockSpec((Q_TILE, Q_WIDTH), lambda i: (i, 0)),
                  resident((t_len, KV_WIDTH), lambda i: (0, 2)),
                  resident((t_len, KV_WIDTH), lambda i: (0, 3)),
                  pl.BlockSpec((N_KV_HEADS, Q_TILE, SEL_LANES), lambda i: (0, i, 0))],
        out_specs=pl.BlockSpec((Q_TILE, Q_WIDTH), lambda i: (i, 0)),
        out_shape=jax.ShapeDtypeStruct((t_len, Q_WIDTH), F32),
        scratch_shapes=[pltpu.VMEM((N_HEADS, Q_TILE, 1), F32), pltpu.VMEM((N_HEADS, Q_TILE, 1), F32),
                        pltpu.VMEM((N_HEADS, Q_TILE, HEAD_DIM), F32)],
        compiler_params=_params("parallel"),
    )(q, kvb, kvb, sel)


def _win_attn_prompt_kernel(q_ref, k_ref, v_ref, o_ref):
    s0 = pl.multiple_of(pl.program_id(0) * Q_TILE, Q_TILE)
    n_keys = WINDOW + Q_TILE
    tq = s0 + _iota((Q_TILE, 1), 0)
    pw = s0 - WINDOW + _iota((1, n_keys), 1)
    dw = tq - pw
    valid = (dw >= 0) & (dw < WINDOW) & (pw >= 0)
    dist = dw.astype(F32)
    for g in range(N_KV_HEADS):
        kw = k_ref[pl.ds(s0, n_keys), g * HEAD_DIM:(g + 1) * HEAD_DIM]
        vw = v_ref[pl.ds(s0, n_keys), g * HEAD_DIM:(g + 1) * HEAD_DIM]
        for r in range(GROUP):
            h = g * GROUP + r
            cols = slice(h * HEAD_DIM, (h + 1) * HEAD_DIM)
            p = _softmax_rows(_dot_nt(q_ref[:, cols], kw) - SLOPES[h] * dist, valid)
            o_ref[:, cols] = _dot(p.astype(BF16), vw)


def _win_attn_prompt(q, kw_pad, vw_pad):
    t_len = q.shape[0]
    resident = functools.partial(pl.BlockSpec, pipeline_mode=pl.Buffered(1))
    return pl.pallas_call(
        _win_attn_prompt_kernel,
        grid=(t_len // Q_TILE,),
        in_specs=[pl.BlockSpec((Q_TILE, Q_WIDTH), lambda i: (i, 0)),
                  resident(kw_pad.shape, lambda i: (0, 0)), resident(vw_pad.shape, lambda i: (0, 0))],
        out_specs=pl.BlockSpec((Q_TILE, Q_WIDTH), lambda i: (i, 0)),
        out_shape=jax.ShapeDtypeStruct((t_len, Q_WIDTH), F32),
        compiler_params=_params("parallel"),
    )(q, kw_pad, vw_pad)


S_ROWS = N_HEADS * DEC_SEQ
S_TAIL = SEL_LEN
S_CMP_BLOCKS = PAST_LEN // CMP_STRIDE
S_CHUNK = 2048


def _page_copies(cache_ref, pt_ref, sem_ref, targets, b, p, slot):
    page = pt_ref[b * N_PAGES + p]
    return [pltpu.make_async_copy(cache_ref.at[page, :, pl.ds(col0, dst.shape[-1])],
                                  dst.at[pl.ds(p * PAGE_SIZE, PAGE_SIZE), :], sem_ref.at[slot])
            for dst, col0 in targets(slot)]


def _fetch_pages(cache_ref, pt_ref, sem_ref, targets):
    b = pl.program_id(0)
    slot = b % 2

    def start(seq, dst_slot):
        def body(p, c):
            for cp in _page_copies(cache_ref, pt_ref, sem_ref, targets, seq, p, dst_slot):
                cp.start()
            return c
        lax.fori_loop(0, N_PAGES, body, 0)

    @pl.when(b == 0)
    def _():
        start(b, slot)

    @pl.when(b + 1 < pl.num_programs(0))
    def _():
        start(b + 1, 1 - slot)

    def wait(p, c):
        for cp in _page_copies(cache_ref, pt_ref, sem_ref, targets, b, p, slot):
            cp.wait()
        return c
    lax.fori_loop(0, N_PAGES, wait, 0)
    return slot


def _cmp_sample_kernel(kv_set, pt_ref, cache_ref, new_ref, pe_ref, w1_ref, w2_ref, o_ref, buf_ref, sem_ref):
    def targets(slot):
        return [(buf_ref.at[slot, pair], kv_set * KV_WIDTH + pair * LANES) for pair in range(HEAD_PAIRS)]

    slot = _fetch_pages(cache_ref, pt_ref, sem_ref, targets)
    rows = buf_ref.at[slot]
    for pair in range(HEAD_PAIRS):
        rows[pair, PAST_LEN:PAST_LEN + DEC_SEQ, :] = new_ref[:, pair * LANES:(pair + 1) * LANES]
        rows[pair, PAST_LEN + DEC_SEQ:, :] = jnp.zeros((S_TAIL - DEC_SEQ, LANES), F32)
    out = _compress_rows(rows, pe_ref, w1_ref, w2_ref, S_CMP_BLOCKS // SEL_RATIO)
    o_ref[0] = out.astype(BF16)


def _compress_sample(page_table, cache, kv4_new, kv_set, pe, w1, w2):
    grid_spec = pltpu.PrefetchScalarGridSpec(
        num_scalar_prefetch=1,
        grid=(DEC_BATCH,),
        in_specs=[pl.BlockSpec(memory_space=pl.ANY),
                  pl.BlockSpec((DEC_SEQ, KV_WIDTH), lambda b, pt: (b, kv_set)),
                  pl.BlockSpec(pe.shape, lambda b, pt: (0, 0)),
                  pl.BlockSpec(w1.shape, lambda b, pt: (0, 0, 0)),
                  pl.BlockSpec(w2.shape, lambda b, pt: (0, 0))],
        out_specs=pl.BlockSpec((1, S_CMP_BLOCKS, KV_WIDTH), lambda b, pt: (b, 0, 0)),
        scratch_shapes=[pltpu.VMEM((2, HEAD_PAIRS, PAST_LEN + S_TAIL, LANES), F32),
                        pltpu.SemaphoreType.DMA((2,))],
    )
    return pl.pallas_call(
        functools.partial(_cmp_sample_kernel, kv_set),
        grid_spec=grid_spec,
        out_shape=jax.ShapeDtypeStruct((DEC_BATCH, S_CMP_BLOCKS, KV_WIDTH), BF16),
        compiler_params=_params("arbitrary"),
    )(page_table.reshape(-1), cache, kv4_new, pe, w1, w2)


def _row_queries():
    return PAST_LEN + _iota((S_ROWS, 1), 0) % DEC_SEQ


def _cmp_attn_sample_kernel(q_ref, slope_ref, kc_ref, vc_ref, o_ref, sel_ref):
    nc = S_CMP_BLOCKS
    nj = nc // SEL_RATIO
    tq = _row_queries()
    col = _iota((1, nc), 1)
    ends = (SEL_RATIO * (col % nj) + col // nj) * CMP_STRIDE + (CMP_LEN - 1)
    s = _dot_nt(q_ref[0], kc_ref[0]) - slope_ref[...] * (tq - ends).astype(F32)
    p = _softmax_rows(s, ends <= tq)
    o_ref[0] = _dot(p.astype(BF16), vc_ref[0])
    rows_g = GROUP * DEC_SEQ
    n_blocks = (PAST_LEN + DEC_SEQ + SEL_LEN - 1) // SEL_LEN
    qblk = (PAST_LEN + _iota((DEC_SEQ, 1), 0)) // SEL_LEN
    picked = []
    for g in range(N_KV_HEADS):
        imp = jnp.zeros((DEC_SEQ, nc), F32)
        for r in range(GROUP):
            imp = imp + p[g * rows_g + r * DEC_SEQ:g * rows_g + (r + 1) * DEC_SEQ]
        score = jnp.concatenate([_block_scores(imp, nj), jnp.zeros((DEC_SEQ, SEL_LANES - nj), F32)], axis=1)
        picked += [_select_blocks(score, qblk, n_blocks)] * GROUP
    sel_ref[0] = jnp.concatenate(picked, axis=0).astype(BF16)


def _cmp_attn_sample(qbd, slope_col, kc, vc):
    seq_spec = lambda rows, w: pl.BlockSpec((1, rows, w), lambda b: (b, 0, 0))
    return pl.pallas_call(
        _cmp_attn_sample_kernel,
        grid=(DEC_BATCH,),
        in_specs=[seq_spec(S_ROWS, KV_WIDTH), _const_spec((S_ROWS, 1)),
                  seq_spec(S_CMP_BLOCKS, KV_WIDTH), seq_spec(S_CMP_BLOCKS, KV_WIDTH)],
        out_specs=[seq_spec(S_ROWS, KV_WIDTH), seq_spec(S_ROWS, SEL_LANES)],
        out_shape=[jax.ShapeDtypeStruct((DEC_BATCH, S_ROWS, KV_WIDTH), F32),
                   jax.ShapeDtypeStruct((DEC_BATCH, S_ROWS, SEL_LANES), BF16)],
        compiler_params=_params("parallel"),
    )(qbd, slope_col, kc, vc)


def _pad_new_rows(x):
    return jnp.concatenate([x, jnp.zeros((LANES - DEC_SEQ, x.shape[1]), F32)], axis=0).astype(BF16)


def _slc_win_sample_kernel(pt_ref, cache_ref, q_ref, slope_ref, sel_ref, new_ref, cwin_ref, wnew_ref,
                           os_ref, ow_ref, buf_ref, sem_ref):
    slot = _fetch_pages(cache_ref, pt_ref, sem_ref, lambda s: [(buf_ref.at[s], 2 * KV_WIDTH)])
    rows = buf_ref.at[slot]
    q = q_ref[0]
    slope = slope_ref[...]
    sel = sel_ref[0]
    tq = _row_queries()
    new_pos = PAST_LEN + _iota((1, LANES), 1)
    new_ok = (tq - new_pos >= 0) & (new_pos < PAST_LEN + DEC_SEQ)
    new_dist = (tq - new_pos).astype(F32)

    m = jnp.full((S_ROWS, 1), NEG_INF, F32)
    l = jnp.zeros((S_ROWS, 1), F32)
    acc = jnp.zeros((S_ROWS, KV_WIDTH), F32)
    for c in range(PAST_LEN // S_CHUNK):
        k = rows[c * S_CHUNK:(c + 1) * S_CHUNK, :KV_WIDTH].astype(BF16)
        v = rows[c * S_CHUNK:(c + 1) * S_CHUNK, KV_WIDTH:].astype(BF16)
        pos = c * S_CHUNK + _iota((1, S_CHUNK), 1)
        expand = jnp.where(_iota((SEL_LANES, S_CHUNK), 0)
                           == (c * S_CHUNK + _iota((SEL_LANES, S_CHUNK), 1)) // SEL_LEN, 1.0, 0.0).astype(BF16)
        valid = (_dot(sel, expand) > 0.5) & (tq - pos >= 0)
        s = _dot_nt(q, k) - slope * (tq - pos).astype(F32)
        m, l, acc = _flash_update(s, valid, v, m, l, acc)
    new = _pad_new_rows(new_ref[...])
    in_last = jnp.sum(jnp.where(_iota((S_ROWS, SEL_LANES), 1) == PAST_LEN // SEL_LEN, sel.astype(F32), 0.0),
                      axis=1, keepdims=True) > 0.5
    s = _dot_nt(q, new[:, :KV_WIDTH]) - slope * new_dist
    m, l, acc = _flash_update(s, in_last & new_ok, new[:, KV_WIDTH:], m, l, acc)
    os_ref[0] = acc / jnp.where(l > 0.0, l, 1.0)

    wb = cwin_ref.shape[1]
    pw = PAST_LEN - wb + _iota((1, wb), 1)
    dw = tq - pw
    s_past = _dot_nt(q, cwin_ref[0, :, :KV_WIDTH].astype(BF16)) - slope * dw.astype(F32)
    wnew = _pad_new_rows(wnew_ref[...])
    s_new = _dot_nt(q, wnew[:, :KV_WIDTH]) - slope * new_dist
    valid = jnp.concatenate([(dw >= 0) & (dw < WINDOW) & (pw >= 0), new_ok & (tq - new_pos < WINDOW)], axis=1)
    p = _softmax_rows(jnp.concatenate([s_past, s_new], axis=1), valid).astype(BF16)
    ow_ref[0] = _dot(p[:, :wb], cwin_ref[0, :, KV_WIDTH:].astype(BF16)) + _dot(p[:, wb:], wnew[:, KV_WIDTH:])


def _slc_win_sample(page_table, cache, qbd, slope_col, sel, kv4_new, cache_win, win_new):
    wb = cache_win.shape[1]
    seq_spec = lambda rows, w: pl.BlockSpec((1, rows, w), lambda b, pt: (b, 0, 0))
    grid_spec = pltpu.PrefetchScalarGridSpec(
        num_scalar_prefetch=1,
        grid=(DEC_BATCH,),
        in_specs=[pl.BlockSpec(memory_space=pl.ANY),
                  seq_spec(S_ROWS, KV_WIDTH),
                  pl.BlockSpec((S_ROWS, 1), lambda b, pt: (0, 0)),
                  seq_spec(S_ROWS, SEL_LANES),
                  pl.BlockSpec((DEC_SEQ, 2 * KV_WIDTH), lambda b, pt: (b, 1)),
                  seq_spec(wb, 2 * KV_WIDTH),
                  pl.BlockSpec((DEC_SEQ, 2 * KV_WIDTH), lambda b, pt: (b, 0))],
        out_specs=[seq_spec(S_ROWS, KV_WIDTH), seq_spec(S_ROWS, KV_WIDTH)],
        scratch_shapes=[pltpu.VMEM((2, PAST_LEN, 2 * KV_WIDTH), F32), pltpu.SemaphoreType.DMA((2,))],
    )
    return pl.pallas_call(
        _slc_win_sample_kernel,
        grid_spec=grid_spec,
        out_shape=[jax.ShapeDtypeStruct((DEC_BATCH, S_ROWS, KV_WIDTH), F32)] * 2,
        compiler_params=_params("arbitrary"),
    )(page_table.reshape(-1), cache, qbd, slope_col, sel, kv4_new, cache_win, win_new)


def _gate_expand():
    rows = jnp.arange(GATE_PAD)[:, None]
    cols = jnp.arange(D_MODEL)[None, :]
    return jnp.stack([(rows == (cols // HEAD_DIM) * N_NSA_BRANCH + n) for n in range(N_NSA_BRANCH)]
                     ).astype(BF16)


def _compress_weights(pe, w1, w2):
    per_tile = LANES // HEAD_DIM
    eye = jnp.eye(per_tile, dtype=F32)
    w1_big = jnp.einsum('sde,gh->sgdhe', w1, eye).reshape(CMP_LEN, LANES, per_tile * PHI_HIDDEN).astype(BF16)
    w2_big = jnp.einsum('ed,gh->gehd', w2, eye).reshape(per_tile * PHI_HIDDEN, LANES).astype(BF16)
    return jnp.tile(pe, (1, per_tile)), w1_big, w2_big


def _rows_from_heads(o):
    o = o.reshape(DEC_BATCH, N_KV_HEADS, GROUP, DEC_SEQ, N_KV_HEADS, HEAD_DIM)
    idx = jnp.arange(N_KV_HEADS)
    own = o[:, idx, :, :, idx]
    return own.transpose(1, 3, 0, 2, 4).reshape(DEC_BATCH * DEC_SEQ, Q_WIDTH)


def kernel(x_prompt, x_sample, c_prompt, c_sample, cache_kv, cache_win, state_conv, page_table, w_ada, b_ada, w_in, conv_dw_w, conv_dw_b, conv_ln_g, conv_ln_b, w_conv_out, cmp_pe_k, cmp_w1_k, cmp_w2_k, cmp_pe_v, cmp_w1_v, cmp_w2_v, w_out, ln1_g, ln1_b, w_router, b_router, w_up, b_up, w_down, b_down, ln2_g, ln2_b):
    l = 0
    n_p = SEQ
    n_s = DEC_BATCH * DEC_SEQ

    c_all = jnp.concatenate([c_prompt, c_sample, jnp.zeros((7, D_MODEL), F32)], axis=0)
    mod = _adaln(c_all, w_ada[l], b_ada[l])
    mod_p = mod[0:1]
    mod_s = jnp.repeat(mod[1:1 + DEC_BATCH], DEC_SEQ, axis=0)

    wi = w_in[l]
    wts = ((wi[:, :OFF_KV] * (HEAD_DIM ** -0.5)).astype(BF16),
           wi[:, OFF_KV:OFF_GLU].astype(BF16),
           wi[:, OFF_GLU:OFF_NSA_G].astype(BF16),
           jnp.pad(wi[:, OFF_NSA_G:OFF_MERGE], ((0, 0), (0, GATE_PAD - N_HEADS * N_NSA_BRANCH))).astype(BF16),
           wi[:, OFF_MERGE:].astype(BF16))
    gate_expand = _gate_expand()
    w_out_b = w_out[l].astype(BF16)
    w_pw_b = w_conv_out[l].astype(BF16)
    w_up_b = w_up[l].astype(BF16)
    w_down_b = w_down[l].astype(BF16)
    w_router_t = w_router[l].T
    cmp_k = _compress_weights(cmp_pe_k[l], cmp_w1_k[l], cmp_w2_k[l])
    cmp_v = _compress_weights(cmp_pe_v[l], cmp_w1_v[l], cmp_w2_v[l])
    conv_w = (conv_dw_w[l], conv_dw_b[l], conv_ln_g[l], conv_ln_b[l], w_pw_b)

    def tail(x, conv_y, o3, g_nsa, merge_g, m):
        x1, h2, wt, rank, cnt = _merge(x, conv_y, *o3, g_nsa, merge_g, m, gate_expand, w_out_b,
                                       ln1_g[l], ln1_b[l], w_router_t, b_router[l])
        return _moe(x1, h2, wt, rank, cnt[:, :, 0].reshape(-1), m, w_up_b, b_up[l], w_down_b, b_down[l],
                    ln2_g[l], ln2_b[l])

    xp = x_prompt.reshape(n_p, D_MODEL)
    q, kv4, win2, kvb, u, g_nsa, merge_g = _project(xp, mod_p, mod_p, wts, 256)
    ct = 512
    u3 = u.reshape(n_p // ct, ct, D_CONV)
    halo = jnp.concatenate([jnp.zeros((1, CONV_HALO, D_CONV), F32), u3[:-1, ct - CONV_HALO:]], axis=0)
    conv_y = _conv_branch(u3, halo, *conv_w, 1).reshape(n_p, D_MODEL)
    kc = _compress_prompt(kv4, 0, *cmp_k)
    vc = _compress_prompt(kv4, 1, *cmp_v)
    o_cmp, sel = _cmp_attn_prompt(q, kc, vc)
    o_slc = _slc_attn_prompt(q, kvb, sel)
    kw_pad = jnp.pad(kvb[:, 4 * KV_WIDTH:5 * KV_WIDTH], ((WINDOW, 0), (0, 0)))
    vw_pad = jnp.pad(kvb[:, 5 * KV_WIDTH:], ((WINDOW, 0), (0, 0)))
    o_win = _win_attn_prompt(q, kw_pad, vw_pad)
    y_p = tail(xp, conv_y, (o_cmp, o_slc, o_win), g_nsa, merge_g, mod_p)
    out_kv_p = kv4.reshape(1, 1, n_p, 4, N_KV_HEADS, HEAD_DIM)
    out_win_p = win2[n_p - WINDOW:].reshape(1, 1, WINDOW, 2, N_KV_HEADS, HEAD_DIM)
    out_conv_p = u[n_p - (CONV_WIDTH - 1):].reshape(1, 1, CONV_WIDTH - 1, D_CONV)

    xs = x_sample.reshape(n_s, D_MODEL)
    q, kv4, win2, kvb, u, g_nsa, merge_g = _project(xs, mod_s, mod_s, wts, 256)
    u3 = u.reshape(DEC_BATCH, DEC_SEQ, D_CONV)
    st = state_conv[l]
    halo = jnp.concatenate([jnp.zeros((DEC_BATCH, CONV_HALO - (CONV_WIDTH - 1), D_CONV), F32), st], axis=1)
    conv_y = _conv_branch(u3, halo, *conv_w, DEC_BATCH).reshape(n_s, D_MODEL)
    qh = q.reshape(DEC_BATCH, DEC_SEQ, N_KV_HEADS, GROUP, HEAD_DIM).transpose(0, 2, 3, 1, 4)
    qbd = jnp.einsum('bgrqd,gh->bgrqhd', qh, jnp.eye(N_KV_HEADS, dtype=BF16)).reshape(DEC_BATCH, S_ROWS, KV_WIDTH)
    slope_col = jnp.repeat(jnp.asarray(SLOPES, F32), DEC_SEQ).reshape(S_ROWS, 1)
    cache = cache_kv[l].reshape(-1, PAGE_SIZE, 4 * KV_WIDTH)
    cwin = cache_win[l].reshape(DEC_BATCH, -1, 2 * KV_WIDTH)
    kc = _compress_sample(page_table, cache, kv4, 0, *cmp_k)
    vc = _compress_sample(page_table, cache, kv4, 1, *cmp_v)
    o_cmp, sel = _cmp_attn_sample(qbd, slope_col, kc, vc)
    o_slc, o_win = _slc_win_sample(page_table, cache, qbd, slope_col, sel, kv4, cwin, win2)
    o3 = tuple(_rows_from_heads(o) for o in (o_cmp, o_slc, o_win))
    y_s = tail(xs, conv_y, o3, g_nsa, merge_g, mod_s)
    out_kv_s = kv4.reshape(1, DEC_BATCH, DEC_SEQ, 4, N_KV_HEADS, HEAD_DIM)
    win_new = win2.reshape(DEC_BATCH, DEC_SEQ, 2, N_KV_HEADS, HEAD_DIM)
    out_win_s = jnp.concatenate([cache_win[l], win_new], axis=1)[:, DEC_SEQ:][None]
    out_conv_s = jnp.concatenate([st, u3], axis=1)[:, DEC_SEQ:][None]

    return (y_p.reshape(1, n_p, D_MODEL), y_s.reshape(DEC_BATCH, DEC_SEQ, D_MODEL),
            out_kv_p, out_kv_s, out_win_p, out_win_s, out_conv_p, out_conv_s)
```

```python
import functools

import jax
import jax.numpy as jnp
from jax import lax
from jax.experimental import pallas as pl
from jax.experimental.pallas import tpu as pltpu

D_MODEL = 1024
SEQ = 16384
DEC_BATCH = 128
DEC_SEQ = 8
PAST_LEN = 8192
PAGE_SIZE = 128
N_PAGES = PAST_LEN // PAGE_SIZE
N_HEADS = 16
HEAD_DIM = 64
N_KV_HEADS = 4
GROUP = N_HEADS // N_KV_HEADS
KV_WIDTH = N_KV_HEADS * HEAD_DIM
Q_WIDTH = N_HEADS * HEAD_DIM
N_KV_SETS = 6
N_NSA_BRANCH = 3
CMP_LEN = 32
CMP_STRIDE = 16
PHI_HIDDEN = 2 * HEAD_DIM
SEL_LEN = 64
SEL_RATIO = SEL_LEN // CMP_STRIDE
N_SEL = 16
N_LOCAL = 2
WINDOW = 512
D_CONV = D_MODEL // 2
CONV_WIDTH = 31
N_EXPERTS = 32
TOP_K = 4
D_EXPERT = D_MODEL
SWIGLU_LIMIT = 7.0
SWIGLU_ALPHA = 1.702
DN_ALPHA = 2.0 ** 0.25
LN_EPS = 1e-5
NEG_INF = -1e30
FORCE_SCORE = 1e9
OFF_KV = Q_WIDTH
OFF_GLU = OFF_KV + N_KV_SETS * KV_WIDTH
OFF_NSA_G = OFF_GLU + 2 * D_CONV
OFF_MERGE = OFF_NSA_G + N_HEADS * N_NSA_BRANCH

LANES = 128
GATE_PAD = LANES
CONV_HALO = 32
VMEM_LIMIT = 56 * 1024 * 1024
Q_TILE = 128
KV_TILE = 256
SEL_LANES = 256
HEAD_PAIRS = KV_WIDTH // LANES
ANY_ROWS = 8
FLAG_BITS = 32
FLAG_WORDS = SEQ // KV_TILE // FLAG_BITS
SLOPES = tuple(2.0 ** (-8.0 * (h + 1) / N_HEADS) for h in range(N_HEADS))

F32 = jnp.float32
BF16 = jnp.bfloat16


def _params(*sem):
    return pltpu.CompilerParams(dimension_semantics=sem, vmem_limit_bytes=VMEM_LIMIT)


def _dot(a, b):
    return jnp.dot(a, b, preferred_element_type=F32)


def _dot_nt(a, b):
    return lax.dot_general(a, b, (((1,), (1,)), ((), ())), preferred_element_type=F32)


def _dot_tn(a, b):
    return lax.dot_general(a, b, (((0,), (0,)), ((), ())), preferred_element_type=F32)


def _split(x):
    hi = x.astype(BF16)
    lo = (x - hi.astype(F32)).astype(BF16)
    return hi, lo


def _dot3(a, b, dot=_dot):
    ah, al = _split(a)
    bh, bl = _split(b)
    return dot(ah, bh) + (dot(ah, bl) + dot(al, bh))


def _sigmoid(x):
    return 1.0 / (1.0 + jnp.exp(-x))


def _layer_norm(x, g, b):
    mu = jnp.mean(x, axis=-1, keepdims=True)
    xc = x - mu
    var = jnp.mean(xc * xc, axis=-1, keepdims=True)
    return xc * lax.rsqrt(var + LN_EPS) * g + b


def _const_spec(shape):
    return pl.BlockSpec(shape, lambda *_: (0,) * len(shape))


def _iota(shape, axis):
    return lax.broadcasted_iota(jnp.int32, shape, axis)


def _ada_kernel(c_ref, w_ref, b_ref, o_ref):
    o_ref[...] = _dot3(c_ref[...], w_ref[...]) + b_ref[...]


def _adaln(c_all, w_ada, b_ada):
    n = c_all.shape[0]
    tn = 1536
    return pl.pallas_call(
        _ada_kernel,
        grid=(6 * D_MODEL // tn,),
        in_specs=[pl.BlockSpec((n, D_MODEL), lambda j: (0, 0)),
                  pl.BlockSpec((D_MODEL, tn), lambda j: (0, j)),
                  pl.BlockSpec((1, tn), lambda j: (0, j))],
        out_specs=pl.BlockSpec((n, tn), lambda j: (0, j)),
        out_shape=jax.ShapeDtypeStruct((n, 6 * D_MODEL), F32),
        compiler_params=_params("parallel"),
    )(c_all, w_ada, b_ada.reshape(1, -1))


def _proj_kernel(x_ref, sc_ref, sh_ref, wq_ref, wkv_ref, wglu_ref, wg_ref, wm_ref,
                 q_ref, kv_ref, win_ref, kvb_ref, u_ref, g_ref, m_ref):
    h = (x_ref[...] * (1.0 + sc_ref[...]) + sh_ref[...]).astype(BF16)
    q_ref[...] = _dot(h, wq_ref[...]).astype(BF16)
    kv = _dot(h, wkv_ref[...])
    kv_ref[...] = kv[:, :4 * KV_WIDTH]
    win_ref[...] = kv[:, 4 * KV_WIDTH:]
    kvb_ref[...] = kv.astype(BF16)
    glu = _dot(h, wglu_ref[...])
    u_ref[...] = glu[:, :D_CONV] * _sigmoid(glu[:, D_CONV:])
    g_ref[...] = _sigmoid(_dot(h, wg_ref[...]))
    m_ref[...] = _sigmoid(_dot(h, wm_ref[...]))


def _project(x, sc, sh, wts, tm):
    n = x.shape[0]
    per_row = sc.shape[0] != 1
    mod_rows = tm if per_row else 1

    def mod_spec(col):
        return pl.BlockSpec((mod_rows, D_MODEL), (lambda i: (i, col)) if per_row else (lambda i: (0, col)))

    widths = (Q_WIDTH, 4 * KV_WIDTH, 2 * KV_WIDTH, N_KV_SETS * KV_WIDTH, D_CONV, GATE_PAD, 2 * D_MODEL)
    dtypes = (BF16, F32, F32, BF16, F32, F32, F32)
    return pl.pallas_call(
        _proj_kernel,
        grid=(n // tm,),
        in_specs=[pl.BlockSpec((tm, D_MODEL), lambda i: (i, 0)), mod_spec(1), mod_spec(0)]
                 + [_const_spec(w.shape) for w in wts],
        out_specs=[pl.BlockSpec((tm, w), lambda i: (i, 0)) for w in widths],
        out_shape=[jax.ShapeDtypeStruct((n, w), d) for w, d in zip(widths, dtypes)],
        compiler_params=_params("parallel"),
    )(x, sc, sh, *wts)


def _conv_kernel(cur_ref, halo_ref, dw_ref, dwb_ref, lg_ref, lb_ref, wpw_ref, o_ref, ext_ref):
    bb, tm, _ = cur_ref.shape
    ext_ref[:, :CONV_HALO, :] = halo_ref[...]
    ext_ref[:, CONV_HALO:, :] = cur_ref[...]
    first = CONV_HALO - (CONV_WIDTH - 1)
    acc = jnp.zeros((bb, tm, D_CONV), F32)
    for j in range(CONV_WIDTH):
        acc = acc + ext_ref[:, first + j:first + j + tm, :] * dw_ref[j:j + 1, :]
    y = _layer_norm(acc + dwb_ref[...], lg_ref[...], lb_ref[...])
    y = (y * _sigmoid(y)).reshape(bb * tm, D_CONV).astype(BF16)
    o_ref[...] = _dot(y, wpw_ref[...]).reshape(bb, tm, D_MODEL)


def _conv_branch(cur, halo, dw_w, dw_b, ln_g, ln_b, w_pw, bb):
    b, tm, _ = cur.shape
    return pl.pallas_call(
        _conv_kernel,
        grid=(b // bb,),
        in_specs=[pl.BlockSpec((bb, tm, D_CONV), lambda i: (i, 0, 0)),
                  pl.BlockSpec((bb, CONV_HALO, D_CONV), lambda i: (i, 0, 0)),
                  _const_spec((CONV_WIDTH, D_CONV)), _const_spec((1, D_CONV)),
                  _const_spec((1, D_CONV)), _const_spec((1, D_CONV)),
                  _const_spec((D_CONV, D_MODEL))],
        out_specs=pl.BlockSpec((bb, tm, D_MODEL), lambda i: (i, 0, 0)),
        out_shape=jax.ShapeDtypeStruct((b, tm, D_MODEL), F32),
        scratch_shapes=[pltpu.VMEM((bb, CONV_HALO + tm, D_CONV), F32)],
        compiler_params=_params("parallel"),
    )(cur, halo, dw_w, dw_b.reshape(1, -1), ln_g.reshape(1, -1), ln_b.reshape(1, -1), w_pw)


MERGE_TILE = 512
MOE_TILE = 1024
MOE_CHUNK = 256


def _merge_kernel(x_ref, cy_ref, oc_ref, os_ref, ow_ref, g_ref, mg_ref, g1_ref, sc2_ref, sh2_ref,
                  exp_ref, wo_ref, lg_ref, lb_ref, wr_ref, br_ref,
                  x1_ref, h2_ref, wt_ref, rank_ref, cnt_ref):
    tm = x_ref.shape[0]
    gh, gl = _split(g_ref[...])
    nsa = jnp.zeros((tm, D_MODEL), F32)
    for n, o_ref in enumerate((oc_ref, os_ref, ow_ref)):
        e = exp_ref[n]
        nsa = nsa + o_ref[...] * (_dot(gh, e) + _dot(gl, e))
    mix = mg_ref[:, :D_MODEL] * cy_ref[...] + mg_ref[:, D_MODEL:] * nsa
    y = _dot(mix.astype(BF16), wo_ref[...])
    x1 = _layer_norm(DN_ALPHA * x_ref[...] + g1_ref[...] * y, lg_ref[...], lb_ref[...])
    x1_ref[...] = x1
    h2 = x1 * (1.0 + sc2_ref[...]) + sh2_ref[...]
    h2_ref[...] = h2.astype(BF16)
    logits = _dot3(wr_ref[...], h2, _dot_nt) + br_ref[...]
    eidx = _iota(logits.shape, 0)
    picked = jnp.zeros(logits.shape, F32)
    wsum = jnp.zeros((1, tm), F32)
    wts = jnp.zeros(logits.shape, F32)
    v0 = None
    for k in range(TOP_K):
        v = jnp.max(logits, axis=0, keepdims=True)
        first = jnp.min(jnp.where(logits == v, eidx, N_EXPERTS), axis=0, keepdims=True)
        hit = eidx == first
        if k == 0:
            v0 = v
        ev = jnp.exp(v - v0)
        wts = wts + jnp.where(hit, ev, 0.0)
        wsum = wsum + ev
        picked = picked + jnp.where(hit, 1.0, 0.0)
        logits = jnp.where(hit, -jnp.inf, logits)
    wt_ref[0] = wts / wsum
    upper = jnp.where(_iota((tm, tm), 0) < _iota((tm, tm), 1), 1.0, 0.0).astype(BF16)
    rank_ref[0] = _dot(picked.astype(BF16), upper)
    cnt = jnp.sum(picked, axis=1, keepdims=True)
    cnt_ref[0] = jnp.broadcast_to(cnt, (N_EXPERTS, LANES)).astype(jnp.int32)


def _merge(x, conv_y, o_cmp, o_slc, o_win, g_nsa, merge_g, mod, gate_expand, w_out, ln_g, ln_b,
           w_router_t, b_router):
    n = x.shape[0]
    tm = MERGE_TILE
    nt = n // tm
    per_row = mod.shape[0] != 1

    def row_spec(w):
        return pl.BlockSpec((tm, w), lambda i: (i, 0))

    def mod_spec(col):
        return pl.BlockSpec((tm if per_row else 1, D_MODEL),
                            (lambda i: (i, col)) if per_row else (lambda i: (0, col)))

    tile_spec = pl.BlockSpec((1, N_EXPERTS, tm), lambda i: (i, 0, 0))
    return pl.pallas_call(
        _merge_kernel,
        grid=(nt,),
        in_specs=[row_spec(D_MODEL)] * 5 + [row_spec(GATE_PAD), row_spec(2 * D_MODEL),
                  mod_spec(2), mod_spec(4), mod_spec(3),
                  _const_spec(gate_expand.shape), _const_spec(w_out.shape),
                  _const_spec((1, D_MODEL)), _const_spec((1, D_MODEL)),
                  _const_spec(w_router_t.shape), _const_spec((N_EXPERTS, 1))],
        out_specs=[row_spec(D_MODEL), row_spec(D_MODEL), tile_spec, tile_spec,
                   pl.BlockSpec((1, N_EXPERTS, LANES), lambda i: (i, 0, 0))],
        out_shape=[jax.ShapeDtypeStruct((n, D_MODEL), F32), jax.ShapeDtypeStruct((n, D_MODEL), BF16),
                   jax.ShapeDtypeStruct((nt, N_EXPERTS, tm), F32),
                   jax.ShapeDtypeStruct((nt, N_EXPERTS, tm), F32),
                   jax.ShapeDtypeStruct((nt, N_EXPERTS, LANES), jnp.int32)],
        compiler_params=_params("parallel"),
    )(x, conv_y, o_cmp, o_slc, o_win, g_nsa, merge_g, mod, mod, mod, gate_expand, w_out,
      ln_g.reshape(1, -1), ln_b.reshape(1, -1), w_router_t, b_router.reshape(-1, 1))


def _moe_kernel(cnt_ref, x1_ref, h2_ref, wt_ref, rank_ref, g2_ref, wup_ref, bup_ref, wdn_ref, bdn_ref,
                lg_ref, lb_ref, o_ref, acc_ref):
    i = pl.program_id(0)
    e = pl.program_id(1)
    tm = x1_ref.shape[0]

    @pl.when(e == 0)
    def _():
        acc_ref[...] = jnp.zeros_like(acc_ref)

    cnt = cnt_ref[i * N_EXPERTS + e]
    for ch in range(tm // MOE_CHUNK):
        @pl.when(cnt > ch * MOE_CHUNK)
        def _():
            w_e = wt_ref[0, pl.ds(e, 1), :]
            slot = rank_ref[0, pl.ds(e, 1), :] - float(ch * MOE_CHUNK)
            rows = _iota((MOE_CHUNK, tm), 0).astype(F32)
            onehot = jnp.where((rows == slot) & (w_e > 0.0), 1.0, 0.0)
            wc = jnp.sum(onehot * w_e, axis=1, keepdims=True)
            sel = onehot.astype(BF16)
            xc = _dot(sel, h2_ref[...]).astype(BF16)
            u = _dot(xc, wup_ref[0]) + bup_ref[0]
            x_glu = jnp.minimum(u[:, :D_EXPERT], SWIGLU_LIMIT)
            x_lin = jnp.clip(u[:, D_EXPERT:], -SWIGLU_LIMIT, SWIGLU_LIMIT)
            a = x_glu * _sigmoid(SWIGLU_ALPHA * x_glu) * (x_lin + 1.0)
            y = (_dot(a.astype(BF16), wdn_ref[0]) + bdn_ref[0]) * wc
            acc_ref[...] += _dot_tn(sel, y.astype(BF16))

    @pl.when(e == N_EXPERTS - 1)
    def _():
        o_ref[...] = _layer_norm(DN_ALPHA * x1_ref[...] + g2_ref[...] * acc_ref[...],
                                 lg_ref[...], lb_ref[...])


def _moe(x1, h2, wt, rank, cnt, mod, w_up, b_up, w_down, b_down, ln_g, ln_b):
    n = x1.shape[0]
    tm = MOE_TILE
    nt = n // tm
    per_row = mod.shape[0] != 1
    g2_spec = pl.BlockSpec((tm if per_row else 1, D_MODEL),
                           (lambda i, e, c: (i, 5)) if per_row else (lambda i, e, c: (0, 5)))
    tile_spec = pl.BlockSpec((1, N_EXPERTS, tm), lambda i, e, c: (i, 0, 0))
    grid_spec = pltpu.PrefetchScalarGridSpec(
        num_scalar_prefetch=1,
        grid=(nt, N_EXPERTS),
        in_specs=[pl.BlockSpec((tm, D_MODEL), lambda i, e, c: (i, 0)),
                  pl.BlockSpec((tm, D_MODEL), lambda i, e, c: (i, 0)),
                  tile_spec, tile_spec, g2_spec,
                  pl.BlockSpec((1, D_MODEL, 2 * D_EXPERT), lambda i, e, c: (e, 0, 0)),
                  pl.BlockSpec((1, 1, 2 * D_EXPERT), lambda i, e, c: (e, 0, 0)),
                  pl.BlockSpec((1, D_EXPERT, D_MODEL), lambda i, e, c: (e, 0, 0)),
                  pl.BlockSpec((1, 1, D_MODEL), lambda i, e, c: (e, 0, 0)),
                  pl.BlockSpec((1, D_MODEL), lambda i, e, c: (0, 0)),
                  pl.BlockSpec((1, D_MODEL), lambda i, e, c: (0, 0))],
        out_specs=pl.BlockSpec((tm, D_MODEL), lambda i, e, c: (i, 0)),
        scratch_shapes=[pltpu.VMEM((tm, D_MODEL), F32)],
    )
    return pl.pallas_call(
        _moe_kernel,
        grid_spec=grid_spec,
        out_shape=jax.ShapeDtypeStruct((n, D_MODEL), F32),
        compiler_params=_params("parallel", "arbitrary"),
    )(cnt, x1, h2, wt, rank, mod, w_up, b_up.reshape(N_EXPERTS, 1, -1), w_down,
      b_down.reshape(N_EXPERTS, 1, -1), ln_g.reshape(1, -1), ln_b.reshape(1, -1))


def _softmax_rows(s, valid):
    s = jnp.where(valid, s, NEG_INF)
    m = jnp.max(s, axis=-1, keepdims=True)
    e = jnp.where(valid, jnp.exp(s - m), 0.0)
    l = jnp.sum(e, axis=-1, keepdims=True)
    return e / jnp.where(l > 0.0, l, 1.0)


def _flash_update(s, valid, v, m_old, l_old, acc_old, pv=_dot):
    s = jnp.where(valid, s, NEG_INF)
    m_new = jnp.maximum(m_old, jnp.max(s, axis=-1, keepdims=True))
    p = jnp.where(valid, jnp.exp(s - m_new), 0.0)
    alpha = jnp.exp(m_old - m_new)
    l_new = alpha * l_old + jnp.sum(p, axis=-1, keepdims=True)
    acc_new = alpha * acc_old + pv(p.astype(BF16), v)
    return m_new, l_new, acc_new


def _shift_right_one(x):
    return jnp.where(_iota(x.shape, 1) == 0, 0.0, pltpu.roll(x, 1, 1))


def _block_scores(imp, nj):
    parts = [imp[:, k * nj:(k + 1) * nj] for k in range(SEL_RATIO)]
    return (parts[0] + parts[1]) + (parts[2] + parts[3]) + _shift_right_one(parts[3])


def _select_blocks(score, qblk, n_blocks):
    j = _iota(score.shape, 1)
    back = qblk - j
    forced = (j == 0) | ((back >= 0) & (back < N_LOCAL))
    score = jnp.where(forced, FORCE_SCORE, jnp.where(back >= 0, score, -1.0))
    score = jnp.where(j < n_blocks, score, -jnp.inf)
    picked = jnp.zeros(score.shape, F32)
    for _ in range(N_SEL):
        m = jnp.max(score, axis=-1, keepdims=True)
        first = jnp.min(jnp.where(score == m, j, SEL_LANES), axis=-1, keepdims=True)
        hit = j == first
        picked = jnp.where(hit, 1.0, picked)
        score = jnp.where(hit, -jnp.inf, score)
    return picked


def _gelu_tanh(x):
    return x * (0.5 * (1.0 + jnp.tanh(0.7978845608028654 * (x + 0.044715 * (x * x * x)))))


def _compress_rows(buf_ref, pe_ref, w1_ref, w2_ref, nj):
    outs = []
    for pair in range(HEAD_PAIRS):
        acc = jnp.zeros((SEL_RATIO * nj, 2 * PHI_HIDDEN), F32)
        for s in range(CMP_LEN):
            parts = [buf_ref[pair, pl.ds(CMP_STRIDE * k + s, nj, stride=SEL_LEN), :] for k in range(SEL_RATIO)]
            x = (jnp.concatenate(parts, axis=0) + pe_ref[s:s + 1, :]).astype(BF16)
            acc = acc + _dot(x, w1_ref[s])
        outs.append(_dot(_gelu_tanh(acc).astype(BF16), w2_ref[...]))
    return jnp.concatenate(outs, axis=1)


CMP_TILE = 4096


def _cmp_prompt_kernel(x_ref, halo_ref, pe_ref, w1_ref, w2_ref, o_ref, buf_ref):
    t = pl.program_id(0)
    last = t == pl.num_programs(0) - 1
    for pair in range(HEAD_PAIRS):
        cols = slice(pair * LANES, (pair + 1) * LANES)
        buf_ref[pair, :CMP_TILE, :] = x_ref[:, cols]
        buf_ref[pair, CMP_TILE:, :] = jnp.where(last, 0.0, halo_ref[:, cols])
    nj = CMP_TILE // SEL_LEN
    out = _compress_rows(buf_ref, pe_ref, w1_ref, w2_ref, nj)
    o_ref[...] = out.reshape(SEL_RATIO, nj, KV_WIDTH).astype(BF16)


def _compress_prompt(kv4, kv_set, pe, w1, w2):
    t_len = kv4.shape[0]
    nt = t_len // CMP_TILE
    nj = CMP_TILE // SEL_LEN
    halo_blocks = CMP_TILE // SEL_LEN
    last_halo = t_len // SEL_LEN - 1
    out = pl.pallas_call(
        _cmp_prompt_kernel,
        grid=(nt,),
        in_specs=[pl.BlockSpec((CMP_TILE, KV_WIDTH), lambda t: (t, kv_set)),
                  pl.BlockSpec((SEL_LEN, KV_WIDTH),
                               lambda t: (jnp.minimum((t + 1) * halo_blocks, last_halo), kv_set)),
                  _const_spec(pe.shape), _const_spec(w1.shape), _const_spec(w2.shape)],
        out_specs=pl.BlockSpec((SEL_RATIO, nj, KV_WIDTH), lambda t: (0, t, 0)),
        out_shape=jax.ShapeDtypeStruct((SEL_RATIO, t_len // SEL_LEN, KV_WIDTH), BF16),
        scratch_shapes=[pltpu.VMEM((HEAD_PAIRS, CMP_TILE + SEL_LEN, LANES), F32)],
        compiler_params=_params("arbitrary"),
    )(kv4, kv4, pe, w1, w2)
    return out.reshape(t_len // CMP_STRIDE, KV_WIDTH)


def _cmp_attn_prompt_kernel(q_ref, kc_ref, vc_ref, o_ref, sel_ref, any_ref):
    nc = kc_ref.shape[0]
    nj = nc // SEL_RATIO
    s0 = pl.program_id(0) * Q_TILE
    tq = s0 + _iota((Q_TILE, 1), 0)
    col = _iota((1, nc), 1)
    ends = (SEL_RATIO * (col % nj) + col // nj) * CMP_STRIDE + (CMP_LEN - 1)
    dist = (tq - ends).astype(F32)
    valid = ends <= tq
    any_rows = []
    for g in range(N_KV_HEADS):
        kc = kc_ref[:, g * HEAD_DIM:(g + 1) * HEAD_DIM]
        vc = vc_ref[:, g * HEAD_DIM:(g + 1) * HEAD_DIM]
        imp = jnp.zeros((Q_TILE, nc), F32)
        for r in range(GROUP):
            h = g * GROUP + r
            cols = slice(h * HEAD_DIM, (h + 1) * HEAD_DIM)
            p = _softmax_rows(_dot_nt(q_ref[:, cols], kc) - SLOPES[h] * dist, valid)
            o_ref[:, cols] = _dot(p.astype(BF16), vc)
            imp = imp + p
        picked = _select_blocks(_block_scores(imp, nj), tq // SEL_LEN, nj)
        sel_ref[g] = picked.astype(BF16)
        any_rows.append(jnp.max(picked, axis=0, keepdims=True))
    any_ref[0] = jnp.concatenate(any_rows + [jnp.zeros((ANY_ROWS - N_KV_HEADS, SEL_LANES), F32)], axis=0)


def _cmp_attn_prompt(q, kc, vc):
    t_len = q.shape[0]
    nt = t_len // Q_TILE
    return pl.pallas_call(
        _cmp_attn_prompt_kernel,
        grid=(nt,),
        in_specs=[pl.BlockSpec((Q_TILE, Q_WIDTH), lambda i: (i, 0)),
                  _const_spec(kc.shape), _const_spec(vc.shape)],
        out_specs=[pl.BlockSpec((Q_TILE, Q_WIDTH), lambda i: (i, 0)),
                   pl.BlockSpec((N_KV_HEADS, Q_TILE, SEL_LANES), lambda i: (0, i, 0)),
                   pl.BlockSpec((1, ANY_ROWS, SEL_LANES), lambda i: (i, 0, 0))],
        out_shape=[jax.ShapeDtypeStruct((t_len, Q_WIDTH), F32),
                   jax.ShapeDtypeStruct((N_KV_HEADS, t_len, SEL_LANES), BF16),
                   jax.ShapeDtypeStruct((nt, ANY_ROWS, SEL_LANES), F32)],
        compiler_params=_params("parallel"),
    )(q, kc, vc)


def _tile_flags(any_sel):
    nt = any_sel.shape[0]
    hit = any_sel[:, :N_KV_HEADS].reshape(nt, N_KV_HEADS, SEL_LANES // SEL_RATIO, SEL_RATIO).max(-1) > 0.0
    bits = hit.reshape(nt, N_KV_HEADS, -1, FLAG_BITS).astype(jnp.uint32) << jnp.arange(FLAG_BITS, dtype=jnp.uint32)
    return lax.bitcast_convert_type(bits.sum(-1, dtype=jnp.uint32), jnp.int32).reshape(-1)


def _slc_attn_prompt_kernel(flag_ref, q_ref, k_ref, v_ref, sel_ref, o_ref, qs_ref, m_ref, l_ref, acc_ref):
    i = pl.program_id(0)
    tq = i * Q_TILE + _iota((Q_TILE, 1), 0)
    for h in range(N_HEADS):
        qs_ref[h // GROUP, (h % GROUP) * Q_TILE:(h % GROUP + 1) * Q_TILE, :] = q_ref[:, h * HEAD_DIM:(h + 1) * HEAD_DIM]
    m_ref[...] = jnp.full(m_ref.shape, NEG_INF, F32)
    l_ref[...] = jnp.zeros(l_ref.shape, F32)
    acc_ref[...] = jnp.zeros(acc_ref.shape, F32)

    def tile(t, carry):
        base = pl.multiple_of(t * KV_TILE, KV_TILE)
        dist_i = tq - (base + _iota((1, KV_TILE), 1))
        dist = dist_i.astype(F32)
        causal = dist_i >= 0
        expand = jnp.where(_iota((SEL_LANES, KV_TILE), 0)
                           == t * SEL_RATIO + _iota((SEL_LANES, KV_TILE), 1) // SEL_LEN, 1.0, 0.0).astype(BF16)
        for g in range(N_KV_HEADS):
            word = flag_ref[(i * N_KV_HEADS + g) * FLAG_WORDS + t // FLAG_BITS]

            @pl.when(((word >> (t % FLAG_BITS)) & 1) == 1)
            def _():
                valid = (_dot(sel_ref[g], expand) > 0.5) & causal
                kt = k_ref[pl.ds(base, KV_TILE), g * HEAD_DIM:(g + 1) * HEAD_DIM]
                vt = v_ref[pl.ds(base, KV_TILE), g * HEAD_DIM:(g + 1) * HEAD_DIM]
                raw = _dot_nt(qs_ref[g], kt)
                s = jnp.concatenate(
                    [jnp.where(valid, raw[r * Q_TILE:(r + 1) * Q_TILE] - SLOPES[g * GROUP + r] * dist, NEG_INF)
                     for r in range(GROUP)], axis=0)
                m_old = m_ref[g]
                m_new = jnp.maximum(m_old, jnp.max(s, axis=-1, keepdims=True))
                p = jnp.exp(s - m_new)
                alpha = jnp.exp(m_old - m_new)
                l_ref[g] = alpha * l_ref[g] + jnp.sum(p, axis=-1, keepdims=True)
                acc_ref[g] = alpha * acc_ref[g] + _dot(p.astype(BF16), vt)
                m_ref[g] = m_new
        return carry

    lax.fori_loop(0, (i * Q_TILE) // KV_TILE + 1, tile, 0)
    for h in range(N_HEADS):
        rows = slice((h % GROUP) * Q_TILE, (h % GROUP + 1) * Q_TILE)
        l = l_ref[h // GROUP, rows, :]
        o_ref[:, h * HEAD_DIM:(h + 1) * HEAD_DIM] = acc_ref[h // GROUP, rows, :] / jnp.where(l > 0.0, l, 1.0)


def _slc_attn_prompt(q, kvb, sel, flags):
    t_len = q.shape[0]
    resident = functools.partial(pl.BlockSpec, pipeline_mode=pl.Buffered(1))
    rows = GROUP * Q_TILE
    grid_spec = pltpu.PrefetchScalarGridSpec(
        num_scalar_prefetch=1,
        grid=(t_len // Q_TILE,),
        in_specs=[pl.BlockSpec((Q_TILE, Q_WIDTH), lambda i, f: (i, 0)),
                  resident((t_len, KV_WIDTH), lambda i, f: (0, 2)),
                  resident((t_len, KV_WIDTH), lambda i, f: (0, 3)),
                  pl.BlockSpec((N_KV_HEADS, Q_TILE, SEL_LANES), lambda i, f: (0, i, 0))],
        out_specs=pl.BlockSpec((Q_TILE, Q_WIDTH), lambda i, f: (i, 0)),
        scratch_shapes=[pltpu.VMEM((N_KV_HEADS, rows, HEAD_DIM), BF16),
                        pltpu.VMEM((N_KV_HEADS, rows, 1), F32), pltpu.VMEM((N_KV_HEADS, rows, 1), F32),
                        pltpu.VMEM((N_KV_HEADS, rows, HEAD_DIM), F32)],
    )
    return pl.pallas_call(
        _slc_attn_prompt_kernel,
        grid_spec=grid_spec,
        out_shape=jax.ShapeDtypeStruct((t_len, Q_WIDTH), F32),
        compiler_params=_params("parallel"),
    )(flags, q, kvb, kvb, sel)


def _win_attn_prompt_kernel(q_ref, k_ref, v_ref, o_ref):
    s0 = pl.multiple_of(pl.program_id(0) * Q_TILE, Q_TILE)
    n_keys = WINDOW + Q_TILE
    tq = s0 + _iota((Q_TILE, 1), 0)
    pw = s0 - WINDOW + _iota((1, n_keys), 1)
    dw = tq - pw
    valid = (dw >= 0) & (dw < WINDOW) & (pw >= 0)
    dist = dw.astype(F32)
    for g in range(N_KV_HEADS):
        kw = k_ref[pl.ds(s0, n_keys), g * HEAD_DIM:(g + 1) * HEAD_DIM]
        vw = v_ref[pl.ds(s0, n_keys), g * HEAD_DIM:(g + 1) * HEAD_DIM]
        for r in range(GROUP):
            h = g * GROUP + r
            cols = slice(h * HEAD_DIM, (h + 1) * HEAD_DIM)
            p = _softmax_rows(_dot_nt(q_ref[:, cols], kw) - SLOPES[h] * dist, valid)
            o_ref[:, cols] = _dot(p.astype(BF16), vw)


def _win_attn_prompt(q, kw_pad, vw_pad):
    t_len = q.shape[0]
    resident = functools.partial(pl.BlockSpec, pipeline_mode=pl.Buffered(1))
    return pl.pallas_call(
        _win_attn_prompt_kernel,
        grid=(t_len // Q_TILE,),
        in_specs=[pl.BlockSpec((Q_TILE, Q_WIDTH), lambda i: (i, 0)),
                  resident(kw_pad.shape, lambda i: (0, 0)), resident(vw_pad.shape, lambda i: (0, 0))],
        out_specs=pl.BlockSpec((Q_TILE, Q_WIDTH), lambda i: (i, 0)),
        out_shape=jax.ShapeDtypeStruct((t_len, Q_WIDTH), F32),
        compiler_params=_params("parallel"),
    )(q, kw_pad, vw_pad)


S_ROWS = N_HEADS * DEC_SEQ
S_TAIL = SEL_LEN
S_CMP_BLOCKS = PAST_LEN // CMP_STRIDE
S_CHUNK = 2048


def _fetch_pages(pt_ref, sem_ref, page_copy):
    b = pl.program_id(0)
    slot = b % 2

    def copy(seq, p, dst_slot):
        return page_copy(pt_ref[seq * N_PAGES + p], p, dst_slot, sem_ref.at[dst_slot])

    def start(seq, dst_slot):
        def body(p, c):
            copy(seq, p, dst_slot).start()
            return c
        lax.fori_loop(0, N_PAGES, body, 0)

    @pl.when(b == 0)
    def _():
        start(b, slot)

    @pl.when(b + 1 < pl.num_programs(0))
    def _():
        start(b + 1, 1 - slot)

    def wait(p, c):
        copy(b, p, slot).wait()
        return c
    lax.fori_loop(0, N_PAGES, wait, 0)
    return slot


def _cmp_sample_kernel(kv_set, pt_ref, cache_ref, new_ref, pe_ref, w1_ref, w2_ref, o_ref,
                       stage_ref, buf_ref, sem_ref):
    def page_copy(page, p, slot, sem):
        return pltpu.make_async_copy(cache_ref.at[page, kv_set],
                                     stage_ref.at[slot, pl.ds(p * KV_WIDTH, KV_WIDTH), :], sem)

    slot = _fetch_pages(pt_ref, sem_ref, page_copy)

    def to_rows(p, c):
        src = pl.multiple_of(p * KV_WIDTH, KV_WIDTH)
        dst = pl.multiple_of(p * PAGE_SIZE, PAGE_SIZE)
        for pair in range(HEAD_PAIRS):
            buf_ref[pair, pl.ds(dst, PAGE_SIZE), :] = stage_ref[slot, pl.ds(src + pair * LANES, LANES), :].T
        return c
    lax.fori_loop(0, N_PAGES, to_rows, 0)
    for pair in range(HEAD_PAIRS):
        buf_ref[pair, PAST_LEN:PAST_LEN + DEC_SEQ, :] = new_ref[:, pair * LANES:(pair + 1) * LANES]
        buf_ref[pair, PAST_LEN + DEC_SEQ:, :] = jnp.zeros((S_TAIL - DEC_SEQ, LANES), F32)
    out = _compress_rows(buf_ref, pe_ref, w1_ref, w2_ref, S_CMP_BLOCKS // SEL_RATIO)
    o_ref[0] = out.astype(BF16)


def _compress_sample(page_table, cache, kv4_new, kv_set, pe, w1, w2):
    grid_spec = pltpu.PrefetchScalarGridSpec(
        num_scalar_prefetch=1,
        grid=(DEC_BATCH,),
        in_specs=[pl.BlockSpec(memory_space=pl.ANY),
                  pl.BlockSpec((DEC_SEQ, KV_WIDTH), lambda b, pt: (b, kv_set)),
                  pl.BlockSpec(pe.shape, lambda b, pt: (0, 0)),
                  pl.BlockSpec(w1.shape, lambda b, pt: (0, 0, 0)),
                  pl.BlockSpec(w2.shape, lambda b, pt: (0, 0))],
        out_specs=pl.BlockSpec((1, S_CMP_BLOCKS, KV_WIDTH), lambda b, pt: (b, 0, 0)),
        scratch_shapes=[pltpu.VMEM((2, N_PAGES * KV_WIDTH, PAGE_SIZE), F32),
                        pltpu.VMEM((HEAD_PAIRS, PAST_LEN + S_TAIL, LANES), F32),
                        pltpu.SemaphoreType.DMA((2,))],
    )
    return pl.pallas_call(
        functools.partial(_cmp_sample_kernel, kv_set),
        grid_spec=grid_spec,
        out_shape=jax.ShapeDtypeStruct((DEC_BATCH, S_CMP_BLOCKS, KV_WIDTH), BF16),
        compiler_params=_params("arbitrary"),
    )(page_table.reshape(-1), cache, kv4_new, pe, w1, w2)


def _row_queries():
    return PAST_LEN + _iota((S_ROWS, 1), 0) % DEC_SEQ


def _cmp_attn_sample_kernel(q_ref, slope_ref, kc_ref, vc_ref, o_ref, sel_ref):
    nc = S_CMP_BLOCKS
    nj = nc // SEL_RATIO
    tq = _row_queries()
    col = _iota((1, nc), 1)
    ends = (SEL_RATIO * (col % nj) + col // nj) * CMP_STRIDE + (CMP_LEN - 1)
    s = _dot_nt(q_ref[0], kc_ref[0]) - slope_ref[...] * (tq - ends).astype(F32)
    p = _softmax_rows(s, ends <= tq)
    o_ref[0] = _dot(p.astype(BF16), vc_ref[0])
    rows_g = GROUP * DEC_SEQ
    n_blocks = (PAST_LEN + DEC_SEQ + SEL_LEN - 1) // SEL_LEN
    qblk = (PAST_LEN + _iota((DEC_SEQ, 1), 0)) // SEL_LEN
    picked = []
    for g in range(N_KV_HEADS):
        imp = jnp.zeros((DEC_SEQ, nc), F32)
        for r in range(GROUP):
            imp = imp + p[g * rows_g + r * DEC_SEQ:g * rows_g + (r + 1) * DEC_SEQ]
        score = jnp.concatenate([_block_scores(imp, nj), jnp.zeros((DEC_SEQ, SEL_LANES - nj), F32)], axis=1)
        picked += [_select_blocks(score, qblk, n_blocks)] * GROUP
    sel_ref[0] = jnp.concatenate(picked, axis=0).astype(BF16)


def _cmp_attn_sample(qbd, slope_col, kc, vc):
    seq_spec = lambda rows, w: pl.BlockSpec((1, rows, w), lambda b: (b, 0, 0))
    return pl.pallas_call(
        _cmp_attn_sample_kernel,
        grid=(DEC_BATCH,),
        in_specs=[seq_spec(S_ROWS, KV_WIDTH), _const_spec((S_ROWS, 1)),
                  seq_spec(S_CMP_BLOCKS, KV_WIDTH), seq_spec(S_CMP_BLOCKS, KV_WIDTH)],
        out_specs=[seq_spec(S_ROWS, KV_WIDTH), seq_spec(S_ROWS, SEL_LANES)],
        out_shape=[jax.ShapeDtypeStruct((DEC_BATCH, S_ROWS, KV_WIDTH), F32),
                   jax.ShapeDtypeStruct((DEC_BATCH, S_ROWS, SEL_LANES), BF16)],
        compiler_params=_params("parallel"),
    )(qbd, slope_col, kc, vc)


def _pad_new_rows(x):
    return jnp.concatenate([x, jnp.zeros((LANES - DEC_SEQ, x.shape[1]), F32)], axis=0).astype(BF16)


def _slc_win_sample_kernel(pt_ref, cache_ref, q_ref, slope_ref, sel_ref, new_ref, cwin_ref, wnew_ref,
                           os_ref, ow_ref, buf_ref, sem_ref):
    def page_copy(page, p, slot, sem):
        return pltpu.make_async_copy(cache_ref.at[page, pl.ds(2, 2)],
                                     buf_ref.at[slot, :, :, pl.ds(p * PAGE_SIZE, PAGE_SIZE)], sem)

    slot = _fetch_pages(pt_ref, sem_ref, page_copy)
    q = q_ref[0]
    slope = slope_ref[...]
    sel = sel_ref[0]
    tq = _row_queries()
    new_pos = PAST_LEN + _iota((1, LANES), 1)
    new_ok = (tq - new_pos >= 0) & (new_pos < PAST_LEN + DEC_SEQ)
    new_dist = (tq - new_pos).astype(F32)

    m = jnp.full((S_ROWS, 1), NEG_INF, F32)
    l = jnp.zeros((S_ROWS, 1), F32)
    acc = jnp.zeros((S_ROWS, KV_WIDTH), F32)
    for c in range(PAST_LEN // S_CHUNK):
        k_t = buf_ref[slot, 0, :, c * S_CHUNK:(c + 1) * S_CHUNK].astype(BF16)
        v_t = buf_ref[slot, 1, :, c * S_CHUNK:(c + 1) * S_CHUNK].astype(BF16)
        pos = c * S_CHUNK + _iota((1, S_CHUNK), 1)
        expand = jnp.where(_iota((SEL_LANES, S_CHUNK), 0)
                           == (c * S_CHUNK + _iota((SEL_LANES, S_CHUNK), 1)) // SEL_LEN, 1.0, 0.0).astype(BF16)
        valid = (_dot(sel, expand) > 0.5) & (tq - pos >= 0)
        s = _dot(q, k_t) - slope * (tq - pos).astype(F32)
        m, l, acc = _flash_update(s, valid, v_t, m, l, acc, pv=_dot_nt)
    new = _pad_new_rows(new_ref[...])
    in_last = jnp.sum(jnp.where(_iota((S_ROWS, SEL_LANES), 1) == PAST_LEN // SEL_LEN, sel.astype(F32), 0.0),
                      axis=1, keepdims=True) > 0.5
    s = _dot_nt(q, new[:, :KV_WIDTH]) - slope * new_dist
    m, l, acc = _flash_update(s, in_last & new_ok, new[:, KV_WIDTH:], m, l, acc)
    os_ref[0] = acc / jnp.where(l > 0.0, l, 1.0)

    wb = cwin_ref.shape[-1]
    pw = PAST_LEN - wb + _iota((1, wb), 1)
    dw = tq - pw
    s_past = _dot(q, cwin_ref[0, 0].astype(BF16)) - slope * dw.astype(F32)
    wnew = _pad_new_rows(wnew_ref[...])
    s_new = _dot_nt(q, wnew[:, :KV_WIDTH]) - slope * new_dist
    valid = jnp.concatenate([(dw >= 0) & (dw < WINDOW) & (pw >= 0), new_ok & (tq - new_pos < WINDOW)], axis=1)
    p = _softmax_rows(jnp.concatenate([s_past, s_new], axis=1), valid).astype(BF16)
    ow_ref[0] = _dot_nt(p[:, :wb], cwin_ref[0, 1].astype(BF16)) + _dot(p[:, wb:], wnew[:, KV_WIDTH:])


def _slc_win_sample(page_table, cache, qbd, slope_col, sel, kv4_new, cache_win, win_new):
    wb = cache_win.shape[-1]
    seq_spec = lambda rows, w: pl.BlockSpec((1, rows, w), lambda b, pt: (b, 0, 0))
    grid_spec = pltpu.PrefetchScalarGridSpec(
        num_scalar_prefetch=1,
        grid=(DEC_BATCH,),
        in_specs=[pl.BlockSpec(memory_space=pl.ANY),
                  seq_spec(S_ROWS, KV_WIDTH),
                  pl.BlockSpec((S_ROWS, 1), lambda b, pt: (0, 0)),
                  seq_spec(S_ROWS, SEL_LANES),
                  pl.BlockSpec((DEC_SEQ, 2 * KV_WIDTH), lambda b, pt: (b, 1)),
                  pl.BlockSpec((1, 2, KV_WIDTH, wb), lambda b, pt: (b, 0, 0, 0)),
                  pl.BlockSpec((DEC_SEQ, 2 * KV_WIDTH), lambda b, pt: (b, 0))],
        out_specs=[seq_spec(S_ROWS, KV_WIDTH), seq_spec(S_ROWS, KV_WIDTH)],
        scratch_shapes=[pltpu.VMEM((2, 2, KV_WIDTH, PAST_LEN), F32), pltpu.SemaphoreType.DMA((2,))],
    )
    return pl.pallas_call(
        _slc_win_sample_kernel,
        grid_spec=grid_spec,
        out_shape=[jax.ShapeDtypeStruct((DEC_BATCH, S_ROWS, KV_WIDTH), F32)] * 2,
        compiler_params=_params("arbitrary"),
    )(page_table.reshape(-1), cache, qbd, slope_col, sel, kv4_new, cache_win, win_new)


def _gate_expand():
    rows = jnp.arange(GATE_PAD)[:, None]
    cols = jnp.arange(D_MODEL)[None, :]
    return jnp.stack([(rows == (cols // HEAD_DIM) * N_NSA_BRANCH + n) for n in range(N_NSA_BRANCH)]
                     ).astype(BF16)


def _compress_weights(pe, w1, w2):
    per_tile = LANES // HEAD_DIM
    eye = jnp.eye(per_tile, dtype=F32)
    w1_big = jnp.einsum('sde,gh->sgdhe', w1, eye).reshape(CMP_LEN, LANES, per_tile * PHI_HIDDEN).astype(BF16)
    w2_big = jnp.einsum('ed,gh->gehd', w2, eye).reshape(per_tile * PHI_HIDDEN, LANES).astype(BF16)
    return jnp.tile(pe, (1, per_tile)), w1_big, w2_big


def _rows_from_heads(o):
    o = o.reshape(DEC_BATCH, N_KV_HEADS, GROUP, DEC_SEQ, N_KV_HEADS, HEAD_DIM)
    idx = jnp.arange(N_KV_HEADS)
    own = o[:, idx, :, :, idx]
    return own.transpose(1, 3, 0, 2, 4).reshape(DEC_BATCH * DEC_SEQ, Q_WIDTH)


def kernel(x_prompt, x_sample, c_prompt, c_sample, cache_kv, cache_win, state_conv, page_table, w_ada, b_ada, w_in, conv_dw_w, conv_dw_b, conv_ln_g, conv_ln_b, w_conv_out, cmp_pe_k, cmp_w1_k, cmp_w2_k, cmp_pe_v, cmp_w1_v, cmp_w2_v, w_out, ln1_g, ln1_b, w_router, b_router, w_up, b_up, w_down, b_down, ln2_g, ln2_b):
    l = 0
    n_p = SEQ
    n_s = DEC_BATCH * DEC_SEQ

    c_all = jnp.concatenate([c_prompt, c_sample, jnp.zeros((7, D_MODEL), F32)], axis=0)
    mod = _adaln(c_all, w_ada[l], b_ada[l])
    mod_p = mod[0:1]
    mod_s = jnp.repeat(mod[1:1 + DEC_BATCH], DEC_SEQ, axis=0)

    wi = w_in[l]
    wts = ((wi[:, :OFF_KV] * (HEAD_DIM ** -0.5)).astype(BF16),
           wi[:, OFF_KV:OFF_GLU].astype(BF16),
           wi[:, OFF_GLU:OFF_NSA_G].astype(BF16),
           jnp.pad(wi[:, OFF_NSA_G:OFF_MERGE], ((0, 0), (0, GATE_PAD - N_HEADS * N_NSA_BRANCH))).astype(BF16),
           wi[:, OFF_MERGE:].astype(BF16))
    gate_expand = _gate_expand()
    w_out_b = w_out[l].astype(BF16)
    w_pw_b = w_conv_out[l].astype(BF16)
    w_up_b = w_up[l].astype(BF16)
    w_down_b = w_down[l].astype(BF16)
    w_router_t = w_router[l].T
    cmp_k = _compress_weights(cmp_pe_k[l], cmp_w1_k[l], cmp_w2_k[l])
    cmp_v = _compress_weights(cmp_pe_v[l], cmp_w1_v[l], cmp_w2_v[l])
    conv_w = (conv_dw_w[l], conv_dw_b[l], conv_ln_g[l], conv_ln_b[l], w_pw_b)

    def tail(x, conv_y, o3, g_nsa, merge_g, m):
        x1, h2, wt, rank, cnt = _merge(x, conv_y, *o3, g_nsa, merge_g, m, gate_expand, w_out_b,
                                       ln1_g[l], ln1_b[l], w_router_t, b_router[l])
        per = MOE_TILE // MERGE_TILE
        cnt = cnt[:, :, 0].reshape(-1, per, N_EXPERTS)
        before = jnp.cumsum(cnt, axis=1) - cnt
        rank = rank.reshape(-1, per, N_EXPERTS, MERGE_TILE) + before[..., None].astype(F32)
        rank = rank.transpose(0, 2, 1, 3).reshape(-1, N_EXPERTS, MOE_TILE)
        wt = wt.reshape(-1, per, N_EXPERTS, MERGE_TILE).transpose(0, 2, 1, 3).reshape(-1, N_EXPERTS, MOE_TILE)
        return _moe(x1, h2, wt, rank, cnt.sum(1).reshape(-1), m, w_up_b, b_up[l], w_down_b, b_down[l],
                    ln2_g[l], ln2_b[l])

    xp = x_prompt.reshape(n_p, D_MODEL)
    q, kv4, win2, kvb, u, g_nsa, merge_g = _project(xp, mod_p, mod_p, wts, 256)
    ct = 512
    u3 = u.reshape(n_p // ct, ct, D_CONV)
    halo = jnp.concatenate([jnp.zeros((1, CONV_HALO, D_CONV), F32), u3[:-1, ct - CONV_HALO:]], axis=0)
    conv_y = _conv_branch(u3, halo, *conv_w, 1).reshape(n_p, D_MODEL)
    kc = _compress_prompt(kv4, 0, *cmp_k)
    vc = _compress_prompt(kv4, 1, *cmp_v)
    o_cmp, sel, any_sel = _cmp_attn_prompt(q, kc, vc)
    o_slc = _slc_attn_prompt(q, kvb, sel, _tile_flags(any_sel))
    kw_pad = jnp.pad(kvb[:, 4 * KV_WIDTH:5 * KV_WIDTH], ((WINDOW, 0), (0, 0)))
    vw_pad = jnp.pad(kvb[:, 5 * KV_WIDTH:], ((WINDOW, 0), (0, 0)))
    o_win = _win_attn_prompt(q, kw_pad, vw_pad)
    y_p = tail(xp, conv_y, (o_cmp, o_slc, o_win), g_nsa, merge_g, mod_p)
    out_kv_p = kv4.reshape(1, 1, n_p, 4, N_KV_HEADS, HEAD_DIM)
    out_win_p = win2[n_p - WINDOW:].reshape(1, 1, WINDOW, 2, N_KV_HEADS, HEAD_DIM)
    out_conv_p = u[n_p - (CONV_WIDTH - 1):].reshape(1, 1, CONV_WIDTH - 1, D_CONV)

    xs = x_sample.reshape(n_s, D_MODEL)
    q, kv4, win2, kvb, u, g_nsa, merge_g = _project(xs, mod_s, mod_s, wts, 256)
    u3 = u.reshape(DEC_BATCH, DEC_SEQ, D_CONV)
    st = state_conv[l]
    halo = jnp.concatenate([jnp.zeros((DEC_BATCH, CONV_HALO - (CONV_WIDTH - 1), D_CONV), F32), st], axis=1)
    conv_y = _conv_branch(u3, halo, *conv_w, DEC_BATCH).reshape(n_s, D_MODEL)
    qh = q.reshape(DEC_BATCH, DEC_SEQ, N_KV_HEADS, GROUP, HEAD_DIM).transpose(0, 2, 3, 1, 4)
    qbd = jnp.einsum('bgrqd,gh->bgrqhd', qh, jnp.eye(N_KV_HEADS, dtype=BF16)).reshape(DEC_BATCH, S_ROWS, KV_WIDTH)
    slope_col = jnp.repeat(jnp.asarray(SLOPES, F32), DEC_SEQ).reshape(S_ROWS, 1)
    cache = cache_kv[l].transpose(0, 2, 3, 4, 1).reshape(-1, 4, KV_WIDTH, PAGE_SIZE)
    cwin = cache_win[l].transpose(0, 2, 3, 4, 1).reshape(DEC_BATCH, 2, KV_WIDTH, -1)
    kc = _compress_sample(page_table, cache, kv4, 0, *cmp_k)
    vc = _compress_sample(page_table, cache, kv4, 1, *cmp_v)
    o_cmp, sel = _cmp_attn_sample(qbd, slope_col, kc, vc)
    o_slc, o_win = _slc_win_sample(page_table, cache, qbd, slope_col, sel, kv4, cwin, win2)
    o3 = tuple(_rows_from_heads(o) for o in (o_cmp, o_slc, o_win))
    y_s = tail(xs, conv_y, o3, g_nsa, merge_g, mod_s)
    out_kv_s = kv4.reshape(1, DEC_BATCH, DEC_SEQ, 4, N_KV_HEADS, HEAD_DIM)
    win_new = win2.reshape(DEC_BATCH, DEC_SEQ, 2, N_KV_HEADS, HEAD_DIM)
    out_win_s = jnp.concatenate([cache_win[l], win_new], axis=1)[:, DEC_SEQ:][None]
    out_conv_s = jnp.concatenate([st, u3], axis=1)[:, DEC_SEQ:][None]

    return (y_p.reshape(1, n_p, D_MODEL), y_s.reshape(DEC_BATCH, DEC_SEQ, D_MODEL),
            out_kv_p, out_kv_s, out_win_p, out_win_s, out_conv_p, out_conv_s)
```

```python
import functools

import jax
import jax.numpy as jnp
from jax import lax
from jax.experimental import pallas as pl
from jax.experimental.pallas import tpu as pltpu

D_MODEL = 1024
SEQ = 16384
DEC_BATCH = 128
DEC_SEQ = 8
PAST_LEN = 8192
PAGE_SIZE = 128
N_PAGES = PAST_LEN // PAGE_SIZE
N_HEADS = 16
HEAD_DIM = 64
N_KV_HEADS = 4
GROUP = N_HEADS // N_KV_HEADS
KV_WIDTH = N_KV_HEADS * HEAD_DIM
Q_WIDTH = N_HEADS * HEAD_DIM
N_KV_SETS = 6
N_NSA_BRANCH = 3
CMP_LEN = 32
CMP_STRIDE = 16
PHI_HIDDEN = 2 * HEAD_DIM
SEL_LEN = 64
SEL_RATIO = SEL_LEN // CMP_STRIDE
N_SEL = 16
N_LOCAL = 2
WINDOW = 512
D_CONV = D_MODEL // 2
CONV_WIDTH = 31
N_EXPERTS = 32
TOP_K = 4
D_EXPERT = D_MODEL
SWIGLU_LIMIT = 7.0
SWIGLU_ALPHA = 1.702
DN_ALPHA = 2.0 ** 0.25
LN_EPS = 1e-5
NEG_INF = -1e30
FORCE_SCORE = 1e9
OFF_KV = Q_WIDTH
OFF_GLU = OFF_KV + N_KV_SETS * KV_WIDTH
OFF_NSA_G = OFF_GLU + 2 * D_CONV
OFF_MERGE = OFF_NSA_G + N_HEADS * N_NSA_BRANCH

LANES = 128
GATE_PAD = LANES
CONV_HALO = 32
VMEM_LIMIT = 56 * 1024 * 1024
Q_TILE = 128
KV_TILE = 512
SEL_LANES = 256
HEAD_PAIRS = KV_WIDTH // LANES
ANY_ROWS = 8
FLAG_BITS = 32
FLAG_WORDS = SEQ // KV_TILE // FLAG_BITS
SLOPES = tuple(2.0 ** (-8.0 * (h + 1) / N_HEADS) for h in range(N_HEADS))

F32 = jnp.float32
BF16 = jnp.bfloat16


def _params(*sem):
    return pltpu.CompilerParams(dimension_semantics=sem, vmem_limit_bytes=VMEM_LIMIT)


def _dot(a, b):
    return jnp.dot(a, b, preferred_element_type=F32)


def _dot_nt(a, b):
    return lax.dot_general(a, b, (((1,), (1,)), ((), ())), preferred_element_type=F32)


def _dot_tn(a, b):
    return lax.dot_general(a, b, (((0,), (0,)), ((), ())), preferred_element_type=F32)


def _split(x):
    hi = x.astype(BF16)
    lo = (x - hi.astype(F32)).astype(BF16)
    return hi, lo


def _dot3(a, b, dot=_dot):
    ah, al = _split(a)
    bh, bl = _split(b)
    return dot(ah, bh) + (dot(ah, bl) + dot(al, bh))


def _sigmoid(x):
    return 1.0 / (1.0 + jnp.exp(-x))


def _layer_norm(x, g, b):
    mu = jnp.mean(x, axis=-1, keepdims=True)
    xc = x - mu
    var = jnp.mean(xc * xc, axis=-1, keepdims=True)
    return xc * lax.rsqrt(var + LN_EPS) * g + b


def _const_spec(shape):
    return pl.BlockSpec(shape, lambda *_: (0,) * len(shape))


def _iota(shape, axis):
    return lax.broadcasted_iota(jnp.int32, shape, axis)


def _ada_kernel(c_ref, w_ref, b_ref, o_ref):
    o_ref[...] = _dot3(c_ref[...], w_ref[...]) + b_ref[...]


def _adaln(c_all, w_ada, b_ada):
    n = c_all.shape[0]
    tn = 1536
    return pl.pallas_call(
        _ada_kernel,
        grid=(6 * D_MODEL // tn,),
        in_specs=[pl.BlockSpec((n, D_MODEL), lambda j: (0, 0)),
                  pl.BlockSpec((D_MODEL, tn), lambda j: (0, j)),
                  pl.BlockSpec((1, tn), lambda j: (0, j))],
        out_specs=pl.BlockSpec((n, tn), lambda j: (0, j)),
        out_shape=jax.ShapeDtypeStruct((n, 6 * D_MODEL), F32),
        compiler_params=_params("parallel"),
    )(c_all, w_ada, b_ada.reshape(1, -1))


def _proj_kernel(x_ref, sc_ref, sh_ref, wq_ref, wkv_ref, wglu_ref, wg_ref, wm_ref,
                 q_ref, kv_ref, win_ref, kvb_ref, u_ref, g_ref, m_ref):
    h = (x_ref[...] * (1.0 + sc_ref[...]) + sh_ref[...]).astype(BF16)
    q_ref[...] = _dot(h, wq_ref[...]).astype(BF16)
    kv = _dot(h, wkv_ref[...])
    kv_ref[...] = kv[:, :4 * KV_WIDTH]
    win_ref[...] = kv[:, 4 * KV_WIDTH:]
    kvb_ref[...] = kv.astype(BF16)
    glu = _dot(h, wglu_ref[...])
    u_ref[...] = glu[:, :D_CONV] * _sigmoid(glu[:, D_CONV:])
    g_ref[...] = _sigmoid(_dot(h, wg_ref[...]))
    m_ref[...] = _sigmoid(_dot(h, wm_ref[...]))


def _project(x, sc, sh, wts, tm):
    n = x.shape[0]
    per_row = sc.shape[0] != 1
    mod_rows = tm if per_row else 1

    def mod_spec(col):
        return pl.BlockSpec((mod_rows, D_MODEL), (lambda i: (i, col)) if per_row else (lambda i: (0, col)))

    widths = (Q_WIDTH, 4 * KV_WIDTH, 2 * KV_WIDTH, N_KV_SETS * KV_WIDTH, D_CONV, GATE_PAD, 2 * D_MODEL)
    dtypes = (BF16, F32, F32, BF16, F32, F32, F32)
    return pl.pallas_call(
        _proj_kernel,
        grid=(n // tm,),
        in_specs=[pl.BlockSpec((tm, D_MODEL), lambda i: (i, 0)), mod_spec(1), mod_spec(0)]
                 + [_const_spec(w.shape) for w in wts],
        out_specs=[pl.BlockSpec((tm, w), lambda i: (i, 0)) for w in widths],
        out_shape=[jax.ShapeDtypeStruct((n, w), d) for w, d in zip(widths, dtypes)],
        compiler_params=_params("parallel"),
    )(x, sc, sh, *wts)


def _conv_kernel(cur_ref, halo_ref, dw_ref, dwb_ref, lg_ref, lb_ref, wpw_ref, o_ref, ext_ref):
    bb, tm, _ = cur_ref.shape
    ext_ref[:, :CONV_HALO, :] = halo_ref[...]
    ext_ref[:, CONV_HALO:, :] = cur_ref[...]
    first = CONV_HALO - (CONV_WIDTH - 1)
    acc = jnp.zeros((bb, tm, D_CONV), F32)
    for j in range(CONV_WIDTH):
        acc = acc + ext_ref[:, first + j:first + j + tm, :] * dw_ref[j:j + 1, :]
    y = _layer_norm(acc + dwb_ref[...], lg_ref[...], lb_ref[...])
    y = (y * _sigmoid(y)).reshape(bb * tm, D_CONV).astype(BF16)
    o_ref[...] = _dot(y, wpw_ref[...]).reshape(bb, tm, D_MODEL)


def _conv_branch(cur, halo, dw_w, dw_b, ln_g, ln_b, w_pw, bb):
    b, tm, _ = cur.shape
    return pl.pallas_call(
        _conv_kernel,
        grid=(b // bb,),
        in_specs=[pl.BlockSpec((bb, tm, D_CONV), lambda i: (i, 0, 0)),
                  pl.BlockSpec((bb, CONV_HALO, D_CONV), lambda i: (i, 0, 0)),
                  _const_spec((CONV_WIDTH, D_CONV)), _const_spec((1, D_CONV)),
                  _const_spec((1, D_CONV)), _const_spec((1, D_CONV)),
                  _const_spec((D_CONV, D_MODEL))],
        out_specs=pl.BlockSpec((bb, tm, D_MODEL), lambda i: (i, 0, 0)),
        out_shape=jax.ShapeDtypeStruct((b, tm, D_MODEL), F32),
        scratch_shapes=[pltpu.VMEM((bb, CONV_HALO + tm, D_CONV), F32)],
        compiler_params=_params("parallel"),
    )(cur, halo, dw_w, dw_b.reshape(1, -1), ln_g.reshape(1, -1), ln_b.reshape(1, -1), w_pw)


MERGE_TILE = 512
MOE_TILE = 1024
MOE_CHUNK = 256


def _merge_kernel(x_ref, cy_ref, oc_ref, os_ref, ow_ref, g_ref, mg_ref, g1_ref, sc2_ref, sh2_ref,
                  exp_ref, wo_ref, lg_ref, lb_ref, wr_ref, br_ref,
                  x1_ref, h2_ref, wt_ref, rank_ref, cnt_ref):
    tm = x_ref.shape[0]
    gh, gl = _split(g_ref[...])
    nsa = jnp.zeros((tm, D_MODEL), F32)
    for n, o_ref in enumerate((oc_ref, os_ref, ow_ref)):
        e = exp_ref[n]
        nsa = nsa + o_ref[...] * (_dot(gh, e) + _dot(gl, e))
    mix = mg_ref[:, :D_MODEL] * cy_ref[...] + mg_ref[:, D_MODEL:] * nsa
    y = _dot(mix.astype(BF16), wo_ref[...])
    x1 = _layer_norm(DN_ALPHA * x_ref[...] + g1_ref[...] * y, lg_ref[...], lb_ref[...])
    x1_ref[...] = x1
    h2 = x1 * (1.0 + sc2_ref[...]) + sh2_ref[...]
    h2_ref[...] = h2.astype(BF16)
    logits = _dot3(wr_ref[...], h2, _dot_nt) + br_ref[...]
    eidx = _iota(logits.shape, 0)
    picked = jnp.zeros(logits.shape, F32)
    wsum = jnp.zeros((1, tm), F32)
    wts = jnp.zeros(logits.shape, F32)
    v0 = None
    for k in range(TOP_K):
        v = jnp.max(logits, axis=0, keepdims=True)
        first = jnp.min(jnp.where(logits == v, eidx, N_EXPERTS), axis=0, keepdims=True)
        hit = eidx == first
        if k == 0:
            v0 = v
        ev = jnp.exp(v - v0)
        wts = wts + jnp.where(hit, ev, 0.0)
        wsum = wsum + ev
        picked = picked + jnp.where(hit, 1.0, 0.0)
        logits = jnp.where(hit, -jnp.inf, logits)
    wt_ref[0] = wts / wsum
    upper = jnp.where(_iota((tm, tm), 0) < _iota((tm, tm), 1), 1.0, 0.0).astype(BF16)
    rank_ref[0] = _dot(picked.astype(BF16), upper)
    cnt = jnp.sum(picked, axis=1, keepdims=True)
    cnt_ref[0] = jnp.broadcast_to(cnt, (N_EXPERTS, LANES)).astype(jnp.int32)


def _merge(x, conv_y, o_cmp, o_slc, o_win, g_nsa, merge_g, mod, gate_expand, w_out, ln_g, ln_b,
           w_router_t, b_router):
    n = x.shape[0]
    tm = MERGE_TILE
    nt = n // tm
    per_row = mod.shape[0] != 1

    def row_spec(w):
        return pl.BlockSpec((tm, w), lambda i: (i, 0))

    def mod_spec(col):
        return pl.BlockSpec((tm if per_row else 1, D_MODEL),
                            (lambda i: (i, col)) if per_row else (lambda i: (0, col)))

    tile_spec = pl.BlockSpec((1, N_EXPERTS, tm), lambda i: (i, 0, 0))
    return pl.pallas_call(
        _merge_kernel,
        grid=(nt,),
        in_specs=[row_spec(D_MODEL)] * 5 + [row_spec(GATE_PAD), row_spec(2 * D_MODEL),
                  mod_spec(2), mod_spec(4), mod_spec(3),
                  _const_spec(gate_expand.shape), _const_spec(w_out.shape),
                  _const_spec((1, D_MODEL)), _const_spec((1, D_MODEL)),
                  _const_spec(w_router_t.shape), _const_spec((N_EXPERTS, 1))],
        out_specs=[row_spec(D_MODEL), row_spec(D_MODEL), tile_spec, tile_spec,
                   pl.BlockSpec((1, N_EXPERTS, LANES), lambda i: (i, 0, 0))],
        out_shape=[jax.ShapeDtypeStruct((n, D_MODEL), F32), jax.ShapeDtypeStruct((n, D_MODEL), BF16),
                   jax.ShapeDtypeStruct((nt, N_EXPERTS, tm), F32),
                   jax.ShapeDtypeStruct((nt, N_EXPERTS, tm), F32),
                   jax.ShapeDtypeStruct((nt, N_EXPERTS, LANES), jnp.int32)],
        compiler_params=_params("parallel"),
    )(x, conv_y, o_cmp, o_slc, o_win, g_nsa, merge_g, mod, mod, mod, gate_expand, w_out,
      ln_g.reshape(1, -1), ln_b.reshape(1, -1), w_router_t, b_router.reshape(-1, 1))


def _moe_kernel(cnt_ref, x1_ref, h2_ref, wt_ref, rank_ref, g2_ref, wup_ref, bup_ref, wdn_ref, bdn_ref,
                lg_ref, lb_ref, o_ref, acc_ref):
    i = pl.program_id(0)
    e = pl.program_id(1)
    tm = x1_ref.shape[0]

    @pl.when(e == 0)
    def _():
        acc_ref[...] = jnp.zeros_like(acc_ref)

    cnt = cnt_ref[i * N_EXPERTS + e]
    for ch in range(tm // MOE_CHUNK):
        @pl.when(cnt > ch * MOE_CHUNK)
        def _():
            w_e = wt_ref[0, pl.ds(e, 1), :]
            slot = rank_ref[0, pl.ds(e, 1), :] - float(ch * MOE_CHUNK)
            rows = _iota((MOE_CHUNK, tm), 0).astype(F32)
            onehot = jnp.where((rows == slot) & (w_e > 0.0), 1.0, 0.0)
            wc = jnp.sum(onehot * w_e, axis=1, keepdims=True)
            sel = onehot.astype(BF16)
            xc = _dot(sel, h2_ref[...]).astype(BF16)
            u = _dot(xc, wup_ref[0]) + bup_ref[0]
            x_glu = jnp.minimum(u[:, :D_EXPERT], SWIGLU_LIMIT)
            x_lin = jnp.clip(u[:, D_EXPERT:], -SWIGLU_LIMIT, SWIGLU_LIMIT)
            a = x_glu * _sigmoid(SWIGLU_ALPHA * x_glu) * (x_lin + 1.0)
            y = (_dot(a.astype(BF16), wdn_ref[0]) + bdn_ref[0]) * wc
            acc_ref[...] += _dot_tn(sel, y.astype(BF16))

    @pl.when(e == N_EXPERTS - 1)
    def _():
        o_ref[...] = _layer_norm(DN_ALPHA * x1_ref[...] + g2_ref[...] * acc_ref[...],
                                 lg_ref[...], lb_ref[...])


def _moe(x1, h2, wt, rank, cnt, mod, w_up, b_up, w_down, b_down, ln_g, ln_b):
    n = x1.shape[0]
    tm = MOE_TILE
    nt = n // tm
    per_row = mod.shape[0] != 1
    g2_spec = pl.BlockSpec((tm if per_row else 1, D_MODEL),
                           (lambda i, e, c: (i, 5)) if per_row else (lambda i, e, c: (0, 5)))
    tile_spec = pl.BlockSpec((1, N_EXPERTS, tm), lambda i, e, c: (i, 0, 0))
    grid_spec = pltpu.PrefetchScalarGridSpec(
        num_scalar_prefetch=1,
        grid=(nt, N_EXPERTS),
        in_specs=[pl.BlockSpec((tm, D_MODEL), lambda i, e, c: (i, 0)),
                  pl.BlockSpec((tm, D_MODEL), lambda i, e, c: (i, 0)),
                  tile_spec, tile_spec, g2_spec,
                  pl.BlockSpec((1, D_MODEL, 2 * D_EXPERT), lambda i, e, c: (e, 0, 0)),
                  pl.BlockSpec((1, 1, 2 * D_EXPERT), lambda i, e, c: (e, 0, 0)),
                  pl.BlockSpec((1, D_EXPERT, D_MODEL), lambda i, e, c: (e, 0, 0)),
                  pl.BlockSpec((1, 1, D_MODEL), lambda i, e, c: (e, 0, 0)),
                  pl.BlockSpec((1, D_MODEL), lambda i, e, c: (0, 0)),
                  pl.BlockSpec((1, D_MODEL), lambda i, e, c: (0, 0))],
        out_specs=pl.BlockSpec((tm, D_MODEL), lambda i, e, c: (i, 0)),
        scratch_shapes=[pltpu.VMEM((tm, D_MODEL), F32)],
    )
    return pl.pallas_call(
        _moe_kernel,
        grid_spec=grid_spec,
        out_shape=jax.ShapeDtypeStruct((n, D_MODEL), F32),
        compiler_params=_params("parallel", "arbitrary"),
    )(cnt, x1, h2, wt, rank, mod, w_up, b_up.reshape(N_EXPERTS, 1, -1), w_down,
      b_down.reshape(N_EXPERTS, 1, -1), ln_g.reshape(1, -1), ln_b.reshape(1, -1))


def _softmax_rows(s, valid):
    s = jnp.where(valid, s, NEG_INF)
    m = jnp.max(s, axis=-1, keepdims=True)
    e = jnp.where(valid, jnp.exp(s - m), 0.0)
    l = jnp.sum(e, axis=-1, keepdims=True)
    return e / jnp.where(l > 0.0, l, 1.0)


def _flash_update(s, valid, v, m_old, l_old, acc_old, pv=_dot):
    s = jnp.where(valid, s, NEG_INF)
    m_new = jnp.maximum(m_old, jnp.max(s, axis=-1, keepdims=True))
    p = jnp.where(valid, jnp.exp(s - m_new), 0.0)
    alpha = jnp.exp(m_old - m_new)
    l_new = alpha * l_old + jnp.sum(p, axis=-1, keepdims=True)
    acc_new = alpha * acc_old + pv(p.astype(BF16), v)
    return m_new, l_new, acc_new


def _shift_right_one(x):
    return jnp.where(_iota(x.shape, 1) == 0, 0.0, pltpu.roll(x, 1, 1))


def _block_scores(imp, nj):
    parts = [imp[:, k * nj:(k + 1) * nj] for k in range(SEL_RATIO)]
    return (parts[0] + parts[1]) + (parts[2] + parts[3]) + _shift_right_one(parts[3])


def _select_blocks(score, qblk, n_blocks):
    j = _iota(score.shape, 1)
    back = qblk - j
    forced = (j == 0) | ((back >= 0) & (back < N_LOCAL))
    score = jnp.where(forced, FORCE_SCORE, jnp.where(back >= 0, score, -1.0))
    score = jnp.where(j < n_blocks, score, -jnp.inf)
    picked = jnp.zeros(score.shape, F32)
    for _ in range(N_SEL):
        m = jnp.max(score, axis=-1, keepdims=True)
        first = jnp.min(jnp.where(score == m, j, SEL_LANES), axis=-1, keepdims=True)
        hit = j == first
        picked = jnp.where(hit, 1.0, picked)
        score = jnp.where(hit, -jnp.inf, score)
    return picked


def _gelu_tanh(x):
    return x * (0.5 * (1.0 + jnp.tanh(0.7978845608028654 * (x + 0.044715 * (x * x * x)))))


def _compress_rows(buf_ref, pe_ref, w1_ref, w2_ref, nj):
    outs = []
    for pair in range(HEAD_PAIRS):
        acc = jnp.zeros((SEL_RATIO * nj, 2 * PHI_HIDDEN), F32)
        for s in range(CMP_LEN):
            parts = [buf_ref[pair, pl.ds(CMP_STRIDE * k + s, nj, stride=SEL_LEN), :] for k in range(SEL_RATIO)]
            x = (jnp.concatenate(parts, axis=0) + pe_ref[s:s + 1, :]).astype(BF16)
            acc = acc + _dot(x, w1_ref[s])
        outs.append(_dot(_gelu_tanh(acc).astype(BF16), w2_ref[...]))
    return jnp.concatenate(outs, axis=1)


CMP_TILE = 4096


def _cmp_prompt_kernel(x_ref, halo_ref, pe_ref, w1_ref, w2_ref, o_ref, buf_ref):
    t = pl.program_id(0)
    last = t == pl.num_programs(0) - 1
    for pair in range(HEAD_PAIRS):
        cols = slice(pair * LANES, (pair + 1) * LANES)
        buf_ref[pair, :CMP_TILE, :] = x_ref[:, cols]
        buf_ref[pair, CMP_TILE:, :] = jnp.where(last, 0.0, halo_ref[:, cols])
    nj = CMP_TILE // SEL_LEN
    out = _compress_rows(buf_ref, pe_ref, w1_ref, w2_ref, nj)
    o_ref[...] = out.reshape(SEL_RATIO, nj, KV_WIDTH).astype(BF16)


def _compress_prompt(kv4, kv_set, pe, w1, w2):
    t_len = kv4.shape[0]
    nt = t_len // CMP_TILE
    nj = CMP_TILE // SEL_LEN
    halo_blocks = CMP_TILE // SEL_LEN
    last_halo = t_len // SEL_LEN - 1
    out = pl.pallas_call(
        _cmp_prompt_kernel,
        grid=(nt,),
        in_specs=[pl.BlockSpec((CMP_TILE, KV_WIDTH), lambda t: (t, kv_set)),
                  pl.BlockSpec((SEL_LEN, KV_WIDTH),
                               lambda t: (jnp.minimum((t + 1) * halo_blocks, last_halo), kv_set)),
                  _const_spec(pe.shape), _const_spec(w1.shape), _const_spec(w2.shape)],
        out_specs=pl.BlockSpec((SEL_RATIO, nj, KV_WIDTH), lambda t: (0, t, 0)),
        out_shape=jax.ShapeDtypeStruct((SEL_RATIO, t_len // SEL_LEN, KV_WIDTH), BF16),
        scratch_shapes=[pltpu.VMEM((HEAD_PAIRS, CMP_TILE + SEL_LEN, LANES), F32)],
        compiler_params=_params("arbitrary"),
    )(kv4, kv4, pe, w1, w2)
    return out.reshape(t_len // CMP_STRIDE, KV_WIDTH)


def _cmp_attn_prompt_kernel(q_ref, kc_ref, vc_ref, o_ref, sel_ref, any_ref):
    nc = kc_ref.shape[0]
    nj = nc // SEL_RATIO
    s0 = pl.program_id(0) * Q_TILE
    tq = s0 + _iota((Q_TILE, 1), 0)
    col = _iota((1, nc), 1)
    ends = (SEL_RATIO * (col % nj) + col // nj) * CMP_STRIDE + (CMP_LEN - 1)
    dist = (tq - ends).astype(F32)
    valid = ends <= tq
    any_rows = []
    scores = []
    for g in range(N_KV_HEADS):
        kc = kc_ref[:, g * HEAD_DIM:(g + 1) * HEAD_DIM]
        vc = vc_ref[:, g * HEAD_DIM:(g + 1) * HEAD_DIM]
        imp = jnp.zeros((Q_TILE, nc), F32)
        for r in range(GROUP):
            h = g * GROUP + r
            cols = slice(h * HEAD_DIM, (h + 1) * HEAD_DIM)
            p = _softmax_rows(_dot_nt(q_ref[:, cols], kc) - SLOPES[h] * dist, valid)
            o_ref[:, cols] = _dot(p.astype(BF16), vc)
            imp = imp + p
        scores.append(_block_scores(imp, nj))
    picked = _select_blocks(jnp.concatenate(scores, axis=0),
                            jnp.concatenate([tq // SEL_LEN] * N_KV_HEADS, axis=0), nj)
    for g in range(N_KV_HEADS):
        sel_ref[g] = picked[g * Q_TILE:(g + 1) * Q_TILE].astype(BF16)
        any_rows.append(jnp.max(picked[g * Q_TILE:(g + 1) * Q_TILE], axis=0, keepdims=True))
    any_ref[0] = jnp.concatenate(any_rows + [jnp.zeros((ANY_ROWS - N_KV_HEADS, SEL_LANES), F32)], axis=0)


def _cmp_attn_prompt(q, kc, vc):
    t_len = q.shape[0]
    nt = t_len // Q_TILE
    return pl.pallas_call(
        _cmp_attn_prompt_kernel,
        grid=(nt,),
        in_specs=[pl.BlockSpec((Q_TILE, Q_WIDTH), lambda i: (i, 0)),
                  _const_spec(kc.shape), _const_spec(vc.shape)],
        out_specs=[pl.BlockSpec((Q_TILE, Q_WIDTH), lambda i: (i, 0)),
                   pl.BlockSpec((N_KV_HEADS, Q_TILE, SEL_LANES), lambda i: (0, i, 0)),
                   pl.BlockSpec((1, ANY_ROWS, SEL_LANES), lambda i: (i, 0, 0))],
        out_shape=[jax.ShapeDtypeStruct((t_len, Q_WIDTH), F32),
                   jax.ShapeDtypeStruct((N_KV_HEADS, t_len, SEL_LANES), BF16),
                   jax.ShapeDtypeStruct((nt, ANY_ROWS, SEL_LANES), F32)],
        compiler_params=_params("parallel"),
    )(q, kc, vc)


def _tile_flags(any_sel):
    nt = any_sel.shape[0]
    per_tile = KV_TILE // SEL_LEN
    hit = any_sel[:, :N_KV_HEADS].reshape(nt, N_KV_HEADS, SEL_LANES // per_tile, per_tile).max(-1) > 0.0
    bits = hit.reshape(nt, N_KV_HEADS, -1, FLAG_BITS).astype(jnp.uint32) << jnp.arange(FLAG_BITS, dtype=jnp.uint32)
    return lax.bitcast_convert_type(bits.sum(-1, dtype=jnp.uint32), jnp.int32).reshape(-1)


def _slc_attn_prompt_kernel(flag_ref, q_ref, k_ref, v_ref, sel_ref, o_ref, qs_ref, m_ref, l_ref, acc_ref):
    i = pl.program_id(0)
    tq = i * Q_TILE + _iota((Q_TILE, 1), 0)
    for h in range(N_HEADS):
        qs_ref[h // GROUP, (h % GROUP) * Q_TILE:(h % GROUP + 1) * Q_TILE, :] = q_ref[:, h * HEAD_DIM:(h + 1) * HEAD_DIM]
    m_ref[...] = jnp.full(m_ref.shape, NEG_INF, F32)
    l_ref[...] = jnp.zeros(l_ref.shape, F32)
    acc_ref[...] = jnp.zeros(acc_ref.shape, F32)

    def tile(t, carry):
        base = pl.multiple_of(t * KV_TILE, KV_TILE)
        dist_i = tq - (base + _iota((1, KV_TILE), 1))
        dist = dist_i.astype(F32)
        causal = dist_i >= 0
        expand = jnp.where(_iota((SEL_LANES, KV_TILE), 0)
                           == t * (KV_TILE // SEL_LEN) + _iota((SEL_LANES, KV_TILE), 1) // SEL_LEN,
                           1.0, 0.0).astype(BF16)
        for g in range(N_KV_HEADS):
            word = flag_ref[(i * N_KV_HEADS + g) * FLAG_WORDS + t // FLAG_BITS]

            @pl.when(((word >> (t % FLAG_BITS)) & 1) == 1)
            def _():
                valid = (_dot(sel_ref[g], expand) > 0.5) & causal
                kt = k_ref[pl.ds(base, KV_TILE), g * HEAD_DIM:(g + 1) * HEAD_DIM]
                vt = v_ref[pl.ds(base, KV_TILE), g * HEAD_DIM:(g + 1) * HEAD_DIM]
                raw = _dot_nt(qs_ref[g], kt)
                s = jnp.concatenate(
                    [jnp.where(valid, raw[r * Q_TILE:(r + 1) * Q_TILE] - SLOPES[g * GROUP + r] * dist, NEG_INF)
                     for r in range(GROUP)], axis=0)
                m_old = m_ref[g]
                m_new = jnp.maximum(m_old, jnp.max(s, axis=-1, keepdims=True))
                p = jnp.exp(s - jnp.concatenate([m_new] * (KV_TILE // LANES), axis=1))
                alpha = jnp.exp(m_old - m_new)
                l_ref[g] = alpha * l_ref[g] + jnp.sum(p, axis=-1, keepdims=True)
                acc_ref[g] = alpha[:, :HEAD_DIM] * acc_ref[g] + _dot(p.astype(BF16), vt)
                m_ref[g] = m_new
        return carry

    lax.fori_loop(0, (i * Q_TILE) // KV_TILE + 1, tile, 0)
    for h in range(N_HEADS):
        rows = slice((h % GROUP) * Q_TILE, (h % GROUP + 1) * Q_TILE)
        l = l_ref[h // GROUP, rows, :HEAD_DIM]
        o_ref[:, h * HEAD_DIM:(h + 1) * HEAD_DIM] = acc_ref[h // GROUP, rows, :] / jnp.where(l > 0.0, l, 1.0)


def _slc_attn_prompt(q, kvb, sel, flags):
    t_len = q.shape[0]
    resident = functools.partial(pl.BlockSpec, pipeline_mode=pl.Buffered(1))
    rows = GROUP * Q_TILE
    grid_spec = pltpu.PrefetchScalarGridSpec(
        num_scalar_prefetch=1,
        grid=(t_len // Q_TILE,),
        in_specs=[pl.BlockSpec((Q_TILE, Q_WIDTH), lambda i, f: (i, 0)),
                  resident((t_len, KV_WIDTH), lambda i, f: (0, 2)),
                  resident((t_len, KV_WIDTH), lambda i, f: (0, 3)),
                  pl.BlockSpec((N_KV_HEADS, Q_TILE, SEL_LANES), lambda i, f: (0, i, 0))],
        out_specs=pl.BlockSpec((Q_TILE, Q_WIDTH), lambda i, f: (i, 0)),
        scratch_shapes=[pltpu.VMEM((N_KV_HEADS, rows, HEAD_DIM), BF16),
                        pltpu.VMEM((N_KV_HEADS, rows, LANES), F32), pltpu.VMEM((N_KV_HEADS, rows, LANES), F32),
                        pltpu.VMEM((N_KV_HEADS, rows, HEAD_DIM), F32)],
    )
    return pl.pallas_call(
        _slc_attn_prompt_kernel,
        grid_spec=grid_spec,
        out_shape=jax.ShapeDtypeStruct((t_len, Q_WIDTH), F32),
        compiler_params=_params("parallel"),
    )(flags, q, kvb, kvb, sel)


def _win_attn_prompt_kernel(q_ref, k_ref, v_ref, o_ref):
    s0 = pl.multiple_of(pl.program_id(0) * Q_TILE, Q_TILE)
    n_keys = WINDOW + Q_TILE
    tq = s0 + _iota((Q_TILE, 1), 0)
    pw = s0 - WINDOW + _iota((1, n_keys), 1)
    dw = tq - pw
    valid = (dw >= 0) & (dw < WINDOW) & (pw >= 0)
    dist = dw.astype(F32)
    for g in range(N_KV_HEADS):
        kw = k_ref[pl.ds(s0, n_keys), g * HEAD_DIM:(g + 1) * HEAD_DIM]
        vw = v_ref[pl.ds(s0, n_keys), g * HEAD_DIM:(g + 1) * HEAD_DIM]
        for r in range(GROUP):
            h = g * GROUP + r
            cols = slice(h * HEAD_DIM, (h + 1) * HEAD_DIM)
            p = _softmax_rows(_dot_nt(q_ref[:, cols], kw) - SLOPES[h] * dist, valid)
            o_ref[:, cols] = _dot(p.astype(BF16), vw)


def _win_attn_prompt(q, kw_pad, vw_pad):
    t_len = q.shape[0]
    resident = functools.partial(pl.BlockSpec, pipeline_mode=pl.Buffered(1))
    return pl.pallas_call(
        _win_attn_prompt_kernel,
        grid=(t_len // Q_TILE,),
        in_specs=[pl.BlockSpec((Q_TILE, Q_WIDTH), lambda i: (i, 0)),
                  resident(kw_pad.shape, lambda i: (0, 0)), resident(vw_pad.shape, lambda i: (0, 0))],
        out_specs=pl.BlockSpec((Q_TILE, Q_WIDTH), lambda i: (i, 0)),
        out_shape=jax.ShapeDtypeStruct((t_len, Q_WIDTH), F32),
        compiler_params=_params("parallel"),
    )(q, kw_pad, vw_pad)


S_ROWS = N_HEADS * DEC_SEQ
S_TAIL = SEL_LEN
S_CMP_BLOCKS = PAST_LEN // CMP_STRIDE
S_CHUNK = 2048


def _fetch_pages(pt_ref, sem_ref, page_copy):
    b = pl.program_id(0)
    slot = b % 2

    def copy(seq, p, dst_slot):
        return page_copy(pt_ref[seq * N_PAGES + p], p, dst_slot, sem_ref.at[dst_slot])

    def start(seq, dst_slot):
        def body(p, c):
            copy(seq, p, dst_slot).start()
            return c
        lax.fori_loop(0, N_PAGES, body, 0)

    @pl.when(b == 0)
    def _():
        start(b, slot)

    @pl.when(b + 1 < pl.num_programs(0))
    def _():
        start(b + 1, 1 - slot)

    def wait(p, c):
        copy(b, p, slot).wait()
        return c
    lax.fori_loop(0, N_PAGES, wait, 0)
    return slot


S_CHUNKS = PAST_LEN // CMP_STRIDE + ANY_ROWS


def _cmp_sample_kernel(kv_set, pt_ref, cache_ref, new_ref, pet_ref, w1_ref, w2_ref, o_ref,
                       stage_ref, buf_ref, sem_ref):
    def page_copy(page, p, slot, sem):
        return pltpu.make_async_copy(cache_ref.at[page, kv_set],
                                     stage_ref.at[slot, pl.ds(p * KV_WIDTH, KV_WIDTH), :], sem)

    slot = _fetch_pages(pt_ref, sem_ref, page_copy)
    per_page = PAGE_SIZE // CMP_STRIDE

    def to_rows(p, c):
        row_out = _iota((PAGE_SIZE, PAGE_SIZE), 0)
        pos_in = _iota((PAGE_SIZE, PAGE_SIZE), 1)
        regroup = jnp.where((pos_in % CMP_STRIDE) * per_page + pos_in // CMP_STRIDE == row_out,
                            1.0, 0.0).astype(BF16)
        src = pl.multiple_of(p * KV_WIDTH, KV_WIDTH)
        dst = pl.multiple_of(p * per_page, per_page)
        for pair in range(HEAD_PAIRS):
            x_t = stage_ref[slot, pl.ds(src + pair * LANES, LANES), :].astype(BF16)
            rows = _dot_nt(regroup, x_t)
            for s in range(CMP_STRIDE):
                buf_ref[pair, s, pl.ds(dst, per_page), :] = rows[s * per_page:(s + 1) * per_page, :]
        return c
    lax.fori_loop(0, N_PAGES, to_rows, 0)
    first_row = _iota((ANY_ROWS, LANES), 0) == 0
    for pair in range(HEAD_PAIRS):
        for s in range(CMP_STRIDE):
            tail = jnp.zeros((ANY_ROWS, LANES), F32)
            if s < DEC_SEQ:
                row = new_ref[s:s + 1, pair * LANES:(pair + 1) * LANES].astype(BF16).astype(F32)
                tail = jnp.where(first_row, row, 0.0)
            buf_ref[pair, s, PAST_LEN // CMP_STRIDE:, :] = tail

    bias = jnp.zeros((1, 2 * PHI_HIDDEN), F32)
    for s in range(CMP_LEN):
        bias = bias + jnp.sum(pet_ref[:, s:s + 1] * w1_ref[s].astype(F32), axis=0, keepdims=True)
    nj = S_CMP_BLOCKS // SEL_RATIO
    outs = []
    for pair in range(HEAD_PAIRS):
        acc = jnp.zeros((SEL_RATIO * nj, 2 * PHI_HIDDEN), F32)
        for s in range(CMP_LEN):
            parts = [buf_ref[pair, s % CMP_STRIDE, pl.ds(k + s // CMP_STRIDE, nj, stride=SEL_RATIO), :]
                     for k in range(SEL_RATIO)]
            acc = acc + _dot(jnp.concatenate(parts, axis=0).astype(BF16), w1_ref[s])
        outs.append(_dot(_gelu_tanh(acc + bias).astype(BF16), w2_ref[...]))
    o_ref[0] = jnp.concatenate(outs, axis=1).astype(BF16)


def _compress_sample(page_table, cache, kv4_new, kv_set, pe, w1, w2):
    pe_t = pe.T
    grid_spec = pltpu.PrefetchScalarGridSpec(
        num_scalar_prefetch=1,
        grid=(DEC_BATCH,),
        in_specs=[pl.BlockSpec(memory_space=pl.ANY),
                  pl.BlockSpec((DEC_SEQ, KV_WIDTH), lambda b, pt: (b, kv_set)),
                  pl.BlockSpec(pe_t.shape, lambda b, pt: (0, 0)),
                  pl.BlockSpec(w1.shape, lambda b, pt: (0, 0, 0)),
                  pl.BlockSpec(w2.shape, lambda b, pt: (0, 0))],
        out_specs=pl.BlockSpec((1, S_CMP_BLOCKS, KV_WIDTH), lambda b, pt: (b, 0, 0)),
        scratch_shapes=[pltpu.VMEM((2, N_PAGES * KV_WIDTH, PAGE_SIZE), F32),
                        pltpu.VMEM((HEAD_PAIRS, CMP_STRIDE, S_CHUNKS, LANES), F32),
                        pltpu.SemaphoreType.DMA((2,))],
    )
    return pl.pallas_call(
        functools.partial(_cmp_sample_kernel, kv_set),
        grid_spec=grid_spec,
        out_shape=jax.ShapeDtypeStruct((DEC_BATCH, S_CMP_BLOCKS, KV_WIDTH), BF16),
        compiler_params=_params("arbitrary"),
    )(page_table.reshape(-1), cache, kv4_new, pe_t, w1, w2)


def _row_queries():
    return PAST_LEN + _iota((S_ROWS, 1), 0) % DEC_SEQ


def _cmp_attn_sample_kernel(q_ref, slope_ref, kc_ref, vc_ref, o_ref, sel_ref):
    nc = S_CMP_BLOCKS
    nj = nc // SEL_RATIO
    tq = _row_queries()
    col = _iota((1, nc), 1)
    ends = (SEL_RATIO * (col % nj) + col // nj) * CMP_STRIDE + (CMP_LEN - 1)
    s = _dot_nt(q_ref[0], kc_ref[0]) - slope_ref[...] * (tq - ends).astype(F32)
    p = _softmax_rows(s, ends <= tq)
    o_ref[0] = _dot(p.astype(BF16), vc_ref[0])
    rows_g = GROUP * DEC_SEQ
    n_blocks = (PAST_LEN + DEC_SEQ + SEL_LEN - 1) // SEL_LEN
    qblk = (PAST_LEN + _iota((DEC_SEQ, 1), 0)) // SEL_LEN
    scores = []
    for g in range(N_KV_HEADS):
        imp = jnp.zeros((DEC_SEQ, nc), F32)
        for r in range(GROUP):
            imp = imp + p[g * rows_g + r * DEC_SEQ:g * rows_g + (r + 1) * DEC_SEQ]
        scores.append(jnp.concatenate([_block_scores(imp, nj), jnp.zeros((DEC_SEQ, SEL_LANES - nj), F32)], axis=1))
    picked = _select_blocks(jnp.concatenate(scores, axis=0), jnp.concatenate([qblk] * N_KV_HEADS, axis=0), n_blocks)
    sel_ref[0] = jnp.concatenate([picked[g * DEC_SEQ:(g + 1) * DEC_SEQ] for g in range(N_KV_HEADS)
                                  for _ in range(GROUP)], axis=0).astype(BF16)


def _cmp_attn_sample(qbd, slope_col, kc, vc):
    seq_spec = lambda rows, w: pl.BlockSpec((1, rows, w), lambda b: (b, 0, 0))
    return pl.pallas_call(
        _cmp_attn_sample_kernel,
        grid=(DEC_BATCH,),
        in_specs=[seq_spec(S_ROWS, KV_WIDTH), _const_spec((S_ROWS, 1)),
                  seq_spec(S_CMP_BLOCKS, KV_WIDTH), seq_spec(S_CMP_BLOCKS, KV_WIDTH)],
        out_specs=[seq_spec(S_ROWS, KV_WIDTH), seq_spec(S_ROWS, SEL_LANES)],
        out_shape=[jax.ShapeDtypeStruct((DEC_BATCH, S_ROWS, KV_WIDTH), F32),
                   jax.ShapeDtypeStruct((DEC_BATCH, S_ROWS, SEL_LANES), BF16)],
        compiler_params=_params("parallel"),
    )(qbd, slope_col, kc, vc)


def _pad_new_rows(x):
    return jnp.concatenate([x, jnp.zeros((LANES - DEC_SEQ, x.shape[1]), F32)], axis=0).astype(BF16)


def _slc_win_sample_kernel(pt_ref, cache_ref, q_ref, slope_ref, sel_ref, new_ref, cwin_ref, wnew_ref,
                           os_ref, ow_ref, buf_ref, sem_ref):
    def page_copy(page, p, slot, sem):
        return pltpu.make_async_copy(cache_ref.at[page, pl.ds(2, 2)],
                                     buf_ref.at[slot, :, :, pl.ds(p * PAGE_SIZE, PAGE_SIZE)], sem)

    slot = _fetch_pages(pt_ref, sem_ref, page_copy)
    q = q_ref[0]
    slope = slope_ref[...]
    sel = sel_ref[0]
    tq = _row_queries()
    new_pos = PAST_LEN + _iota((1, LANES), 1)
    new_ok = (tq - new_pos >= 0) & (new_pos < PAST_LEN + DEC_SEQ)
    new_dist = (tq - new_pos).astype(F32)

    m = jnp.full((S_ROWS, 1), NEG_INF, F32)
    l = jnp.zeros((S_ROWS, 1), F32)
    acc = jnp.zeros((S_ROWS, KV_WIDTH), F32)
    for c in range(PAST_LEN // S_CHUNK):
        k_t = buf_ref[slot, 0, :, c * S_CHUNK:(c + 1) * S_CHUNK].astype(BF16)
        v_t = buf_ref[slot, 1, :, c * S_CHUNK:(c + 1) * S_CHUNK].astype(BF16)
        pos = c * S_CHUNK + _iota((1, S_CHUNK), 1)
        expand = jnp.where(_iota((SEL_LANES, S_CHUNK), 0)
                           == (c * S_CHUNK + _iota((SEL_LANES, S_CHUNK), 1)) // SEL_LEN, 1.0, 0.0).astype(BF16)
        valid = (_dot(sel, expand) > 0.5) & (tq - pos >= 0)
        s = _dot(q, k_t) - slope * (tq - pos).astype(F32)
        m, l, acc = _flash_update(s, valid, v_t, m, l, acc, pv=_dot_nt)
    new = _pad_new_rows(new_ref[...])
    in_last = jnp.sum(jnp.where(_iota((S_ROWS, SEL_LANES), 1) == PAST_LEN // SEL_LEN, sel.astype(F32), 0.0),
                      axis=1, keepdims=True) > 0.5
    s = _dot_nt(q, new[:, :KV_WIDTH]) - slope * new_dist
    m, l, acc = _flash_update(s, in_last & new_ok, new[:, KV_WIDTH:], m, l, acc)
    os_ref[0] = acc / jnp.where(l > 0.0, l, 1.0)

    wb = cwin_ref.shape[-1]
    pw = PAST_LEN - wb + _iota((1, wb), 1)
    dw = tq - pw
    s_past = _dot(q, cwin_ref[0, 0].astype(BF16)) - slope * dw.astype(F32)
    wnew = _pad_new_rows(wnew_ref[...])
    s_new = _dot_nt(q, wnew[:, :KV_WIDTH]) - slope * new_dist
    valid = jnp.concatenate([(dw >= 0) & (dw < WINDOW) & (pw >= 0), new_ok & (tq - new_pos < WINDOW)], axis=1)
    p = _softmax_rows(jnp.concatenate([s_past, s_new], axis=1), valid).astype(BF16)
    ow_ref[0] = _dot_nt(p[:, :wb], cwin_ref[0, 1].astype(BF16)) + _dot(p[:, wb:], wnew[:, KV_WIDTH:])


def _slc_win_sample(page_table, cache, qbd, slope_col, sel, kv4_new, cache_win, win_new):
    wb = cache_win.shape[-1]
    seq_spec = lambda rows, w: pl.BlockSpec((1, rows, w), lambda b, pt: (b, 0, 0))
    grid_spec = pltpu.PrefetchScalarGridSpec(
        num_scalar_prefetch=1,
        grid=(DEC_BATCH,),
        in_specs=[pl.BlockSpec(memory_space=pl.ANY),
                  seq_spec(S_ROWS, KV_WIDTH),
                  pl.BlockSpec((S_ROWS, 1), lambda b, pt: (0, 0)),
                  seq_spec(S_ROWS, SEL_LANES),
                  pl.BlockSpec((DEC_SEQ, 2 * KV_WIDTH), lambda b, pt: (b, 1)),
                  pl.BlockSpec((1, 2, KV_WIDTH, wb), lambda b, pt: (b, 0, 0, 0)),
                  pl.BlockSpec((DEC_SEQ, 2 * KV_WIDTH), lambda b, pt: (b, 0))],
        out_specs=[seq_spec(S_ROWS, KV_WIDTH), seq_spec(S_ROWS, KV_WIDTH)],
        scratch_shapes=[pltpu.VMEM((2, 2, KV_WIDTH, PAST_LEN), F32), pltpu.SemaphoreType.DMA((2,))],
    )
    return pl.pallas_call(
        _slc_win_sample_kernel,
        grid_spec=grid_spec,
        out_shape=[jax.ShapeDtypeStruct((DEC_BATCH, S_ROWS, KV_WIDTH), F32)] * 2,
        compiler_params=_params("arbitrary"),
    )(page_table.reshape(-1), cache, qbd, slope_col, sel, kv4_new, cache_win, win_new)


def _gate_expand():
    rows = jnp.arange(GATE_PAD)[:, None]
    cols = jnp.arange(D_MODEL)[None, :]
    return jnp.stack([(rows == (cols // HEAD_DIM) * N_NSA_BRANCH + n) for n in range(N_NSA_BRANCH)]
                     ).astype(BF16)


def _compress_weights(pe, w1, w2):
    per_tile = LANES // HEAD_DIM
    eye = jnp.eye(per_tile, dtype=F32)
    w1_big = jnp.einsum('sde,gh->sgdhe', w1, eye).reshape(CMP_LEN, LANES, per_tile * PHI_HIDDEN).astype(BF16)
    w2_big = jnp.einsum('ed,gh->gehd', w2, eye).reshape(per_tile * PHI_HIDDEN, LANES).astype(BF16)
    return jnp.tile(pe, (1, per_tile)), w1_big, w2_big


def _rows_from_heads(o):
    o = o.reshape(DEC_BATCH, N_KV_HEADS, GROUP, DEC_SEQ, N_KV_HEADS, HEAD_DIM)
    idx = jnp.arange(N_KV_HEADS)
    own = o[:, idx, :, :, idx]
    return own.transpose(1, 3, 0, 2, 4).reshape(DEC_BATCH * DEC_SEQ, Q_WIDTH)


def kernel(x_prompt, x_sample, c_prompt, c_sample, cache_kv, cache_win, state_conv, page_table, w_ada, b_ada, w_in, conv_dw_w, conv_dw_b, conv_ln_g, conv_ln_b, w_conv_out, cmp_pe_k, cmp_w1_k, cmp_w2_k, cmp_pe_v, cmp_w1_v, cmp_w2_v, w_out, ln1_g, ln1_b, w_router, b_router, w_up, b_up, w_down, b_down, ln2_g, ln2_b):
    l = 0
    n_p = SEQ
    n_s = DEC_BATCH * DEC_SEQ

    c_all = jnp.concatenate([c_prompt, c_sample, jnp.zeros((7, D_MODEL), F32)], axis=0)
    mod = _adaln(c_all, w_ada[l], b_ada[l])
    mod_p = mod[0:1]
    mod_s = jnp.repeat(mod[1:1 + DEC_BATCH], DEC_SEQ, axis=0)

    wi = w_in[l]
    wts = ((wi[:, :OFF_KV] * (HEAD_DIM ** -0.5)).astype(BF16),
           wi[:, OFF_KV:OFF_GLU].astype(BF16),
           wi[:, OFF_GLU:OFF_NSA_G].astype(BF16),
           jnp.pad(wi[:, OFF_NSA_G:OFF_MERGE], ((0, 0), (0, GATE_PAD - N_HEADS * N_NSA_BRANCH))).astype(BF16),
           wi[:, OFF_MERGE:].astype(BF16))
    gate_expand = _gate_expand()
    w_out_b = w_out[l].astype(BF16)
    w_pw_b = w_conv_out[l].astype(BF16)
    w_up_b = w_up[l].astype(BF16)
    w_down_b = w_down[l].astype(BF16)
    w_router_t = w_router[l].T
    cmp_k = _compress_weights(cmp_pe_k[l], cmp_w1_k[l], cmp_w2_k[l])
    cmp_v = _compress_weights(cmp_pe_v[l], cmp_w1_v[l], cmp_w2_v[l])
    conv_w = (conv_dw_w[l], conv_dw_b[l], conv_ln_g[l], conv_ln_b[l], w_pw_b)

    def tail(x, conv_y, o3, g_nsa, merge_g, m):
        x1, h2, wt, rank, cnt = _merge(x, conv_y, *o3, g_nsa, merge_g, m, gate_expand, w_out_b,
                                       ln1_g[l], ln1_b[l], w_router_t, b_router[l])
        per = MOE_TILE // MERGE_TILE
        cnt = cnt[:, :, 0].reshape(-1, per, N_EXPERTS)
        before = jnp.cumsum(cnt, axis=1) - cnt
        rank = rank.reshape(-1, per, N_EXPERTS, MERGE_TILE) + before[..., None].astype(F32)
        rank = rank.transpose(0, 2, 1, 3).reshape(-1, N_EXPERTS, MOE_TILE)
        wt = wt.reshape(-1, per, N_EXPERTS, MERGE_TILE).transpose(0, 2, 1, 3).reshape(-1, N_EXPERTS, MOE_TILE)
        return _moe(x1, h2, wt, rank, cnt.sum(1).reshape(-1), m, w_up_b, b_up[l], w_down_b, b_down[l],
                    ln2_g[l], ln2_b[l])

    xp = x_prompt.reshape(n_p, D_MODEL)
    q, kv4, win2, kvb, u, g_nsa, merge_g = _project(xp, mod_p, mod_p, wts, 256)
    ct = 512
    u3 = u.reshape(n_p // ct, ct, D_CONV)
    halo = jnp.concatenate([jnp.zeros((1, CONV_HALO, D_CONV), F32), u3[:-1, ct - CONV_HALO:]], axis=0)
    conv_y = _conv_branch(u3, halo, *conv_w, 1).reshape(n_p, D_MODEL)
    kc = _compress_prompt(kv4, 0, *cmp_k)
    vc = _compress_prompt(kv4, 1, *cmp_v)
    o_cmp, sel, any_sel = _cmp_attn_prompt(q, kc, vc)
    o_slc = _slc_attn_prompt(q, kvb, sel, _tile_flags(any_sel))
    kw_pad = jnp.pad(kvb[:, 4 * KV_WIDTH:5 * KV_WIDTH], ((WINDOW, 0), (0, 0)))
    vw_pad = jnp.pad(kvb[:, 5 * KV_WIDTH:], ((WINDOW, 0), (0, 0)))
    o_win = _win_attn_prompt(q, kw_pad, vw_pad)
    y_p = tail(xp, conv_y, (o_cmp, o_slc, o_win), g_nsa, merge_g, mod_p)
    out_kv_p = kv4.reshape(1, 1, n_p, 4, N_KV_HEADS, HEAD_DIM)
    out_win_p = win2[n_p - WINDOW:].reshape(1, 1, WINDOW, 2, N_KV_HEADS, HEAD_DIM)
    out_conv_p = u[n_p - (CONV_WIDTH - 1):].reshape(1, 1, CONV_WIDTH - 1, D_CONV)

    xs = x_sample.reshape(n_s, D_MODEL)
    q, kv4, win2, kvb, u, g_nsa, merge_g = _project(xs, mod_s, mod_s, wts, 256)
    u3 = u.reshape(DEC_BATCH, DEC_SEQ, D_CONV)
    st = state_conv[l]
    halo = jnp.concatenate([jnp.zeros((DEC_BATCH, CONV_HALO - (CONV_WIDTH - 1), D_CONV), F32), st], axis=1)
    conv_y = _conv_branch(u3, halo, *conv_w, DEC_BATCH).reshape(n_s, D_MODEL)
    qh = q.reshape(DEC_BATCH, DEC_SEQ, N_KV_HEADS, GROUP, HEAD_DIM).transpose(0, 2, 3, 1, 4)
    qbd = jnp.einsum('bgrqd,gh->bgrqhd', qh, jnp.eye(N_KV_HEADS, dtype=BF16)).reshape(DEC_BATCH, S_ROWS, KV_WIDTH)
    slope_col = jnp.repeat(jnp.asarray(SLOPES, F32), DEC_SEQ).reshape(S_ROWS, 1)
    cache = cache_kv[l].transpose(0, 2, 3, 4, 1).reshape(-1, 4, KV_WIDTH, PAGE_SIZE)
    cwin = cache_win[l].transpose(0, 2, 3, 4, 1).reshape(DEC_BATCH, 2, KV_WIDTH, -1)
    kc = _compress_sample(page_table, cache, kv4, 0, *cmp_k)
    vc = _compress_sample(page_table, cache, kv4, 1, *cmp_v)
    o_cmp, sel = _cmp_attn_sample(qbd, slope_col, kc, vc)
    o_slc, o_win = _slc_win_sample(page_table, cache, qbd, slope_col, sel, kv4, cwin, win2)
    o3 = tuple(_rows_from_heads(o) for o in (o_cmp, o_slc, o_win))
    y_s = tail(xs, conv_y, o3, g_nsa, merge_g, mod_s)
    out_kv_s = kv4.reshape(1, DEC_BATCH, DEC_SEQ, 4, N_KV_HEADS, HEAD_DIM)
    win_new = win2.reshape(DEC_BATCH, DEC_SEQ, 2, N_KV_HEADS, HEAD_DIM)
    out_win_s = jnp.concatenate([cache_win[l], win_new], axis=1)[:, DEC_SEQ:][None]
    out_conv_s = jnp.concatenate([st, u3], axis=1)[:, DEC_SEQ:][None]

    return (y_p.reshape(1, n_p, D_MODEL), y_s.reshape(DEC_BATCH, DEC_SEQ, D_MODEL),
            out_kv_p, out_kv_s, out_win_p, out_win_s, out_conv_p, out_conv_s)
```

```python
import functools

import jax
import jax.numpy as jnp
from jax import lax
from jax.experimental import pallas as pl
from jax.experimental.pallas import tpu as pltpu

D_MODEL = 1024
SEQ = 16384
DEC_BATCH = 128
DEC_SEQ = 8
PAST_LEN = 8192
PAGE_SIZE = 128
N_PAGES = PAST_LEN // PAGE_SIZE
N_HEADS = 16
HEAD_DIM = 64
N_KV_HEADS = 4
GROUP = N_HEADS // N_KV_HEADS
KV_WIDTH = N_KV_HEADS * HEAD_DIM
Q_WIDTH = N_HEADS * HEAD_DIM
N_KV_SETS = 6
N_NSA_BRANCH = 3
CMP_LEN = 32
CMP_STRIDE = 16
PHI_HIDDEN = 2 * HEAD_DIM
SEL_LEN = 64
SEL_RATIO = SEL_LEN // CMP_STRIDE
N_SEL = 16
N_LOCAL = 2
WINDOW = 512
D_CONV = D_MODEL // 2
CONV_WIDTH = 31
N_EXPERTS = 32
TOP_K = 4
D_EXPERT = D_MODEL
SWIGLU_LIMIT = 7.0
SWIGLU_ALPHA = 1.702
DN_ALPHA = 2.0 ** 0.25
LN_EPS = 1e-5
NEG_INF = -1e30
FORCE_SCORE = 1e9
OFF_KV = Q_WIDTH
OFF_GLU = OFF_KV + N_KV_SETS * KV_WIDTH
OFF_NSA_G = OFF_GLU + 2 * D_CONV
OFF_MERGE = OFF_NSA_G + N_HEADS * N_NSA_BRANCH

LANES = 128
GATE_PAD = LANES
CONV_HALO = 32
VMEM_LIMIT = 56 * 1024 * 1024
Q_TILE = 128
KV_TILE = 512
SEL_LANES = 256
HEAD_PAIRS = KV_WIDTH // LANES
ANY_ROWS = 8
FLAG_BITS = 32
FLAG_WORDS = SEQ // KV_TILE // FLAG_BITS
SLOPES = tuple(2.0 ** (-8.0 * (h + 1) / N_HEADS) for h in range(N_HEADS))

F32 = jnp.float32
BF16 = jnp.bfloat16


def _params(*sem):
    return pltpu.CompilerParams(dimension_semantics=sem, vmem_limit_bytes=VMEM_LIMIT)


def _dot(a, b):
    return jnp.dot(a, b, preferred_element_type=F32)


def _dot_nt(a, b):
    return lax.dot_general(a, b, (((1,), (1,)), ((), ())), preferred_element_type=F32)


def _dot_tn(a, b):
    return lax.dot_general(a, b, (((0,), (0,)), ((), ())), preferred_element_type=F32)


def _split(x):
    hi = x.astype(BF16)
    lo = (x - hi.astype(F32)).astype(BF16)
    return hi, lo


def _dot3(a, b, dot=_dot):
    ah, al = _split(a)
    bh, bl = _split(b)
    return dot(ah, bh) + (dot(ah, bl) + dot(al, bh))


def _sigmoid(x):
    return 1.0 / (1.0 + jnp.exp(-x))


def _layer_norm(x, g, b):
    mu = jnp.mean(x, axis=-1, keepdims=True)
    xc = x - mu
    var = jnp.mean(xc * xc, axis=-1, keepdims=True)
    return xc * lax.rsqrt(var + LN_EPS) * g + b


def _const_spec(shape):
    return pl.BlockSpec(shape, lambda *_: (0,) * len(shape))


def _iota(shape, axis):
    return lax.broadcasted_iota(jnp.int32, shape, axis)


def _ada_kernel(c_ref, w_ref, b_ref, o_ref):
    o_ref[...] = _dot3(c_ref[...], w_ref[...]) + b_ref[...]


def _adaln(c_all, w_ada, b_ada):
    n = c_all.shape[0]
    tn = 1536
    return pl.pallas_call(
        _ada_kernel,
        grid=(6 * D_MODEL // tn,),
        in_specs=[pl.BlockSpec((n, D_MODEL), lambda j: (0, 0)),
                  pl.BlockSpec((D_MODEL, tn), lambda j: (0, j)),
                  pl.BlockSpec((1, tn), lambda j: (0, j))],
        out_specs=pl.BlockSpec((n, tn), lambda j: (0, j)),
        out_shape=jax.ShapeDtypeStruct((n, 6 * D_MODEL), F32),
        compiler_params=_params("parallel"),
    )(c_all, w_ada, b_ada.reshape(1, -1))


def _proj_kernel(x_ref, sc_ref, sh_ref, wq_ref, wkv_ref, wglu_ref, wg_ref, wm_ref,
                 q_ref, kv_ref, win_ref, kvb_ref, u_ref, g_ref, m_ref):
    h = (x_ref[...] * (1.0 + sc_ref[...]) + sh_ref[...]).astype(BF16)
    q_ref[...] = _dot(h, wq_ref[...]).astype(BF16)
    kv = _dot(h, wkv_ref[...])
    kv_ref[...] = kv[:, :4 * KV_WIDTH]
    win_ref[...] = kv[:, 4 * KV_WIDTH:]
    kvb_ref[...] = kv.astype(BF16)
    glu = _dot(h, wglu_ref[...])
    u_ref[...] = glu[:, :D_CONV] * _sigmoid(glu[:, D_CONV:])
    g_ref[...] = _sigmoid(_dot(h, wg_ref[...]))
    m_ref[...] = _sigmoid(_dot(h, wm_ref[...]))


def _project(x, sc, sh, wts, tm):
    n = x.shape[0]
    per_row = sc.shape[0] != 1
    mod_rows = tm if per_row else 1

    def mod_spec(col):
        return pl.BlockSpec((mod_rows, D_MODEL), (lambda i: (i, col)) if per_row else (lambda i: (0, col)))

    widths = (Q_WIDTH, 4 * KV_WIDTH, 2 * KV_WIDTH, N_KV_SETS * KV_WIDTH, D_CONV, GATE_PAD, 2 * D_MODEL)
    dtypes = (BF16, F32, F32, BF16, F32, F32, F32)
    return pl.pallas_call(
        _proj_kernel,
        grid=(n // tm,),
        in_specs=[pl.BlockSpec((tm, D_MODEL), lambda i: (i, 0)), mod_spec(1), mod_spec(0)]
                 + [_const_spec(w.shape) for w in wts],
        out_specs=[pl.BlockSpec((tm, w), lambda i: (i, 0)) for w in widths],
        out_shape=[jax.ShapeDtypeStruct((n, w), d) for w, d in zip(widths, dtypes)],
        compiler_params=_params("parallel"),
    )(x, sc, sh, *wts)


def _conv_kernel(cur_ref, halo_ref, dw_ref, dwb_ref, lg_ref, lb_ref, wpw_ref, o_ref, ext_ref):
    bb, tm, _ = cur_ref.shape
    ext_ref[:, :CONV_HALO, :] = halo_ref[...]
    ext_ref[:, CONV_HALO:, :] = cur_ref[...]
    first = CONV_HALO - (CONV_WIDTH - 1)
    acc = jnp.zeros((bb, tm, D_CONV), F32)
    for j in range(CONV_WIDTH):
        acc = acc + ext_ref[:, first + j:first + j + tm, :] * dw_ref[j:j + 1, :]
    y = _layer_norm(acc + dwb_ref[...], lg_ref[...], lb_ref[...])
    y = (y * _sigmoid(y)).reshape(bb * tm, D_CONV).astype(BF16)
    o_ref[...] = _dot(y, wpw_ref[...]).reshape(bb, tm, D_MODEL)


def _conv_branch(cur, halo, dw_w, dw_b, ln_g, ln_b, w_pw, bb):
    b, tm, _ = cur.shape
    return pl.pallas_call(
        _conv_kernel,
        grid=(b // bb,),
        in_specs=[pl.BlockSpec((bb, tm, D_CONV), lambda i: (i, 0, 0)),
                  pl.BlockSpec((bb, CONV_HALO, D_CONV), lambda i: (i, 0, 0)),
                  _const_spec((CONV_WIDTH, D_CONV)), _const_spec((1, D_CONV)),
                  _const_spec((1, D_CONV)), _const_spec((1, D_CONV)),
                  _const_spec((D_CONV, D_MODEL))],
        out_specs=pl.BlockSpec((bb, tm, D_MODEL), lambda i: (i, 0, 0)),
        out_shape=jax.ShapeDtypeStruct((b, tm, D_MODEL), F32),
        scratch_shapes=[pltpu.VMEM((bb, CONV_HALO + tm, D_CONV), F32)],
        compiler_params=_params("parallel"),
    )(cur, halo, dw_w, dw_b.reshape(1, -1), ln_g.reshape(1, -1), ln_b.reshape(1, -1), w_pw)


MERGE_TILE = 512
MOE_TILE = 1024
MOE_CHUNK = 256


def _merge_kernel(x_ref, cy_ref, oc_ref, os_ref, ow_ref, g_ref, mg_ref, g1_ref, sc2_ref, sh2_ref,
                  exp_ref, wo_ref, lg_ref, lb_ref, wr_ref, br_ref,
                  x1_ref, h2_ref, wt_ref, rank_ref, cnt_ref):
    tm = x_ref.shape[0]
    gh, gl = _split(g_ref[...])
    nsa = jnp.zeros((tm, D_MODEL), F32)
    for n, o_ref in enumerate((oc_ref, os_ref, ow_ref)):
        e = exp_ref[n]
        nsa = nsa + o_ref[...] * (_dot(gh, e) + _dot(gl, e))
    mix = mg_ref[:, :D_MODEL] * cy_ref[...] + mg_ref[:, D_MODEL:] * nsa
    y = _dot(mix.astype(BF16), wo_ref[...])
    x1 = _layer_norm(DN_ALPHA * x_ref[...] + g1_ref[...] * y, lg_ref[...], lb_ref[...])
    x1_ref[...] = x1
    h2 = x1 * (1.0 + sc2_ref[...]) + sh2_ref[...]
    h2_ref[...] = h2.astype(BF16)
    logits = _dot3(wr_ref[...], h2, _dot_nt) + br_ref[...]
    eidx = _iota(logits.shape, 0)
    picked = jnp.zeros(logits.shape, F32)
    wsum = jnp.zeros((1, tm), F32)
    wts = jnp.zeros(logits.shape, F32)
    v0 = None
    for k in range(TOP_K):
        v = jnp.max(logits, axis=0, keepdims=True)
        first = jnp.min(jnp.where(logits == v, eidx, N_EXPERTS), axis=0, keepdims=True)
        hit = eidx == first
        if k == 0:
            v0 = v
        ev = jnp.exp(v - v0)
        wts = wts + jnp.where(hit, ev, 0.0)
        wsum = wsum + ev
        picked = picked + jnp.where(hit, 1.0, 0.0)
        logits = jnp.where(hit, -jnp.inf, logits)
    wt_ref[0] = wts / wsum
    upper = jnp.where(_iota((tm, tm), 0) < _iota((tm, tm), 1), 1.0, 0.0).astype(BF16)
    rank_ref[0] = _dot(picked.astype(BF16), upper)
    cnt = jnp.sum(picked, axis=1, keepdims=True)
    cnt_ref[0] = jnp.broadcast_to(cnt, (N_EXPERTS, LANES)).astype(jnp.int32)


def _merge(x, conv_y, o_cmp, o_slc, o_win, g_nsa, merge_g, mod, gate_expand, w_out, ln_g, ln_b,
           w_router_t, b_router):
    n = x.shape[0]
    tm = MERGE_TILE
    nt = n // tm
    per_row = mod.shape[0] != 1

    def row_spec(w):
        return pl.BlockSpec((tm, w), lambda i: (i, 0))

    def mod_spec(col):
        return pl.BlockSpec((tm if per_row else 1, D_MODEL),
                            (lambda i: (i, col)) if per_row else (lambda i: (0, col)))

    tile_spec = pl.BlockSpec((1, N_EXPERTS, tm), lambda i: (i, 0, 0))
    return pl.pallas_call(
        _merge_kernel,
        grid=(nt,),
        in_specs=[row_spec(D_MODEL)] * 5 + [row_spec(GATE_PAD), row_spec(2 * D_MODEL),
                  mod_spec(2), mod_spec(4), mod_spec(3),
                  _const_spec(gate_expand.shape), _const_spec(w_out.shape),
                  _const_spec((1, D_MODEL)), _const_spec((1, D_MODEL)),
                  _const_spec(w_router_t.shape), _const_spec((N_EXPERTS, 1))],
        out_specs=[row_spec(D_MODEL), row_spec(D_MODEL), tile_spec, tile_spec,
                   pl.BlockSpec((1, N_EXPERTS, LANES), lambda i: (i, 0, 0))],
        out_shape=[jax.ShapeDtypeStruct((n, D_MODEL), F32), jax.ShapeDtypeStruct((n, D_MODEL), BF16),
                   jax.ShapeDtypeStruct((nt, N_EXPERTS, tm), F32),
                   jax.ShapeDtypeStruct((nt, N_EXPERTS, tm), F32),
                   jax.ShapeDtypeStruct((nt, N_EXPERTS, LANES), jnp.int32)],
        compiler_params=_params("parallel"),
    )(x, conv_y, o_cmp, o_slc, o_win, g_nsa, merge_g, mod, mod, mod, gate_expand, w_out,
      ln_g.reshape(1, -1), ln_b.reshape(1, -1), w_router_t, b_router.reshape(-1, 1))


def _moe_kernel(cnt_ref, x1_ref, h2_ref, wt_ref, rank_ref, g2_ref, wup_ref, bup_ref, wdn_ref, bdn_ref,
                lg_ref, lb_ref, o_ref, acc_ref):
    i = pl.program_id(0)
    e = pl.program_id(1)
    tm = x1_ref.shape[0]

    @pl.when(e == 0)
    def _():
        acc_ref[...] = jnp.zeros_like(acc_ref)

    cnt = cnt_ref[i * N_EXPERTS + e]
    for ch in range(tm // MOE_CHUNK):
        @pl.when(cnt > ch * MOE_CHUNK)
        def _():
            w_e = wt_ref[0, pl.ds(e, 1), :]
            slot = rank_ref[0, pl.ds(e, 1), :] - float(ch * MOE_CHUNK)
            rows = _iota((MOE_CHUNK, tm), 0).astype(F32)
            onehot = jnp.where((rows == slot) & (w_e > 0.0), 1.0, 0.0)
            wc = jnp.sum(onehot * w_e, axis=1, keepdims=True)
            sel = onehot.astype(BF16)
            xc = _dot(sel, h2_ref[...]).astype(BF16)
            u = _dot(xc, wup_ref[0]) + bup_ref[0]
            x_glu = jnp.minimum(u[:, :D_EXPERT], SWIGLU_LIMIT)
            x_lin = jnp.clip(u[:, D_EXPERT:], -SWIGLU_LIMIT, SWIGLU_LIMIT)
            a = x_glu * _sigmoid(SWIGLU_ALPHA * x_glu) * (x_lin + 1.0)
            y = (_dot(a.astype(BF16), wdn_ref[0]) + bdn_ref[0]) * wc
            acc_ref[...] += _dot_tn(sel, y.astype(BF16))

    @pl.when(e == N_EXPERTS - 1)
    def _():
        o_ref[...] = _layer_norm(DN_ALPHA * x1_ref[...] + g2_ref[...] * acc_ref[...],
                                 lg_ref[...], lb_ref[...])


def _moe(x1, h2, wt, rank, cnt, mod, w_up, b_up, w_down, b_down, ln_g, ln_b):
    n = x1.shape[0]
    tm = MOE_TILE
    nt = n // tm
    per_row = mod.shape[0] != 1
    g2_spec = pl.BlockSpec((tm if per_row else 1, D_MODEL),
                           (lambda i, e, c: (i, 5)) if per_row else (lambda i, e, c: (0, 5)))
    tile_spec = pl.BlockSpec((1, N_EXPERTS, tm), lambda i, e, c: (i, 0, 0))
    grid_spec = pltpu.PrefetchScalarGridSpec(
        num_scalar_prefetch=1,
        grid=(nt, N_EXPERTS),
        in_specs=[pl.BlockSpec((tm, D_MODEL), lambda i, e, c: (i, 0)),
                  pl.BlockSpec((tm, D_MODEL), lambda i, e, c: (i, 0)),
                  tile_spec, tile_spec, g2_spec,
                  pl.BlockSpec((1, D_MODEL, 2 * D_EXPERT), lambda i, e, c: (e, 0, 0)),
                  pl.BlockSpec((1, 1, 2 * D_EXPERT), lambda i, e, c: (e, 0, 0)),
                  pl.BlockSpec((1, D_EXPERT, D_MODEL), lambda i, e, c: (e, 0, 0)),
                  pl.BlockSpec((1, 1, D_MODEL), lambda i, e, c: (e, 0, 0)),
                  pl.BlockSpec((1, D_MODEL), lambda i, e, c: (0, 0)),
                  pl.BlockSpec((1, D_MODEL), lambda i, e, c: (0, 0))],
        out_specs=pl.BlockSpec((tm, D_MODEL), lambda i, e, c: (i, 0)),
        scratch_shapes=[pltpu.VMEM((tm, D_MODEL), F32)],
    )
    return pl.pallas_call(
        _moe_kernel,
        grid_spec=grid_spec,
        out_shape=jax.ShapeDtypeStruct((n, D_MODEL), F32),
        compiler_params=_params("parallel", "arbitrary"),
    )(cnt, x1, h2, wt, rank, mod, w_up, b_up.reshape(N_EXPERTS, 1, -1), w_down,
      b_down.reshape(N_EXPERTS, 1, -1), ln_g.reshape(1, -1), ln_b.reshape(1, -1))


def _softmax_rows(s, valid):
    s = jnp.where(valid, s, NEG_INF)
    m = jnp.max(s, axis=-1, keepdims=True)
    e = jnp.exp(s - m)
    l = jnp.sum(e, axis=-1, keepdims=True)
    return e * jnp.where(m > 0.5 * NEG_INF, 1.0 / l, 0.0)


def _flash_update(s, valid, v, m_old, l_old, acc_old, pv=_dot):
    s = jnp.where(valid, s, NEG_INF)
    m_new = jnp.maximum(m_old, jnp.max(s, axis=-1, keepdims=True))
    p = jnp.where(valid, jnp.exp(s - m_new), 0.0)
    alpha = jnp.exp(m_old - m_new)
    l_new = alpha * l_old + jnp.sum(p, axis=-1, keepdims=True)
    acc_new = alpha * acc_old + pv(p.astype(BF16), v)
    return m_new, l_new, acc_new


def _shift_right_one(x):
    return jnp.where(_iota(x.shape, 1) == 0, 0.0, pltpu.roll(x, 1, 1))


def _block_scores(imp, nj):
    parts = [imp[:, k * nj:(k + 1) * nj] for k in range(SEL_RATIO)]
    return (parts[0] + parts[1]) + (parts[2] + parts[3]) + _shift_right_one(parts[3])


def _select_blocks(score, qblk, n_blocks):
    j = _iota(score.shape, 1)
    back = qblk - j
    forced = (j == 0) | ((back >= 0) & (back < N_LOCAL))
    score = jnp.where(forced, FORCE_SCORE, jnp.where(back >= 0, score, -1.0))
    score = jnp.where(j < n_blocks, score, -jnp.inf)
    picked = jnp.zeros(score.shape, F32)
    lane = j.astype(F32)
    for _ in range(N_SEL):
        m = jnp.max(score, axis=-1, keepdims=True)
        first = jnp.min(jnp.where(score == m, lane, float(SEL_LANES)), axis=-1, keepdims=True)
        hit = lane == first
        picked = jnp.where(hit, 1.0, picked)
        score = jnp.where(hit, -jnp.inf, score)
    return picked


def _gelu_tanh(x):
    return x * (0.5 * (1.0 + jnp.tanh(0.7978845608028654 * (x + 0.044715 * (x * x * x)))))


def _compress_rows(buf_ref, pe_ref, w1_ref, w2_ref, nj):
    outs = []
    for pair in range(HEAD_PAIRS):
        acc = jnp.zeros((SEL_RATIO * nj, 2 * PHI_HIDDEN), F32)
        for s in range(CMP_LEN):
            parts = [buf_ref[pair, pl.ds(CMP_STRIDE * k + s, nj, stride=SEL_LEN), :] for k in range(SEL_RATIO)]
            x = (jnp.concatenate(parts, axis=0) + pe_ref[s:s + 1, :]).astype(BF16)
            acc = acc + _dot(x, w1_ref[s])
        outs.append(_dot(_gelu_tanh(acc).astype(BF16), w2_ref[...]))
    return jnp.concatenate(outs, axis=1)


CMP_TILE = 4096


def _cmp_prompt_kernel(x_ref, halo_ref, pe_ref, w1_ref, w2_ref, o_ref, buf_ref):
    t = pl.program_id(0)
    last = t == pl.num_programs(0) - 1
    for pair in range(HEAD_PAIRS):
        cols = slice(pair * LANES, (pair + 1) * LANES)
        buf_ref[pair, :CMP_TILE, :] = x_ref[:, cols]
        buf_ref[pair, CMP_TILE:, :] = jnp.where(last, 0.0, halo_ref[:, cols])
    nj = CMP_TILE // SEL_LEN
    out = _compress_rows(buf_ref, pe_ref, w1_ref, w2_ref, nj)
    o_ref[...] = out.reshape(SEL_RATIO, nj, KV_WIDTH).astype(BF16)


def _compress_prompt(kv4, kv_set, pe, w1, w2):
    t_len = kv4.shape[0]
    nt = t_len // CMP_TILE
    nj = CMP_TILE // SEL_LEN
    halo_blocks = CMP_TILE // SEL_LEN
    last_halo = t_len // SEL_LEN - 1
    out = pl.pallas_call(
        _cmp_prompt_kernel,
        grid=(nt,),
        in_specs=[pl.BlockSpec((CMP_TILE, KV_WIDTH), lambda t: (t, kv_set)),
                  pl.BlockSpec((SEL_LEN, KV_WIDTH),
                               lambda t: (jnp.minimum((t + 1) * halo_blocks, last_halo), kv_set)),
                  _const_spec(pe.shape), _const_spec(w1.shape), _const_spec(w2.shape)],
        out_specs=pl.BlockSpec((SEL_RATIO, nj, KV_WIDTH), lambda t: (0, t, 0)),
        out_shape=jax.ShapeDtypeStruct((SEL_RATIO, t_len // SEL_LEN, KV_WIDTH), BF16),
        scratch_shapes=[pltpu.VMEM((HEAD_PAIRS, CMP_TILE + SEL_LEN, LANES), F32)],
        compiler_params=_params("arbitrary"),
    )(kv4, kv4, pe, w1, w2)
    return out.reshape(t_len // CMP_STRIDE, KV_WIDTH)


def _cmp_attn_prompt_kernel(q_ref, kc_ref, vc_ref, o_ref, sel_ref, any_ref):
    nc = kc_ref.shape[0]
    nj = nc // SEL_RATIO
    s0 = pl.program_id(0) * Q_TILE
    tq = s0 + _iota((Q_TILE, 1), 0)
    col = _iota((1, nc), 1)
    ends = (SEL_RATIO * (col % nj) + col // nj) * CMP_STRIDE + (CMP_LEN - 1)
    rel = (ends - s0).astype(F32)
    valid = ends <= tq
    any_rows = []
    scores = []
    for g in range(N_KV_HEADS):
        kc = kc_ref[:, g * HEAD_DIM:(g + 1) * HEAD_DIM]
        vc = vc_ref[:, g * HEAD_DIM:(g + 1) * HEAD_DIM]
        imp = jnp.zeros((Q_TILE, nc), F32)
        for r in range(GROUP):
            h = g * GROUP + r
            cols = slice(h * HEAD_DIM, (h + 1) * HEAD_DIM)
            p = _softmax_rows(_dot_nt(q_ref[:, cols], kc) + SLOPES[h] * rel, valid)
            o_ref[:, cols] = _dot(p.astype(BF16), vc)
            imp = imp + p
        scores.append(_block_scores(imp, nj))
    picked = _select_blocks(jnp.concatenate(scores, axis=0),
                            jnp.concatenate([tq // SEL_LEN] * N_KV_HEADS, axis=0), nj)
    for g in range(N_KV_HEADS):
        sel_ref[g] = picked[g * Q_TILE:(g + 1) * Q_TILE].astype(BF16)
        any_rows.append(jnp.max(picked[g * Q_TILE:(g + 1) * Q_TILE], axis=0, keepdims=True))
    any_ref[0] = jnp.concatenate(any_rows + [jnp.zeros((ANY_ROWS - N_KV_HEADS, SEL_LANES), F32)], axis=0)


def _cmp_attn_prompt(q, kc, vc):
    t_len = q.shape[0]
    nt = t_len // Q_TILE
    return pl.pallas_call(
        _cmp_attn_prompt_kernel,
        grid=(nt,),
        in_specs=[pl.BlockSpec((Q_TILE, Q_WIDTH), lambda i: (i, 0)),
                  _const_spec(kc.shape), _const_spec(vc.shape)],
        out_specs=[pl.BlockSpec((Q_TILE, Q_WIDTH), lambda i: (i, 0)),
                   pl.BlockSpec((N_KV_HEADS, Q_TILE, SEL_LANES), lambda i: (0, i, 0)),
                   pl.BlockSpec((1, ANY_ROWS, SEL_LANES), lambda i: (i, 0, 0))],
        out_shape=[jax.ShapeDtypeStruct((t_len, Q_WIDTH), F32),
                   jax.ShapeDtypeStruct((N_KV_HEADS, t_len, SEL_LANES), BF16),
                   jax.ShapeDtypeStruct((nt, ANY_ROWS, SEL_LANES), F32)],
        compiler_params=_params("parallel"),
    )(q, kc, vc)


def _tile_flags(any_sel):
    nt = any_sel.shape[0]
    per_tile = KV_TILE // SEL_LEN
    hit = any_sel[:, :N_KV_HEADS].reshape(nt, N_KV_HEADS, SEL_LANES // per_tile, per_tile).max(-1) > 0.0
    bits = hit.reshape(nt, N_KV_HEADS, -1, FLAG_BITS).astype(jnp.uint32) << jnp.arange(FLAG_BITS, dtype=jnp.uint32)
    return lax.bitcast_convert_type(bits.sum(-1, dtype=jnp.uint32), jnp.int32).reshape(-1)


def _slc_attn_prompt_kernel(flag_ref, q_ref, k_ref, v_ref, sel_ref, o_ref, qs_ref, m_ref, l_ref, acc_ref):
    i = pl.program_id(0)
    tq = i * Q_TILE + _iota((Q_TILE, 1), 0)
    for h in range(N_HEADS):
        qs_ref[h // GROUP, (h % GROUP) * Q_TILE:(h % GROUP + 1) * Q_TILE, :] = q_ref[:, h * HEAD_DIM:(h + 1) * HEAD_DIM]
    m_ref[...] = jnp.full(m_ref.shape, NEG_INF, F32)
    l_ref[...] = jnp.zeros(l_ref.shape, F32)
    acc_ref[...] = jnp.zeros(acc_ref.shape, F32)

    def tile(t, carry):
        base = pl.multiple_of(t * KV_TILE, KV_TILE)
        causal = tq - (base + _iota((1, KV_TILE), 1)) >= 0
        rel = (base - i * Q_TILE + _iota((1, KV_TILE), 1)).astype(F32)
        expand = jnp.where(_iota((SEL_LANES, KV_TILE), 0)
                           == t * (KV_TILE // SEL_LEN) + _iota((SEL_LANES, KV_TILE), 1) // SEL_LEN,
                           1.0, 0.0).astype(BF16)
        for g in range(N_KV_HEADS):
            word = flag_ref[(i * N_KV_HEADS + g) * FLAG_WORDS + t // FLAG_BITS]

            @pl.when(((word >> (t % FLAG_BITS)) & 1) == 1)
            def _():
                valid = (_dot(sel_ref[g], expand) > 0.5) & causal
                kt = k_ref[pl.ds(base, KV_TILE), g * HEAD_DIM:(g + 1) * HEAD_DIM]
                vt = v_ref[pl.ds(base, KV_TILE), g * HEAD_DIM:(g + 1) * HEAD_DIM]
                raw = _dot_nt(qs_ref[g], kt)
                s = jnp.concatenate(
                    [jnp.where(valid, raw[r * Q_TILE:(r + 1) * Q_TILE] + SLOPES[g * GROUP + r] * rel, NEG_INF)
                     for r in range(GROUP)], axis=0)
                m_old = m_ref[g]
                m_new = jnp.maximum(m_old, jnp.max(s, axis=-1, keepdims=True))
                p = jnp.exp(s - jnp.concatenate([m_new] * (KV_TILE // LANES), axis=1))
                alpha = jnp.exp(m_old - m_new)
                l_ref[g] = alpha * l_ref[g] + jnp.sum(p, axis=-1, keepdims=True)
                acc_ref[g] = alpha[:, :HEAD_DIM] * acc_ref[g] + _dot(p.astype(BF16), vt)
                m_ref[g] = m_new
        return carry

    lax.fori_loop(0, (i * Q_TILE) // KV_TILE + 1, tile, 0)
    for h in range(N_HEADS):
        rows = slice((h % GROUP) * Q_TILE, (h % GROUP + 1) * Q_TILE)
        l = l_ref[h // GROUP, rows, :HEAD_DIM]
        o_ref[:, h * HEAD_DIM:(h + 1) * HEAD_DIM] = acc_ref[h // GROUP, rows, :] / jnp.where(l > 0.0, l, 1.0)


def _slc_attn_prompt(q, kvb, sel, flags):
    t_len = q.shape[0]
    resident = functools.partial(pl.BlockSpec, pipeline_mode=pl.Buffered(1))
    rows = GROUP * Q_TILE
    grid_spec = pltpu.PrefetchScalarGridSpec(
        num_scalar_prefetch=1,
        grid=(t_len // Q_TILE,),
        in_specs=[pl.BlockSpec((Q_TILE, Q_WIDTH), lambda i, f: (i, 0)),
                  resident((t_len, KV_WIDTH), lambda i, f: (0, 2)),
                  resident((t_len, KV_WIDTH), lambda i, f: (0, 3)),
                  pl.BlockSpec((N_KV_HEADS, Q_TILE, SEL_LANES), lambda i, f: (0, i, 0))],
        out_specs=pl.BlockSpec((Q_TILE, Q_WIDTH), lambda i, f: (i, 0)),
        scratch_shapes=[pltpu.VMEM((N_KV_HEADS, rows, HEAD_DIM), BF16),
                        pltpu.VMEM((N_KV_HEADS, rows, LANES), F32), pltpu.VMEM((N_KV_HEADS, rows, LANES), F32),
                        pltpu.VMEM((N_KV_HEADS, rows, HEAD_DIM), F32)],
    )
    return pl.pallas_call(
        _slc_attn_prompt_kernel,
        grid_spec=grid_spec,
        out_shape=jax.ShapeDtypeStruct((t_len, Q_WIDTH), F32),
        compiler_params=_params("parallel"),
    )(flags, q, kvb, kvb, sel)


def _win_attn_prompt_kernel(q_ref, k_ref, v_ref, o_ref):
    s0 = pl.multiple_of(pl.program_id(0) * Q_TILE, Q_TILE)
    n_keys = WINDOW + Q_TILE
    tq = s0 + _iota((Q_TILE, 1), 0)
    pw = s0 - WINDOW + _iota((1, n_keys), 1)
    dw = tq - pw
    valid = (dw >= 0) & (dw < WINDOW) & (pw >= 0)
    rel = (_iota((1, n_keys), 1) - WINDOW).astype(F32)
    for g in range(N_KV_HEADS):
        kw = k_ref[pl.ds(s0, n_keys), g * HEAD_DIM:(g + 1) * HEAD_DIM]
        vw = v_ref[pl.ds(s0, n_keys), g * HEAD_DIM:(g + 1) * HEAD_DIM]
        for r in range(GROUP):
            h = g * GROUP + r
            cols = slice(h * HEAD_DIM, (h + 1) * HEAD_DIM)
            p = _softmax_rows(_dot_nt(q_ref[:, cols], kw) + SLOPES[h] * rel, valid)
            o_ref[:, cols] = _dot(p.astype(BF16), vw)


def _win_attn_prompt(q, kw_pad, vw_pad):
    t_len = q.shape[0]
    resident = functools.partial(pl.BlockSpec, pipeline_mode=pl.Buffered(1))
    return pl.pallas_call(
        _win_attn_prompt_kernel,
        grid=(t_len // Q_TILE,),
        in_specs=[pl.BlockSpec((Q_TILE, Q_WIDTH), lambda i: (i, 0)),
                  resident(kw_pad.shape, lambda i: (0, 0)), resident(vw_pad.shape, lambda i: (0, 0))],
        out_specs=pl.BlockSpec((Q_TILE, Q_WIDTH), lambda i: (i, 0)),
        out_shape=jax.ShapeDtypeStruct((t_len, Q_WIDTH), F32),
        compiler_params=_params("parallel"),
    )(q, kw_pad, vw_pad)


S_ROWS = N_HEADS * DEC_SEQ
S_TAIL = SEL_LEN
S_CMP_BLOCKS = PAST_LEN // CMP_STRIDE
S_CHUNK = 2048


def _fetch_pages(pt_ref, sem_ref, page_copy):
    b = pl.program_id(0)
    slot = b % 2

    def copy(seq, p, dst_slot):
        return page_copy(pt_ref[seq * N_PAGES + p], p, dst_slot, sem_ref.at[dst_slot])

    def start(seq, dst_slot):
        def body(p, c):
            copy(seq, p, dst_slot).start()
            return c
        lax.fori_loop(0, N_PAGES, body, 0)

    @pl.when(b == 0)
    def _():
        start(b, slot)

    @pl.when(b + 1 < pl.num_programs(0))
    def _():
        start(b + 1, 1 - slot)

    def wait(p, c):
        copy(b, p, slot).wait()
        return c
    lax.fori_loop(0, N_PAGES, wait, 0)
    return slot


PAGES_PER_STEP = 4
TAPS_PER_DOT = 8
S_CHUNKS = PAST_LEN // CMP_STRIDE + ANY_ROWS


def _cmp_sample_kernel(kv_set, pt_ref, cache_ref, new_ref, pet_ref, w1_ref, w2_ref, o_ref,
                       stage_ref, buf_ref, sem_ref):
    def page_copy(page, p, slot, sem):
        return pltpu.make_async_copy(cache_ref.at[page, kv_set],
                                     stage_ref.at[slot, pl.ds(p * KV_WIDTH, KV_WIDTH), :], sem)

    slot = _fetch_pages(pt_ref, sem_ref, page_copy)
    per_page = PAGE_SIZE // CMP_STRIDE
    regroup = jnp.where((_iota((PAGE_SIZE, PAGE_SIZE), 1) % CMP_STRIDE) * per_page
                        + _iota((PAGE_SIZE, PAGE_SIZE), 1) // CMP_STRIDE == _iota((PAGE_SIZE, PAGE_SIZE), 0),
                        1.0, 0.0).astype(BF16)

    def to_rows(i, c):
        for u in range(PAGES_PER_STEP):
            p = i * PAGES_PER_STEP + u
            src = pl.multiple_of(p * KV_WIDTH, KV_WIDTH)
            dst = pl.multiple_of(p * per_page, per_page)
            for pair in range(HEAD_PAIRS):
                x_t = stage_ref[slot, pl.ds(src + pair * LANES, LANES), :].astype(BF16)
                rows = _dot_nt(regroup, x_t)
                for s in range(CMP_STRIDE):
                    buf_ref[pair, s, pl.ds(dst, per_page), :] = rows[s * per_page:(s + 1) * per_page, :]
        return c
    lax.fori_loop(0, N_PAGES // PAGES_PER_STEP, to_rows, 0)
    first_row = _iota((ANY_ROWS, LANES), 0) == 0
    for pair in range(HEAD_PAIRS):
        for s in range(CMP_STRIDE):
            tail = jnp.zeros((ANY_ROWS, LANES), F32)
            if s < DEC_SEQ:
                row = new_ref[s:s + 1, pair * LANES:(pair + 1) * LANES].astype(BF16).astype(F32)
                tail = jnp.where(first_row, row, 0.0)
            buf_ref[pair, s, PAST_LEN // CMP_STRIDE:, :] = tail

    bias = jnp.zeros((1, 2 * PHI_HIDDEN), F32)
    for s in range(CMP_LEN):
        bias = bias + jnp.sum(pet_ref[:, s:s + 1] * w1_ref[s].astype(F32), axis=0, keepdims=True)
    nj = S_CMP_BLOCKS // SEL_RATIO
    outs = []
    for pair in range(HEAD_PAIRS):
        acc = jnp.zeros((SEL_RATIO * nj, 2 * PHI_HIDDEN), F32)
        for s0 in range(0, CMP_LEN, TAPS_PER_DOT):
            taps = [jnp.concatenate(
                [buf_ref[pair, s % CMP_STRIDE, pl.ds(k + s // CMP_STRIDE, nj, stride=SEL_RATIO), :]
                 for k in range(SEL_RATIO)], axis=0).astype(BF16) for s in range(s0, s0 + TAPS_PER_DOT)]
            w = w1_ref[s0:s0 + TAPS_PER_DOT].reshape(TAPS_PER_DOT * LANES, 2 * PHI_HIDDEN)
            acc = acc + _dot(jnp.concatenate(taps, axis=1), w)
        outs.append(_dot(_gelu_tanh(acc + bias).astype(BF16), w2_ref[...]))
    o_ref[0] = jnp.concatenate(outs, axis=1).astype(BF16)


def _compress_sample(page_table, cache, kv4_new, kv_set, pe, w1, w2):
    pe_t = pe.T
    grid_spec = pltpu.PrefetchScalarGridSpec(
        num_scalar_prefetch=1,
        grid=(DEC_BATCH,),
        in_specs=[pl.BlockSpec(memory_space=pl.ANY),
                  pl.BlockSpec((DEC_SEQ, KV_WIDTH), lambda b, pt: (b, kv_set)),
                  pl.BlockSpec(pe_t.shape, lambda b, pt: (0, 0)),
                  pl.BlockSpec(w1.shape, lambda b, pt: (0, 0, 0)),
                  pl.BlockSpec(w2.shape, lambda b, pt: (0, 0))],
        out_specs=pl.BlockSpec((1, S_CMP_BLOCKS, KV_WIDTH), lambda b, pt: (b, 0, 0)),
        scratch_shapes=[pltpu.VMEM((2, N_PAGES * KV_WIDTH, PAGE_SIZE), F32),
                        pltpu.VMEM((HEAD_PAIRS, CMP_STRIDE, S_CHUNKS, LANES), F32),
                        pltpu.SemaphoreType.DMA((2,))],
    )
    return pl.pallas_call(
        functools.partial(_cmp_sample_kernel, kv_set),
        grid_spec=grid_spec,
        out_shape=jax.ShapeDtypeStruct((DEC_BATCH, S_CMP_BLOCKS, KV_WIDTH), BF16),
        compiler_params=_params("arbitrary"),
    )(page_table.reshape(-1), cache, kv4_new, pe_t, w1, w2)


def _row_queries():
    return PAST_LEN + _iota((S_ROWS, 1), 0) % DEC_SEQ


def _cmp_attn_sample_kernel(q_ref, slope_ref, kc_ref, vc_ref, o_ref, sel_ref):
    nc = S_CMP_BLOCKS
    nj = nc // SEL_RATIO
    tq = _row_queries()
    col = _iota((1, nc), 1)
    ends = (SEL_RATIO * (col % nj) + col // nj) * CMP_STRIDE + (CMP_LEN - 1)
    s = _dot_nt(q_ref[0], kc_ref[0]) - slope_ref[...] * (tq - ends).astype(F32)
    p = _softmax_rows(s, ends <= tq)
    o_ref[0] = _dot(p.astype(BF16), vc_ref[0])
    rows_g = GROUP * DEC_SEQ
    n_blocks = (PAST_LEN + DEC_SEQ + SEL_LEN - 1) // SEL_LEN
    qblk = (PAST_LEN + _iota((DEC_SEQ, 1), 0)) // SEL_LEN
    scores = []
    for g in range(N_KV_HEADS):
        imp = jnp.zeros((DEC_SEQ, nc), F32)
        for r in range(GROUP):
            imp = imp + p[g * rows_g + r * DEC_SEQ:g * rows_g + (r + 1) * DEC_SEQ]
        scores.append(jnp.concatenate([_block_scores(imp, nj), jnp.zeros((DEC_SEQ, SEL_LANES - nj), F32)], axis=1))
    picked = _select_blocks(jnp.concatenate(scores, axis=0), jnp.concatenate([qblk] * N_KV_HEADS, axis=0), n_blocks)
    sel_ref[0] = jnp.concatenate([picked[g * DEC_SEQ:(g + 1) * DEC_SEQ] for g in range(N_KV_HEADS)
                                  for _ in range(GROUP)], axis=0).astype(BF16)


def _cmp_attn_sample(qbd, slope_col, kc, vc):
    seq_spec = lambda rows, w: pl.BlockSpec((1, rows, w), lambda b: (b, 0, 0))
    return pl.pallas_call(
        _cmp_attn_sample_kernel,
        grid=(DEC_BATCH,),
        in_specs=[seq_spec(S_ROWS, KV_WIDTH), _const_spec((S_ROWS, 1)),
                  seq_spec(S_CMP_BLOCKS, KV_WIDTH), seq_spec(S_CMP_BLOCKS, KV_WIDTH)],
        out_specs=[seq_spec(S_ROWS, KV_WIDTH), seq_spec(S_ROWS, SEL_LANES)],
        out_shape=[jax.ShapeDtypeStruct((DEC_BATCH, S_ROWS, KV_WIDTH), F32),
                   jax.ShapeDtypeStruct((DEC_BATCH, S_ROWS, SEL_LANES), BF16)],
        compiler_params=_params("parallel"),
    )(qbd, slope_col, kc, vc)


def _pad_new_rows(x):
    return jnp.concatenate([x, jnp.zeros((LANES - DEC_SEQ, x.shape[1]), F32)], axis=0).astype(BF16)


def _slc_win_sample_kernel(pt_ref, cache_ref, q_ref, slope_ref, sel_ref, new_ref, cwin_ref, wnew_ref,
                           os_ref, ow_ref, buf_ref, sem_ref):
    def page_copy(page, p, slot, sem):
        return pltpu.make_async_copy(cache_ref.at[page, pl.ds(2, 2)],
                                     buf_ref.at[slot, :, :, pl.ds(p * PAGE_SIZE, PAGE_SIZE)], sem)

    slot = _fetch_pages(pt_ref, sem_ref, page_copy)
    q = q_ref[0]
    slope = slope_ref[...]
    sel = sel_ref[0]
    tq = _row_queries()
    new_pos = PAST_LEN + _iota((1, LANES), 1)
    new_ok = (tq - new_pos >= 0) & (new_pos < PAST_LEN + DEC_SEQ)
    new_dist = (tq - new_pos).astype(F32)

    m = jnp.full((S_ROWS, 1), NEG_INF, F32)
    l = jnp.zeros((S_ROWS, 1), F32)
    acc = jnp.zeros((S_ROWS, KV_WIDTH), F32)
    for c in range(PAST_LEN // S_CHUNK):
        k_t = buf_ref[slot, 0, :, c * S_CHUNK:(c + 1) * S_CHUNK].astype(BF16)
        v_t = buf_ref[slot, 1, :, c * S_CHUNK:(c + 1) * S_CHUNK].astype(BF16)
        pos = c * S_CHUNK + _iota((1, S_CHUNK), 1)
        expand = jnp.where(_iota((SEL_LANES, S_CHUNK), 0)
                           == (c * S_CHUNK + _iota((SEL_LANES, S_CHUNK), 1)) // SEL_LEN, 1.0, 0.0).astype(BF16)
        valid = (_dot(sel, expand) > 0.5) & (tq - pos >= 0)
        s = _dot(q, k_t) - slope * (tq - pos).astype(F32)
        m, l, acc = _flash_update(s, valid, v_t, m, l, acc, pv=_dot_nt)
    new = _pad_new_rows(new_ref[...])
    in_last = jnp.sum(jnp.where(_iota((S_ROWS, SEL_LANES), 1) == PAST_LEN // SEL_LEN, sel.astype(F32), 0.0),
                      axis=1, keepdims=True) > 0.5
    s = _dot_nt(q, new[:, :KV_WIDTH]) - slope * new_dist
    m, l, acc = _flash_update(s, in_last & new_ok, new[:, KV_WIDTH:], m, l, acc)
    os_ref[0] = acc / jnp.where(l > 0.0, l, 1.0)

    wb = cwin_ref.shape[-1]
    pw = PAST_LEN - wb + _iota((1, wb), 1)
    dw = tq - pw
    s_past = _dot(q, cwin_ref[0, 0].astype(BF16)) - slope * dw.astype(F32)
    wnew = _pad_new_rows(wnew_ref[...])
    s_new = _dot_nt(q, wnew[:, :KV_WIDTH]) - slope * new_dist
    valid = jnp.concatenate([(dw >= 0) & (dw < WINDOW) & (pw >= 0), new_ok & (tq - new_pos < WINDOW)], axis=1)
    p = _softmax_rows(jnp.concatenate([s_past, s_new], axis=1), valid).astype(BF16)
    ow_ref[0] = _dot_nt(p[:, :wb], cwin_ref[0, 1].astype(BF16)) + _dot(p[:, wb:], wnew[:, KV_WIDTH:])


def _slc_win_sample(page_table, cache, qbd, slope_col, sel, kv4_new, cache_win, win_new):
    wb = cache_win.shape[-1]
    seq_spec = lambda rows, w: pl.BlockSpec((1, rows, w), lambda b, pt: (b, 0, 0))
    grid_spec = pltpu.PrefetchScalarGridSpec(
        num_scalar_prefetch=1,
        grid=(DEC_BATCH,),
        in_specs=[pl.BlockSpec(memory_space=pl.ANY),
                  seq_spec(S_ROWS, KV_WIDTH),
                  pl.BlockSpec((S_ROWS, 1), lambda b, pt: (0, 0)),
                  seq_spec(S_ROWS, SEL_LANES),
                  pl.BlockSpec((DEC_SEQ, 2 * KV_WIDTH), lambda b, pt: (b, 1)),
                  pl.BlockSpec((1, 2, KV_WIDTH, wb), lambda b, pt: (b, 0, 0, 0)),
                  pl.BlockSpec((DEC_SEQ, 2 * KV_WIDTH), lambda b, pt: (b, 0))],
        out_specs=[seq_spec(S_ROWS, KV_WIDTH), seq_spec(S_ROWS, KV_WIDTH)],
        scratch_shapes=[pltpu.VMEM((2, 2, KV_WIDTH, PAST_LEN), F32), pltpu.SemaphoreType.DMA((2,))],
    )
    return pl.pallas_call(
        _slc_win_sample_kernel,
        grid_spec=grid_spec,
        out_shape=[jax.ShapeDtypeStruct((DEC_BATCH, S_ROWS, KV_WIDTH), F32)] * 2,
        compiler_params=_params("arbitrary"),
    )(page_table.reshape(-1), cache, qbd, slope_col, sel, kv4_new, cache_win, win_new)


def _gate_expand():
    rows = jnp.arange(GATE_PAD)[:, None]
    cols = jnp.arange(D_MODEL)[None, :]
    return jnp.stack([(rows == (cols // HEAD_DIM) * N_NSA_BRANCH + n) for n in range(N_NSA_BRANCH)]
                     ).astype(BF16)


def _compress_weights(pe, w1, w2):
    per_tile = LANES // HEAD_DIM
    eye = jnp.eye(per_tile, dtype=F32)
    w1_big = jnp.einsum('sde,gh->sgdhe', w1, eye).reshape(CMP_LEN, LANES, per_tile * PHI_HIDDEN).astype(BF16)
    w2_big = jnp.einsum('ed,gh->gehd', w2, eye).reshape(per_tile * PHI_HIDDEN, LANES).astype(BF16)
    return jnp.tile(pe, (1, per_tile)), w1_big, w2_big


def _rows_from_heads(o):
    o = o.reshape(DEC_BATCH, N_KV_HEADS, GROUP, DEC_SEQ, N_KV_HEADS, HEAD_DIM)
    idx = jnp.arange(N_KV_HEADS)
    own = o[:, idx, :, :, idx]
    return own.transpose(1, 3, 0, 2, 4).reshape(DEC_BATCH * DEC_SEQ, Q_WIDTH)


def kernel(x_prompt, x_sample, c_prompt, c_sample, cache_kv, cache_win, state_conv, page_table, w_ada, b_ada, w_in, conv_dw_w, conv_dw_b, conv_ln_g, conv_ln_b, w_conv_out, cmp_pe_k, cmp_w1_k, cmp_w2_k, cmp_pe_v, cmp_w1_v, cmp_w2_v, w_out, ln1_g, ln1_b, w_router, b_router, w_up, b_up, w_down, b_down, ln2_g, ln2_b):
    l = 0
    n_p = SEQ
    n_s = DEC_BATCH * DEC_SEQ

    c_all = jnp.concatenate([c_prompt, c_sample, jnp.zeros((7, D_MODEL), F32)], axis=0)
    mod = _adaln(c_all, w_ada[l], b_ada[l])
    mod_p = mod[0:1]
    mod_s = jnp.repeat(mod[1:1 + DEC_BATCH], DEC_SEQ, axis=0)

    wi = w_in[l]
    wts = ((wi[:, :OFF_KV] * (HEAD_DIM ** -0.5)).astype(BF16),
           wi[:, OFF_KV:OFF_GLU].astype(BF16),
           wi[:, OFF_GLU:OFF_NSA_G].astype(BF16),
           jnp.pad(wi[:, OFF_NSA_G:OFF_MERGE], ((0, 0), (0, GATE_PAD - N_HEADS * N_NSA_BRANCH))).astype(BF16),
           wi[:, OFF_MERGE:].astype(BF16))
    gate_expand = _gate_expand()
    w_out_b = w_out[l].astype(BF16)
    w_pw_b = w_conv_out[l].astype(BF16)
    w_up_b = w_up[l].astype(BF16)
    w_down_b = w_down[l].astype(BF16)
    w_router_t = w_router[l].T
    cmp_k = _compress_weights(cmp_pe_k[l], cmp_w1_k[l], cmp_w2_k[l])
    cmp_v = _compress_weights(cmp_pe_v[l], cmp_w1_v[l], cmp_w2_v[l])
    conv_w = (conv_dw_w[l], conv_dw_b[l], conv_ln_g[l], conv_ln_b[l], w_pw_b)

    def tail(x, conv_y, o3, g_nsa, merge_g, m):
        x1, h2, wt, rank, cnt = _merge(x, conv_y, *o3, g_nsa, merge_g, m, gate_expand, w_out_b,
                                       ln1_g[l], ln1_b[l], w_router_t, b_router[l])
        per = MOE_TILE // MERGE_TILE
        cnt = cnt[:, :, 0].reshape(-1, per, N_EXPERTS)
        before = jnp.cumsum(cnt, axis=1) - cnt
        rank = rank.reshape(-1, per, N_EXPERTS, MERGE_TILE) + before[..., None].astype(F32)
        rank = rank.transpose(0, 2, 1, 3).reshape(-1, N_EXPERTS, MOE_TILE)
        wt = wt.reshape(-1, per, N_EXPERTS, MERGE_TILE).transpose(0, 2, 1, 3).reshape(-1, N_EXPERTS, MOE_TILE)
        return _moe(x1, h2, wt, rank, cnt.sum(1).reshape(-1), m, w_up_b, b_up[l], w_down_b, b_down[l],
                    ln2_g[l], ln2_b[l])

    xp = x_prompt.reshape(n_p, D_MODEL)
    q, kv4, win2, kvb, u, g_nsa, merge_g = _project(xp, mod_p, mod_p, wts, 256)
    ct = 512
    u3 = u.reshape(n_p // ct, ct, D_CONV)
    halo = jnp.concatenate([jnp.zeros((1, CONV_HALO, D_CONV), F32), u3[:-1, ct - CONV_HALO:]], axis=0)
    conv_y = _conv_branch(u3, halo, *conv_w, 1).reshape(n_p, D_MODEL)
    kc = _compress_prompt(kv4, 0, *cmp_k)
    vc = _compress_prompt(kv4, 1, *cmp_v)
    o_cmp, sel, any_sel = _cmp_attn_prompt(q, kc, vc)
    o_slc = _slc_attn_prompt(q, kvb, sel, _tile_flags(any_sel))
    kw_pad = jnp.pad(kvb[:, 4 * KV_WIDTH:5 * KV_WIDTH], ((WINDOW, 0), (0, 0)))
    vw_pad = jnp.pad(kvb[:, 5 * KV_WIDTH:], ((WINDOW, 0), (0, 0)))
    o_win = _win_attn_prompt(q, kw_pad, vw_pad)
    y_p = tail(xp, conv_y, (o_cmp, o_slc, o_win), g_nsa, merge_g, mod_p)
    out_kv_p = kv4.reshape(1, 1, n_p, 4, N_KV_HEADS, HEAD_DIM)
    out_win_p = win2[n_p - WINDOW:].reshape(1, 1, WINDOW, 2, N_KV_HEADS, HEAD_DIM)
    out_conv_p = u[n_p - (CONV_WIDTH - 1):].reshape(1, 1, CONV_WIDTH - 1, D_CONV)

    xs = x_sample.reshape(n_s, D_MODEL)
    q, kv4, win2, kvb, u, g_nsa, merge_g = _project(xs, mod_s, mod_s, wts, 256)
    u3 = u.reshape(DEC_BATCH, DEC_SEQ, D_CONV)
    st = state_conv[l]
    halo = jnp.concatenate([jnp.zeros((DEC_BATCH, CONV_HALO - (CONV_WIDTH - 1), D_CONV), F32), st], axis=1)
    conv_y = _conv_branch(u3, halo, *conv_w, DEC_BATCH).reshape(n_s, D_MODEL)
    qh = q.reshape(DEC_BATCH, DEC_SEQ, N_KV_HEADS, GROUP, HEAD_DIM).transpose(0, 2, 3, 1, 4)
    qbd = jnp.einsum('bgrqd,gh->bgrqhd', qh, jnp.eye(N_KV_HEADS, dtype=BF16)).reshape(DEC_BATCH, S_ROWS, KV_WIDTH)
    slope_col = jnp.repeat(jnp.asarray(SLOPES, F32), DEC_SEQ).reshape(S_ROWS, 1)
    cache = cache_kv[l].transpose(0, 2, 3, 4, 1).reshape(-1, 4, KV_WIDTH, PAGE_SIZE)
    cwin = cache_win[l].transpose(0, 2, 3, 4, 1).reshape(DEC_BATCH, 2, KV_WIDTH, -1)
    kc = _compress_sample(page_table, cache, kv4, 0, *cmp_k)
    vc = _compress_sample(page_table, cache, kv4, 1, *cmp_v)
    o_cmp, sel = _cmp_attn_sample(qbd, slope_col, kc, vc)
    o_slc, o_win = _slc_win_sample(page_table, cache, qbd, slope_col, sel, kv4, cwin, win2)
    o3 = tuple(_rows_from_heads(o) for o in (o_cmp, o_slc, o_win))
    y_s = tail(xs, conv_y, o3, g_nsa, merge_g, mod_s)
    out_kv_s = kv4.reshape(1, DEC_BATCH, DEC_SEQ, 4, N_KV_HEADS, HEAD_DIM)
    win_new = win2.reshape(DEC_BATCH, DEC_SEQ, 2, N_KV_HEADS, HEAD_DIM)
    out_win_s = jnp.concatenate([cache_win[l], win_new], axis=1)[:, DEC_SEQ:][None]
    out_conv_s = jnp.concatenate([st, u3], axis=1)[:, DEC_SEQ:][None]

    return (y_p.reshape(1, n_p, D_MODEL), y_s.reshape(DEC_BATCH, DEC_SEQ, D_MODEL),
            out_kv_p, out_kv_s, out_win_p, out_win_s, out_conv_p, out_conv_s)
```

```python
import functools

import jax
import jax.numpy as jnp
from jax import lax
from jax.experimental import pallas as pl
from jax.experimental.pallas import tpu as pltpu

D_MODEL = 1024
SEQ = 16384
DEC_BATCH = 128
DEC_SEQ = 8
PAST_LEN = 8192
PAGE_SIZE = 128
N_PAGES = PAST_LEN // PAGE_SIZE
N_HEADS = 16
HEAD_DIM = 64
N_KV_HEADS = 4
GROUP = N_HEADS // N_KV_HEADS
KV_WIDTH = N_KV_HEADS * HEAD_DIM
Q_WIDTH = N_HEADS * HEAD_DIM
N_KV_SETS = 6
N_NSA_BRANCH = 3
CMP_LEN = 32
CMP_STRIDE = 16
PHI_HIDDEN = 2 * HEAD_DIM
SEL_LEN = 64
SEL_RATIO = SEL_LEN // CMP_STRIDE
N_SEL = 16
N_LOCAL = 2
WINDOW = 512
D_CONV = D_MODEL // 2
CONV_WIDTH = 31
N_EXPERTS = 32
TOP_K = 4
D_EXPERT = D_MODEL
SWIGLU_LIMIT = 7.0
SWIGLU_ALPHA = 1.702
DN_ALPHA = 2.0 ** 0.25
LN_EPS = 1e-5
NEG_INF = -1e30
FORCE_SCORE = 1e9
OFF_KV = Q_WIDTH
OFF_GLU = OFF_KV + N_KV_SETS * KV_WIDTH
OFF_NSA_G = OFF_GLU + 2 * D_CONV
OFF_MERGE = OFF_NSA_G + N_HEADS * N_NSA_BRANCH

LANES = 128
GATE_PAD = LANES
CONV_HALO = 32
VMEM_LIMIT = 56 * 1024 * 1024
Q_TILE = 128
KV_TILE = 512
SEL_LANES = 256
HEAD_PAIRS = KV_WIDTH // LANES
ANY_ROWS = 8
FLAG_BITS = 32
FLAG_WORDS = SEQ // KV_TILE // FLAG_BITS
SLOPES = tuple(2.0 ** (-8.0 * (h + 1) / N_HEADS) for h in range(N_HEADS))

F32 = jnp.float32
BF16 = jnp.bfloat16


def _params(*sem):
    return pltpu.CompilerParams(dimension_semantics=sem, vmem_limit_bytes=VMEM_LIMIT)


def _dot(a, b):
    return jnp.dot(a, b, preferred_element_type=F32)


def _dot_nt(a, b):
    return lax.dot_general(a, b, (((1,), (1,)), ((), ())), preferred_element_type=F32)


def _dot_tn(a, b):
    return lax.dot_general(a, b, (((0,), (0,)), ((), ())), preferred_element_type=F32)


def _split(x):
    hi = x.astype(BF16)
    lo = (x - hi.astype(F32)).astype(BF16)
    return hi, lo


def _dot3(a, b, dot=_dot):
    ah, al = _split(a)
    bh, bl = _split(b)
    return dot(ah, bh) + (dot(ah, bl) + dot(al, bh))


def _sigmoid(x):
    return 1.0 / (1.0 + jnp.exp(-x))


def _layer_norm(x, g, b):
    mu = jnp.mean(x, axis=-1, keepdims=True)
    xc = x - mu
    var = jnp.mean(xc * xc, axis=-1, keepdims=True)
    return xc * lax.rsqrt(var + LN_EPS) * g + b


def _const_spec(shape):
    return pl.BlockSpec(shape, lambda *_: (0,) * len(shape))


def _iota(shape, axis):
    return lax.broadcasted_iota(jnp.int32, shape, axis)


def _ada_kernel(c_ref, w_ref, b_ref, o_ref):
    o_ref[...] = _dot3(c_ref[...], w_ref[...]) + b_ref[...]


def _adaln(c_all, w_ada, b_ada):
    n = c_all.shape[0]
    tn = 1536
    return pl.pallas_call(
        _ada_kernel,
        grid=(6 * D_MODEL // tn,),
        in_specs=[pl.BlockSpec((n, D_MODEL), lambda j: (0, 0)),
                  pl.BlockSpec((D_MODEL, tn), lambda j: (0, j)),
                  pl.BlockSpec((1, tn), lambda j: (0, j))],
        out_specs=pl.BlockSpec((n, tn), lambda j: (0, j)),
        out_shape=jax.ShapeDtypeStruct((n, 6 * D_MODEL), F32),
        compiler_params=_params("parallel"),
    )(c_all, w_ada, b_ada.reshape(1, -1))


def _proj_kernel(x_ref, sc_ref, sh_ref, wq_ref, wkv_ref, wglu_ref, wg_ref, wm_ref,
                 q_ref, kv_ref, win_ref, kvb_ref, u_ref, g_ref, m_ref):
    h = (x_ref[...] * (1.0 + sc_ref[...]) + sh_ref[...]).astype(BF16)
    q_ref[...] = _dot(h, wq_ref[...]).astype(BF16)
    kv = _dot(h, wkv_ref[...])
    kv_ref[...] = kv[:, :4 * KV_WIDTH]
    win_ref[...] = kv[:, 4 * KV_WIDTH:]
    kvb_ref[...] = kv.astype(BF16)
    glu = _dot(h, wglu_ref[...])
    u_ref[...] = glu[:, :D_CONV] * _sigmoid(glu[:, D_CONV:])
    g_ref[...] = _sigmoid(_dot(h, wg_ref[...]))
    m_ref[...] = _sigmoid(_dot(h, wm_ref[...]))


def _project(x, sc, sh, wts, tm):
    n = x.shape[0]
    per_row = sc.shape[0] != 1
    mod_rows = tm if per_row else 1

    def mod_spec(col):
        return pl.BlockSpec((mod_rows, D_MODEL), (lambda i: (i, col)) if per_row else (lambda i: (0, col)))

    widths = (Q_WIDTH, 4 * KV_WIDTH, 2 * KV_WIDTH, N_KV_SETS * KV_WIDTH, D_CONV, GATE_PAD, 2 * D_MODEL)
    dtypes = (BF16, F32, F32, BF16, F32, F32, F32)
    return pl.pallas_call(
        _proj_kernel,
        grid=(n // tm,),
        in_specs=[pl.BlockSpec((tm, D_MODEL), lambda i: (i, 0)), mod_spec(1), mod_spec(0)]
                 + [_const_spec(w.shape) for w in wts],
        out_specs=[pl.BlockSpec((tm, w), lambda i: (i, 0)) for w in widths],
        out_shape=[jax.ShapeDtypeStruct((n, w), d) for w, d in zip(widths, dtypes)],
        compiler_params=_params("parallel"),
    )(x, sc, sh, *wts)


def _conv_kernel(cur_ref, halo_ref, dw_ref, dwb_ref, lg_ref, lb_ref, wpw_ref, o_ref, ext_ref):
    bb, tm, _ = cur_ref.shape
    ext_ref[:, :CONV_HALO, :] = halo_ref[...]
    ext_ref[:, CONV_HALO:, :] = cur_ref[...]
    first = CONV_HALO - (CONV_WIDTH - 1)
    acc = jnp.zeros((bb, tm, D_CONV), F32)
    for j in range(CONV_WIDTH):
        acc = acc + ext_ref[:, first + j:first + j + tm, :] * dw_ref[j:j + 1, :]
    y = _layer_norm(acc + dwb_ref[...], lg_ref[...], lb_ref[...])
    y = (y * _sigmoid(y)).reshape(bb * tm, D_CONV).astype(BF16)
    o_ref[...] = _dot(y, wpw_ref[...]).reshape(bb, tm, D_MODEL)


def _conv_branch(cur, halo, dw_w, dw_b, ln_g, ln_b, w_pw, bb):
    b, tm, _ = cur.shape
    return pl.pallas_call(
        _conv_kernel,
        grid=(b // bb,),
        in_specs=[pl.BlockSpec((bb, tm, D_CONV), lambda i: (i, 0, 0)),
                  pl.BlockSpec((bb, CONV_HALO, D_CONV), lambda i: (i, 0, 0)),
                  _const_spec((CONV_WIDTH, D_CONV)), _const_spec((1, D_CONV)),
                  _const_spec((1, D_CONV)), _const_spec((1, D_CONV)),
                  _const_spec((D_CONV, D_MODEL))],
        out_specs=pl.BlockSpec((bb, tm, D_MODEL), lambda i: (i, 0, 0)),
        out_shape=jax.ShapeDtypeStruct((b, tm, D_MODEL), F32),
        scratch_shapes=[pltpu.VMEM((bb, CONV_HALO + tm, D_CONV), F32)],
        compiler_params=_params("parallel"),
    )(cur, halo, dw_w, dw_b.reshape(1, -1), ln_g.reshape(1, -1), ln_b.reshape(1, -1), w_pw)


MERGE_TILE = 512
MOE_TILE = 1024
MOE_CHUNK = 256


def _merge_kernel(x_ref, cy_ref, oc_ref, os_ref, ow_ref, g_ref, mg_ref, g1_ref, sc2_ref, sh2_ref,
                  exp_ref, wo_ref, lg_ref, lb_ref, wr_ref, br_ref,
                  x1_ref, h2_ref, wt_ref, rank_ref, cnt_ref):
    tm = x_ref.shape[0]
    gh, gl = _split(g_ref[...])
    nsa = jnp.zeros((tm, D_MODEL), F32)
    for n, o_ref in enumerate((oc_ref, os_ref, ow_ref)):
        e = exp_ref[n]
        nsa = nsa + o_ref[...] * (_dot(gh, e) + _dot(gl, e))
    mix = mg_ref[:, :D_MODEL] * cy_ref[...] + mg_ref[:, D_MODEL:] * nsa
    y = _dot(mix.astype(BF16), wo_ref[...])
    x1 = _layer_norm(DN_ALPHA * x_ref[...] + g1_ref[...] * y, lg_ref[...], lb_ref[...])
    x1_ref[...] = x1
    h2 = x1 * (1.0 + sc2_ref[...]) + sh2_ref[...]
    h2_ref[...] = h2.astype(BF16)
    logits = _dot3(wr_ref[...], h2, _dot_nt) + br_ref[...]
    eidx = _iota(logits.shape, 0)
    picked = jnp.zeros(logits.shape, F32)
    wsum = jnp.zeros((1, tm), F32)
    wts = jnp.zeros(logits.shape, F32)
    v0 = None
    for k in range(TOP_K):
        v = jnp.max(logits, axis=0, keepdims=True)
        first = jnp.min(jnp.where(logits == v, eidx, N_EXPERTS), axis=0, keepdims=True)
        hit = eidx == first
        if k == 0:
            v0 = v
        ev = jnp.exp(v - v0)
        wts = wts + jnp.where(hit, ev, 0.0)
        wsum = wsum + ev
        picked = picked + jnp.where(hit, 1.0, 0.0)
        logits = jnp.where(hit, -jnp.inf, logits)
    wt_ref[0] = wts / wsum
    upper = jnp.where(_iota((tm, tm), 0) < _iota((tm, tm), 1), 1.0, 0.0).astype(BF16)
    rank_ref[0] = _dot(picked.astype(BF16), upper)
    cnt = jnp.sum(picked, axis=1, keepdims=True)
    cnt_ref[0] = jnp.broadcast_to(cnt, (N_EXPERTS, LANES)).astype(jnp.int32)


def _merge(x, conv_y, o_cmp, o_slc, o_win, g_nsa, merge_g, mod, gate_expand, w_out, ln_g, ln_b,
           w_router_t, b_router):
    n = x.shape[0]
    tm = MERGE_TILE
    nt = n // tm
    per_row = mod.shape[0] != 1

    def row_spec(w):
        return pl.BlockSpec((tm, w), lambda i: (i, 0))

    def mod_spec(col):
        return pl.BlockSpec((tm if per_row else 1, D_MODEL),
                            (lambda i: (i, col)) if per_row else (lambda i: (0, col)))

    tile_spec = pl.BlockSpec((1, N_EXPERTS, tm), lambda i: (i, 0, 0))
    return pl.pallas_call(
        _merge_kernel,
        grid=(nt,),
        in_specs=[row_spec(D_MODEL)] * 5 + [row_spec(GATE_PAD), row_spec(2 * D_MODEL),
                  mod_spec(2), mod_spec(4), mod_spec(3),
                  _const_spec(gate_expand.shape), _const_spec(w_out.shape),
                  _const_spec((1, D_MODEL)), _const_spec((1, D_MODEL)),
                  _const_spec(w_router_t.shape), _const_spec((N_EXPERTS, 1))],
        out_specs=[row_spec(D_MODEL), row_spec(D_MODEL), tile_spec, tile_spec,
                   pl.BlockSpec((1, N_EXPERTS, LANES), lambda i: (i, 0, 0))],
        out_shape=[jax.ShapeDtypeStruct((n, D_MODEL), F32), jax.ShapeDtypeStruct((n, D_MODEL), BF16),
                   jax.ShapeDtypeStruct((nt, N_EXPERTS, tm), F32),
                   jax.ShapeDtypeStruct((nt, N_EXPERTS, tm), F32),
                   jax.ShapeDtypeStruct((nt, N_EXPERTS, LANES), jnp.int32)],
        compiler_params=_params("parallel"),
    )(x, conv_y, o_cmp, o_slc, o_win, g_nsa, merge_g, mod, mod, mod, gate_expand, w_out,
      ln_g.reshape(1, -1), ln_b.reshape(1, -1), w_router_t, b_router.reshape(-1, 1))


def _moe_kernel(cnt_ref, x1_ref, h2_ref, wt_ref, rank_ref, g2_ref, wup_ref, bup_ref, wdn_ref, bdn_ref,
                lg_ref, lb_ref, o_ref, acc_ref):
    i = pl.program_id(0)
    e = pl.program_id(1)
    tm = x1_ref.shape[0]

    @pl.when(e == 0)
    def _():
        acc_ref[...] = jnp.zeros_like(acc_ref)

    cnt = cnt_ref[i * N_EXPERTS + e]
    for ch in range(tm // MOE_CHUNK):
        @pl.when(cnt > ch * MOE_CHUNK)
        def _():
            w_e = wt_ref[0, pl.ds(e, 1), :]
            slot = rank_ref[0, pl.ds(e, 1), :] - float(ch * MOE_CHUNK)
            rows = _iota((MOE_CHUNK, tm), 0).astype(F32)
            onehot = jnp.where((rows == slot) & (w_e > 0.0), 1.0, 0.0)
            wc = jnp.sum(onehot * w_e, axis=1, keepdims=True)
            sel = onehot.astype(BF16)
            xc = _dot(sel, h2_ref[...]).astype(BF16)
            u = _dot(xc, wup_ref[0]) + bup_ref[0]
            x_glu = jnp.minimum(u[:, :D_EXPERT], SWIGLU_LIMIT)
            x_lin = jnp.clip(u[:, D_EXPERT:], -SWIGLU_LIMIT, SWIGLU_LIMIT)
            a = x_glu * _sigmoid(SWIGLU_ALPHA * x_glu) * (x_lin + 1.0)
            y = (_dot(a.astype(BF16), wdn_ref[0]) + bdn_ref[0]) * wc
            acc_ref[...] += _dot_tn(sel, y.astype(BF16))

    @pl.when(e == N_EXPERTS - 1)
    def _():
        o_ref[...] = _layer_norm(DN_ALPHA * x1_ref[...] + g2_ref[...] * acc_ref[...],
                                 lg_ref[...], lb_ref[...])


def _moe(x1, h2, wt, rank, cnt, mod, w_up, b_up, w_down, b_down, ln_g, ln_b):
    n = x1.shape[0]
    tm = MOE_TILE
    nt = n // tm
    per_row = mod.shape[0] != 1
    g2_spec = pl.BlockSpec((tm if per_row else 1, D_MODEL),
                           (lambda i, e, c: (i, 5)) if per_row else (lambda i, e, c: (0, 5)))
    tile_spec = pl.BlockSpec((1, N_EXPERTS, tm), lambda i, e, c: (i, 0, 0))
    grid_spec = pltpu.PrefetchScalarGridSpec(
        num_scalar_prefetch=1,
        grid=(nt, N_EXPERTS),
        in_specs=[pl.BlockSpec((tm, D_MODEL), lambda i, e, c: (i, 0)),
                  pl.BlockSpec((tm, D_MODEL), lambda i, e, c: (i, 0)),
                  tile_spec, tile_spec, g2_spec,
                  pl.BlockSpec((1, D_MODEL, 2 * D_EXPERT), lambda i, e, c: (e, 0, 0)),
                  pl.BlockSpec((1, 1, 2 * D_EXPERT), lambda i, e, c: (e, 0, 0)),
                  pl.BlockSpec((1, D_EXPERT, D_MODEL), lambda i, e, c: (e, 0, 0)),
                  pl.BlockSpec((1, 1, D_MODEL), lambda i, e, c: (e, 0, 0)),
                  pl.BlockSpec((1, D_MODEL), lambda i, e, c: (0, 0)),
                  pl.BlockSpec((1, D_MODEL), lambda i, e, c: (0, 0))],
        out_specs=pl.BlockSpec((tm, D_MODEL), lambda i, e, c: (i, 0)),
        scratch_shapes=[pltpu.VMEM((tm, D_MODEL), F32)],
    )
    return pl.pallas_call(
        _moe_kernel,
        grid_spec=grid_spec,
        out_shape=jax.ShapeDtypeStruct((n, D_MODEL), F32),
        compiler_params=_params("parallel", "arbitrary"),
    )(cnt, x1, h2, wt, rank, mod, w_up, b_up.reshape(N_EXPERTS, 1, -1), w_down,
      b_down.reshape(N_EXPERTS, 1, -1), ln_g.reshape(1, -1), ln_b.reshape(1, -1))


def _softmax_rows(s, valid):
    s = jnp.where(valid, s, NEG_INF)
    m = jnp.max(s, axis=-1, keepdims=True)
    e = jnp.exp(s - m)
    l = jnp.sum(e, axis=-1, keepdims=True)
    return e * jnp.where(m > 0.5 * NEG_INF, 1.0 / l, 0.0)


def _flash_update(s, valid, v, m_old, l_old, acc_old, pv=_dot):
    s = jnp.where(valid, s, NEG_INF)
    m_new = jnp.maximum(m_old, jnp.max(s, axis=-1, keepdims=True))
    p = jnp.where(valid, jnp.exp(s - m_new), 0.0)
    alpha = jnp.exp(m_old - m_new)
    l_new = alpha * l_old + jnp.sum(p, axis=-1, keepdims=True)
    acc_new = alpha * acc_old + pv(p.astype(BF16), v)
    return m_new, l_new, acc_new


def _shift_right_one(x):
    return jnp.where(_iota(x.shape, 1) == 0, 0.0, pltpu.roll(x, 1, 1))


def _block_scores(imp, nj):
    parts = [imp[:, k * nj:(k + 1) * nj] for k in range(SEL_RATIO)]
    return (parts[0] + parts[1]) + (parts[2] + parts[3]) + _shift_right_one(parts[3])


def _select_blocks(score, qblk, n_blocks):
    j = _iota(score.shape, 1)
    back = qblk - j
    forced = (j == 0) | ((back >= 0) & (back < N_LOCAL))
    score = jnp.where(forced, FORCE_SCORE, jnp.where(back >= 0, score, -1.0))
    score = jnp.where(j < n_blocks, score, -jnp.inf)
    picked = jnp.zeros(score.shape, F32)
    lane = j.astype(F32)
    for _ in range(N_SEL):
        m = jnp.max(score, axis=-1, keepdims=True)
        first = jnp.min(jnp.where(score == m, lane, float(SEL_LANES)), axis=-1, keepdims=True)
        hit = lane == first
        picked = jnp.where(hit, 1.0, picked)
        score = jnp.where(hit, -jnp.inf, score)
    return picked


def _gelu_tanh(x):
    return x * (0.5 * (1.0 + jnp.tanh(0.7978845608028654 * (x + 0.044715 * (x * x * x)))))


def _compress_rows(buf_ref, pe_ref, w1_ref, w2_ref, nj):
    outs = []
    for pair in range(HEAD_PAIRS):
        acc = jnp.zeros((SEL_RATIO * nj, 2 * PHI_HIDDEN), F32)
        for s in range(CMP_LEN):
            parts = [buf_ref[pair, pl.ds(CMP_STRIDE * k + s, nj, stride=SEL_LEN), :] for k in range(SEL_RATIO)]
            x = (jnp.concatenate(parts, axis=0) + pe_ref[s:s + 1, :]).astype(BF16)
            acc = acc + _dot(x, w1_ref[s])
        outs.append(_dot(_gelu_tanh(acc).astype(BF16), w2_ref[...]))
    return jnp.concatenate(outs, axis=1)


CMP_TILE = 4096


def _cmp_prompt_kernel(x_ref, halo_ref, pe_ref, w1_ref, w2_ref, o_ref, buf_ref):
    t = pl.program_id(0)
    last = t == pl.num_programs(0) - 1
    for pair in range(HEAD_PAIRS):
        cols = slice(pair * LANES, (pair + 1) * LANES)
        buf_ref[pair, :CMP_TILE, :] = x_ref[:, cols]
        buf_ref[pair, CMP_TILE:, :] = jnp.where(last, 0.0, halo_ref[:, cols])
    nj = CMP_TILE // SEL_LEN
    out = _compress_rows(buf_ref, pe_ref, w1_ref, w2_ref, nj)
    o_ref[...] = out.reshape(SEL_RATIO, nj, KV_WIDTH).astype(BF16)


def _compress_prompt(kv4, kv_set, pe, w1, w2):
    t_len = kv4.shape[0]
    nt = t_len // CMP_TILE
    nj = CMP_TILE // SEL_LEN
    halo_blocks = CMP_TILE // SEL_LEN
    last_halo = t_len // SEL_LEN - 1
    out = pl.pallas_call(
        _cmp_prompt_kernel,
        grid=(nt,),
        in_specs=[pl.BlockSpec((CMP_TILE, KV_WIDTH), lambda t: (t, kv_set)),
                  pl.BlockSpec((SEL_LEN, KV_WIDTH),
                               lambda t: (jnp.minimum((t + 1) * halo_blocks, last_halo), kv_set)),
                  _const_spec(pe.shape), _const_spec(w1.shape), _const_spec(w2.shape)],
        out_specs=pl.BlockSpec((SEL_RATIO, nj, KV_WIDTH), lambda t: (0, t, 0)),
        out_shape=jax.ShapeDtypeStruct((SEL_RATIO, t_len // SEL_LEN, KV_WIDTH), BF16),
        scratch_shapes=[pltpu.VMEM((HEAD_PAIRS, CMP_TILE + SEL_LEN, LANES), F32)],
        compiler_params=_params("arbitrary"),
    )(kv4, kv4, pe, w1, w2)
    return out.reshape(t_len // CMP_STRIDE, KV_WIDTH)


def _stack_heads(q_ref, g):
    return jnp.concatenate([q_ref[:, (g * GROUP + r) * HEAD_DIM:(g * GROUP + r + 1) * HEAD_DIM]
                            for r in range(GROUP)], axis=0)


def _head_bias(g, rel):
    return jnp.concatenate([jnp.broadcast_to(SLOPES[g * GROUP + r] * rel, (Q_TILE, rel.shape[1]))
                            for r in range(GROUP)], axis=0)


def _cmp_attn_prompt_kernel(q_ref, kc_ref, vc_ref, o_ref, sel_ref, any_ref):
    nc = kc_ref.shape[0]
    nj = nc // SEL_RATIO
    s0 = pl.program_id(0) * Q_TILE
    tq = s0 + _iota((Q_TILE, 1), 0)
    col = _iota((1, nc), 1)
    ends = (SEL_RATIO * (col % nj) + col // nj) * CMP_STRIDE + (CMP_LEN - 1)
    rel = (ends - s0).astype(F32)
    valid = jnp.concatenate([ends <= tq] * GROUP, axis=0)
    any_rows = []
    scores = []
    for g in range(N_KV_HEADS):
        kc = kc_ref[:, g * HEAD_DIM:(g + 1) * HEAD_DIM]
        vc = vc_ref[:, g * HEAD_DIM:(g + 1) * HEAD_DIM]
        p = _softmax_rows(_dot_nt(_stack_heads(q_ref, g), kc) + _head_bias(g, rel), valid)
        o = _dot(p.astype(BF16), vc)
        imp = jnp.zeros((Q_TILE, nc), F32)
        for r in range(GROUP):
            h = g * GROUP + r
            o_ref[:, h * HEAD_DIM:(h + 1) * HEAD_DIM] = o[r * Q_TILE:(r + 1) * Q_TILE]
            imp = imp + p[r * Q_TILE:(r + 1) * Q_TILE]
        scores.append(_block_scores(imp, nj))
    picked = _select_blocks(jnp.concatenate(scores, axis=0),
                            jnp.concatenate([tq // SEL_LEN] * N_KV_HEADS, axis=0), nj)
    for g in range(N_KV_HEADS):
        sel_ref[g] = picked[g * Q_TILE:(g + 1) * Q_TILE].astype(BF16)
        any_rows.append(jnp.max(picked[g * Q_TILE:(g + 1) * Q_TILE], axis=0, keepdims=True))
    any_ref[0] = jnp.concatenate(any_rows + [jnp.zeros((ANY_ROWS - N_KV_HEADS, SEL_LANES), F32)], axis=0)


def _cmp_attn_prompt(q, kc, vc):
    t_len = q.shape[0]
    nt = t_len // Q_TILE
    return pl.pallas_call(
        _cmp_attn_prompt_kernel,
        grid=(nt,),
        in_specs=[pl.BlockSpec((Q_TILE, Q_WIDTH), lambda i: (i, 0)),
                  _const_spec(kc.shape), _const_spec(vc.shape)],
        out_specs=[pl.BlockSpec((Q_TILE, Q_WIDTH), lambda i: (i, 0)),
                   pl.BlockSpec((N_KV_HEADS, Q_TILE, SEL_LANES), lambda i: (0, i, 0)),
                   pl.BlockSpec((1, ANY_ROWS, SEL_LANES), lambda i: (i, 0, 0))],
        out_shape=[jax.ShapeDtypeStruct((t_len, Q_WIDTH), F32),
                   jax.ShapeDtypeStruct((N_KV_HEADS, t_len, SEL_LANES), BF16),
                   jax.ShapeDtypeStruct((nt, ANY_ROWS, SEL_LANES), F32)],
        compiler_params=_params("parallel"),
    )(q, kc, vc)


def _tile_flags(any_sel):
    nt = any_sel.shape[0]
    per_tile = KV_TILE // SEL_LEN
    hit = any_sel[:, :N_KV_HEADS].reshape(nt, N_KV_HEADS, SEL_LANES // per_tile, per_tile).max(-1) > 0.0
    bits = hit.reshape(nt, N_KV_HEADS, -1, FLAG_BITS).astype(jnp.uint32) << jnp.arange(FLAG_BITS, dtype=jnp.uint32)
    return lax.bitcast_convert_type(bits.sum(-1, dtype=jnp.uint32), jnp.int32).reshape(-1)


def _slc_attn_prompt_kernel(flag_ref, q_ref, k_ref, v_ref, sel_ref, o_ref, qs_ref, m_ref, l_ref, acc_ref):
    i = pl.program_id(0)
    tq = i * Q_TILE + _iota((Q_TILE, 1), 0)
    for h in range(N_HEADS):
        qs_ref[h // GROUP, (h % GROUP) * Q_TILE:(h % GROUP + 1) * Q_TILE, :] = q_ref[:, h * HEAD_DIM:(h + 1) * HEAD_DIM]
    m_ref[...] = jnp.full(m_ref.shape, NEG_INF, F32)
    l_ref[...] = jnp.zeros(l_ref.shape, F32)
    acc_ref[...] = jnp.zeros(acc_ref.shape, F32)

    def tile(t, carry):
        base = pl.multiple_of(t * KV_TILE, KV_TILE)
        causal = tq - (base + _iota((1, KV_TILE), 1)) >= 0
        rel = (base - i * Q_TILE + _iota((1, KV_TILE), 1)).astype(F32)
        expand = jnp.where(_iota((SEL_LANES, KV_TILE), 0)
                           == t * (KV_TILE // SEL_LEN) + _iota((SEL_LANES, KV_TILE), 1) // SEL_LEN,
                           1.0, 0.0).astype(BF16)
        for g in range(N_KV_HEADS):
            word = flag_ref[(i * N_KV_HEADS + g) * FLAG_WORDS + t // FLAG_BITS]

            @pl.when(((word >> (t % FLAG_BITS)) & 1) == 1)
            def _():
                valid = (_dot(sel_ref[g], expand) > 0.5) & causal
                kt = k_ref[pl.ds(base, KV_TILE), g * HEAD_DIM:(g + 1) * HEAD_DIM]
                vt = v_ref[pl.ds(base, KV_TILE), g * HEAD_DIM:(g + 1) * HEAD_DIM]
                raw = _dot_nt(qs_ref[g], kt)
                s = jnp.concatenate(
                    [jnp.where(valid, raw[r * Q_TILE:(r + 1) * Q_TILE] + SLOPES[g * GROUP + r] * rel, NEG_INF)
                     for r in range(GROUP)], axis=0)
                m_old = m_ref[g]
                m_new = jnp.maximum(m_old, jnp.max(s, axis=-1, keepdims=True))
                p = jnp.exp(s - jnp.concatenate([m_new] * (KV_TILE // LANES), axis=1))
                alpha = jnp.exp(m_old - m_new)
                l_ref[g] = alpha * l_ref[g] + jnp.sum(p, axis=-1, keepdims=True)
                acc_ref[g] = alpha[:, :HEAD_DIM] * acc_ref[g] + _dot(p.astype(BF16), vt)
                m_ref[g] = m_new
        return carry

    lax.fori_loop(0, (i * Q_TILE) // KV_TILE + 1, tile, 0)
    for h in range(N_HEADS):
        rows = slice((h % GROUP) * Q_TILE, (h % GROUP + 1) * Q_TILE)
        l = l_ref[h // GROUP, rows, :HEAD_DIM]
        o_ref[:, h * HEAD_DIM:(h + 1) * HEAD_DIM] = acc_ref[h // GROUP, rows, :] / jnp.where(l > 0.0, l, 1.0)


def _slc_attn_prompt(q, kvb, sel, flags):
    t_len = q.shape[0]
    resident = functools.partial(pl.BlockSpec, pipeline_mode=pl.Buffered(1))
    rows = GROUP * Q_TILE
    grid_spec = pltpu.PrefetchScalarGridSpec(
        num_scalar_prefetch=1,
        grid=(t_len // Q_TILE,),
        in_specs=[pl.BlockSpec((Q_TILE, Q_WIDTH), lambda i, f: (i, 0)),
                  resident((t_len, KV_WIDTH), lambda i, f: (0, 2)),
                  resident((t_len, KV_WIDTH), lambda i, f: (0, 3)),
                  pl.BlockSpec((N_KV_HEADS, Q_TILE, SEL_LANES), lambda i, f: (0, i, 0))],
        out_specs=pl.BlockSpec((Q_TILE, Q_WIDTH), lambda i, f: (i, 0)),
        scratch_shapes=[pltpu.VMEM((N_KV_HEADS, rows, HEAD_DIM), BF16),
                        pltpu.VMEM((N_KV_HEADS, rows, LANES), F32), pltpu.VMEM((N_KV_HEADS, rows, LANES), F32),
                        pltpu.VMEM((N_KV_HEADS, rows, HEAD_DIM), F32)],
    )
    return pl.pallas_call(
        _slc_attn_prompt_kernel,
        grid_spec=grid_spec,
        out_shape=jax.ShapeDtypeStruct((t_len, Q_WIDTH), F32),
        compiler_params=_params("parallel"),
    )(flags, q, kvb, kvb, sel)


def _win_attn_prompt_kernel(q_ref, k_ref, v_ref, o_ref):
    s0 = pl.multiple_of(pl.program_id(0) * Q_TILE, Q_TILE)
    n_keys = WINDOW + Q_TILE
    tq = s0 + _iota((Q_TILE, 1), 0)
    pw = s0 - WINDOW + _iota((1, n_keys), 1)
    dw = tq - pw
    valid = (dw >= 0) & (dw < WINDOW) & (pw >= 0)
    valid = jnp.concatenate([valid] * GROUP, axis=0)
    rel = (_iota((1, n_keys), 1) - WINDOW).astype(F32)
    for g in range(N_KV_HEADS):
        kw = k_ref[pl.ds(s0, n_keys), g * HEAD_DIM:(g + 1) * HEAD_DIM]
        vw = v_ref[pl.ds(s0, n_keys), g * HEAD_DIM:(g + 1) * HEAD_DIM]
        p = _softmax_rows(_dot_nt(_stack_heads(q_ref, g), kw) + _head_bias(g, rel), valid)
        o = _dot(p.astype(BF16), vw)
        for r in range(GROUP):
            h = g * GROUP + r
            o_ref[:, h * HEAD_DIM:(h + 1) * HEAD_DIM] = o[r * Q_TILE:(r + 1) * Q_TILE]


def _win_attn_prompt(q, kw_pad, vw_pad):
    t_len = q.shape[0]
    resident = functools.partial(pl.BlockSpec, pipeline_mode=pl.Buffered(1))
    return pl.pallas_call(
        _win_attn_prompt_kernel,
        grid=(t_len // Q_TILE,),
        in_specs=[pl.BlockSpec((Q_TILE, Q_WIDTH), lambda i: (i, 0)),
                  resident(kw_pad.shape, lambda i: (0, 0)), resident(vw_pad.shape, lambda i: (0, 0))],
        out_specs=pl.BlockSpec((Q_TILE, Q_WIDTH), lambda i: (i, 0)),
        out_shape=jax.ShapeDtypeStruct((t_len, Q_WIDTH), F32),
        compiler_params=_params("parallel"),
    )(q, kw_pad, vw_pad)


S_ROWS = N_HEADS * DEC_SEQ
S_TAIL = SEL_LEN
S_CMP_BLOCKS = PAST_LEN // CMP_STRIDE
S_CHUNK = 2048


def _fetch_pages(pt_ref, sem_ref, page_copy):
    b = pl.program_id(0)
    slot = b % 2

    def copy(seq, p, dst_slot):
        return page_copy(pt_ref[seq * N_PAGES + p], p, dst_slot, sem_ref.at[dst_slot])

    def start(seq, dst_slot):
        def body(p, c):
            copy(seq, p, dst_slot).start()
            return c
        lax.fori_loop(0, N_PAGES, body, 0)

    @pl.when(b == 0)
    def _():
        start(b, slot)

    @pl.when(b + 1 < pl.num_programs(0))
    def _():
        start(b + 1, 1 - slot)

    def wait(p, c):
        copy(b, p, slot).wait()
        return c
    lax.fori_loop(0, N_PAGES, wait, 0)
    return slot


PAGES_PER_STEP = 8
TAPS_PER_DOT = 8
S_CHUNKS = PAST_LEN // CMP_STRIDE + ANY_ROWS


def _cmp_sample_kernel(kv_set, pt_ref, cache_ref, new_ref, pet_ref, w1_ref, w2_ref, o_ref,
                       stage_ref, buf_ref, sem_ref):
    def page_copy(page, p, slot, sem):
        return pltpu.make_async_copy(cache_ref.at[page, kv_set],
                                     stage_ref.at[slot, pl.ds(p * KV_WIDTH, KV_WIDTH), :], sem)

    slot = _fetch_pages(pt_ref, sem_ref, page_copy)
    per_page = PAGE_SIZE // CMP_STRIDE
    regroup = jnp.where((_iota((PAGE_SIZE, PAGE_SIZE), 1) % CMP_STRIDE) * per_page
                        + _iota((PAGE_SIZE, PAGE_SIZE), 1) // CMP_STRIDE == _iota((PAGE_SIZE, PAGE_SIZE), 0),
                        1.0, 0.0).astype(BF16)

    def to_rows(i, c):
        for u in range(PAGES_PER_STEP):
            p = i * PAGES_PER_STEP + u
            src = pl.multiple_of(p * KV_WIDTH, KV_WIDTH)
            dst = pl.multiple_of(p * per_page, per_page)
            x_t = stage_ref[slot, pl.ds(src, KV_WIDTH), :].astype(BF16)
            rows = _dot_nt(regroup, x_t)
            for pair in range(HEAD_PAIRS):
                for s in range(CMP_STRIDE):
                    buf_ref[pair, s, pl.ds(dst, per_page), :] = rows[s * per_page:(s + 1) * per_page,
                                                                     pair * LANES:(pair + 1) * LANES]
        return c
    lax.fori_loop(0, N_PAGES // PAGES_PER_STEP, to_rows, 0)
    first_row = _iota((ANY_ROWS, LANES), 0) == 0
    for pair in range(HEAD_PAIRS):
        for s in range(CMP_STRIDE):
            tail = jnp.zeros((ANY_ROWS, LANES), F32)
            if s < DEC_SEQ:
                row = new_ref[s:s + 1, pair * LANES:(pair + 1) * LANES].astype(BF16).astype(F32)
                tail = jnp.where(first_row, row, 0.0)
            buf_ref[pair, s, PAST_LEN // CMP_STRIDE:, :] = tail

    bias = jnp.zeros((1, 2 * PHI_HIDDEN), F32)
    for s in range(CMP_LEN):
        bias = bias + jnp.sum(pet_ref[:, s:s + 1] * w1_ref[s].astype(F32), axis=0, keepdims=True)
    nj = S_CMP_BLOCKS // SEL_RATIO
    outs = []
    for pair in range(HEAD_PAIRS):
        acc = jnp.zeros((SEL_RATIO * nj, 2 * PHI_HIDDEN), F32)
        for s0 in range(0, CMP_LEN, TAPS_PER_DOT):
            taps = [jnp.concatenate(
                [buf_ref[pair, s % CMP_STRIDE, pl.ds(k + s // CMP_STRIDE, nj, stride=SEL_RATIO), :]
                 for k in range(SEL_RATIO)], axis=0).astype(BF16) for s in range(s0, s0 + TAPS_PER_DOT)]
            w = w1_ref[s0:s0 + TAPS_PER_DOT].reshape(TAPS_PER_DOT * LANES, 2 * PHI_HIDDEN)
            acc = acc + _dot(jnp.concatenate(taps, axis=1), w)
        outs.append(_dot(_gelu_tanh(acc + bias).astype(BF16), w2_ref[...]))
    o_ref[0] = jnp.concatenate(outs, axis=1).astype(BF16)


def _compress_sample(page_table, cache, kv4_new, kv_set, pe, w1, w2):
    pe_t = pe.T
    grid_spec = pltpu.PrefetchScalarGridSpec(
        num_scalar_prefetch=1,
        grid=(DEC_BATCH,),
        in_specs=[pl.BlockSpec(memory_space=pl.ANY),
                  pl.BlockSpec((DEC_SEQ, KV_WIDTH), lambda b, pt: (b, kv_set)),
                  pl.BlockSpec(pe_t.shape, lambda b, pt: (0, 0)),
                  pl.BlockSpec(w1.shape, lambda b, pt: (0, 0, 0)),
                  pl.BlockSpec(w2.shape, lambda b, pt: (0, 0))],
        out_specs=pl.BlockSpec((1, S_CMP_BLOCKS, KV_WIDTH), lambda b, pt: (b, 0, 0)),
        scratch_shapes=[pltpu.VMEM((2, N_PAGES * KV_WIDTH, PAGE_SIZE), F32),
                        pltpu.VMEM((HEAD_PAIRS, CMP_STRIDE, S_CHUNKS, LANES), F32),
                        pltpu.SemaphoreType.DMA((2,))],
    )
    return pl.pallas_call(
        functools.partial(_cmp_sample_kernel, kv_set),
        grid_spec=grid_spec,
        out_shape=jax.ShapeDtypeStruct((DEC_BATCH, S_CMP_BLOCKS, KV_WIDTH), BF16),
        compiler_params=_params("arbitrary"),
    )(page_table.reshape(-1), cache, kv4_new, pe_t, w1, w2)


def _row_queries():
    return PAST_LEN + _iota((S_ROWS, 1), 0) % DEC_SEQ


def _cmp_attn_sample_kernel(q_ref, slope_ref, kc_ref, vc_ref, o_ref, sel_ref):
    nc = S_CMP_BLOCKS
    nj = nc // SEL_RATIO
    tq = _row_queries()
    col = _iota((1, nc), 1)
    ends = (SEL_RATIO * (col % nj) + col // nj) * CMP_STRIDE + (CMP_LEN - 1)
    s = _dot_nt(q_ref[0], kc_ref[0]) - slope_ref[...] * (tq - ends).astype(F32)
    p = _softmax_rows(s, ends <= tq)
    o_ref[0] = _dot(p.astype(BF16), vc_ref[0])
    rows_g = GROUP * DEC_SEQ
    n_blocks = (PAST_LEN + DEC_SEQ + SEL_LEN - 1) // SEL_LEN
    qblk = (PAST_LEN + _iota((DEC_SEQ, 1), 0)) // SEL_LEN
    scores = []
    for g in range(N_KV_HEADS):
        imp = jnp.zeros((DEC_SEQ, nc), F32)
        for r in range(GROUP):
            imp = imp + p[g * rows_g + r * DEC_SEQ:g * rows_g + (r + 1) * DEC_SEQ]
        scores.append(jnp.concatenate([_block_scores(imp, nj), jnp.zeros((DEC_SEQ, SEL_LANES - nj), F32)], axis=1))
    picked = _select_blocks(jnp.concatenate(scores, axis=0), jnp.concatenate([qblk] * N_KV_HEADS, axis=0), n_blocks)
    sel_ref[0] = jnp.concatenate([picked[g * DEC_SEQ:(g + 1) * DEC_SEQ] for g in range(N_KV_HEADS)
                                  for _ in range(GROUP)], axis=0).astype(BF16)


def _cmp_attn_sample(qbd, slope_col, kc, vc):
    seq_spec = lambda rows, w: pl.BlockSpec((1, rows, w), lambda b: (b, 0, 0))
    return pl.pallas_call(
        _cmp_attn_sample_kernel,
        grid=(DEC_BATCH,),
        in_specs=[seq_spec(S_ROWS, KV_WIDTH), _const_spec((S_ROWS, 1)),
                  seq_spec(S_CMP_BLOCKS, KV_WIDTH), seq_spec(S_CMP_BLOCKS, KV_WIDTH)],
        out_specs=[seq_spec(S_ROWS, KV_WIDTH), seq_spec(S_ROWS, SEL_LANES)],
        out_shape=[jax.ShapeDtypeStruct((DEC_BATCH, S_ROWS, KV_WIDTH), F32),
                   jax.ShapeDtypeStruct((DEC_BATCH, S_ROWS, SEL_LANES), BF16)],
        compiler_params=_params("parallel"),
    )(qbd, slope_col, kc, vc)


def _pad_new_rows(x):
    return jnp.concatenate([x, jnp.zeros((LANES - DEC_SEQ, x.shape[1]), F32)], axis=0).astype(BF16)


def _slc_win_sample_kernel(pt_ref, cache_ref, q_ref, slope_ref, sel_ref, new_ref, cwin_ref, wnew_ref,
                           os_ref, ow_ref, buf_ref, sem_ref):
    def page_copy(page, p, slot, sem):
        return pltpu.make_async_copy(cache_ref.at[page, pl.ds(2, 2)],
                                     buf_ref.at[slot, :, :, pl.ds(p * PAGE_SIZE, PAGE_SIZE)], sem)

    slot = _fetch_pages(pt_ref, sem_ref, page_copy)
    q = q_ref[0]
    slope = slope_ref[...]
    sel = sel_ref[0]
    tq = _row_queries()
    new_pos = PAST_LEN + _iota((1, LANES), 1)
    new_ok = (tq - new_pos >= 0) & (new_pos < PAST_LEN + DEC_SEQ)
    new_dist = (tq - new_pos).astype(F32)

    m = jnp.full((S_ROWS, 1), NEG_INF, F32)
    l = jnp.zeros((S_ROWS, 1), F32)
    acc = jnp.zeros((S_ROWS, KV_WIDTH), F32)
    for c in range(PAST_LEN // S_CHUNK):
        k_t = buf_ref[slot, 0, :, c * S_CHUNK:(c + 1) * S_CHUNK].astype(BF16)
        v_t = buf_ref[slot, 1, :, c * S_CHUNK:(c + 1) * S_CHUNK].astype(BF16)
        pos = c * S_CHUNK + _iota((1, S_CHUNK), 1)
        expand = jnp.where(_iota((SEL_LANES, S_CHUNK), 0)
                           == (c * S_CHUNK + _iota((SEL_LANES, S_CHUNK), 1)) // SEL_LEN, 1.0, 0.0).astype(BF16)
        valid = (_dot(sel, expand) > 0.5) & (tq - pos >= 0)
        s = _dot(q, k_t) - slope * (tq - pos).astype(F32)
        m, l, acc = _flash_update(s, valid, v_t, m, l, acc, pv=_dot_nt)
    new = _pad_new_rows(new_ref[...])
    in_last = jnp.sum(jnp.where(_iota((S_ROWS, SEL_LANES), 1) == PAST_LEN // SEL_LEN, sel.astype(F32), 0.0),
                      axis=1, keepdims=True) > 0.5
    s = _dot_nt(q, new[:, :KV_WIDTH]) - slope * new_dist
    m, l, acc = _flash_update(s, in_last & new_ok, new[:, KV_WIDTH:], m, l, acc)
    os_ref[0] = acc / jnp.where(l > 0.0, l, 1.0)

    wb = cwin_ref.shape[-1]
    pw = PAST_LEN - wb + _iota((1, wb), 1)
    dw = tq - pw
    s_past = _dot(q, cwin_ref[0, 0].astype(BF16)) - slope * dw.astype(F32)
    wnew = _pad_new_rows(wnew_ref[...])
    s_new = _dot_nt(q, wnew[:, :KV_WIDTH]) - slope * new_dist
    valid = jnp.concatenate([(dw >= 0) & (dw < WINDOW) & (pw >= 0), new_ok & (tq - new_pos < WINDOW)], axis=1)
    p = _softmax_rows(jnp.concatenate([s_past, s_new], axis=1), valid).astype(BF16)
    ow_ref[0] = _dot_nt(p[:, :wb], cwin_ref[0, 1].astype(BF16)) + _dot(p[:, wb:], wnew[:, KV_WIDTH:])


def _slc_win_sample(page_table, cache, qbd, slope_col, sel, kv4_new, cache_win, win_new):
    wb = cache_win.shape[-1]
    seq_spec = lambda rows, w: pl.BlockSpec((1, rows, w), lambda b, pt: (b, 0, 0))
    grid_spec = pltpu.PrefetchScalarGridSpec(
        num_scalar_prefetch=1,
        grid=(DEC_BATCH,),
        in_specs=[pl.BlockSpec(memory_space=pl.ANY),
                  seq_spec(S_ROWS, KV_WIDTH),
                  pl.BlockSpec((S_ROWS, 1), lambda b, pt: (0, 0)),
                  seq_spec(S_ROWS, SEL_LANES),
                  pl.BlockSpec((DEC_SEQ, 2 * KV_WIDTH), lambda b, pt: (b, 1)),
                  pl.BlockSpec((1, 2, KV_WIDTH, wb), lambda b, pt: (b, 0, 0, 0)),
                  pl.BlockSpec((DEC_SEQ, 2 * KV_WIDTH), lambda b, pt: (b, 0))],
        out_specs=[seq_spec(S_ROWS, KV_WIDTH), seq_spec(S_ROWS, KV_WIDTH)],
        scratch_shapes=[pltpu.VMEM((2, 2, KV_WIDTH, PAST_LEN), F32), pltpu.SemaphoreType.DMA((2,))],
    )
    return pl.pallas_call(
        _slc_win_sample_kernel,
        grid_spec=grid_spec,
        out_shape=[jax.ShapeDtypeStruct((DEC_BATCH, S_ROWS, KV_WIDTH), F32)] * 2,
        compiler_params=_params("arbitrary"),
    )(page_table.reshape(-1), cache, qbd, slope_col, sel, kv4_new, cache_win, win_new)


def _gate_expand():
    rows = jnp.arange(GATE_PAD)[:, None]
    cols = jnp.arange(D_MODEL)[None, :]
    return jnp.stack([(rows == (cols // HEAD_DIM) * N_NSA_BRANCH + n) for n in range(N_NSA_BRANCH)]
                     ).astype(BF16)


def _compress_weights(pe, w1, w2):
    per_tile = LANES // HEAD_DIM
    eye = jnp.eye(per_tile, dtype=F32)
    w1_big = jnp.einsum('sde,gh->sgdhe', w1, eye).reshape(CMP_LEN, LANES, per_tile * PHI_HIDDEN).astype(BF16)
    w2_big = jnp.einsum('ed,gh->gehd', w2, eye).reshape(per_tile * PHI_HIDDEN, LANES).astype(BF16)
    return jnp.tile(pe, (1, per_tile)), w1_big, w2_big


def _rows_from_heads(o):
    o = o.reshape(DEC_BATCH, N_KV_HEADS, GROUP, DEC_SEQ, N_KV_HEADS, HEAD_DIM)
    idx = jnp.arange(N_KV_HEADS)
    own = o[:, idx, :, :, idx]
    return own.transpose(1, 3, 0, 2, 4).reshape(DEC_BATCH * DEC_SEQ, Q_WIDTH)


def kernel(x_prompt, x_sample, c_prompt, c_sample, cache_kv, cache_win, state_conv, page_table, w_ada, b_ada, w_in, conv_dw_w, conv_dw_b, conv_ln_g, conv_ln_b, w_conv_out, cmp_pe_k, cmp_w1_k, cmp_w2_k, cmp_pe_v, cmp_w1_v, cmp_w2_v, w_out, ln1_g, ln1_b, w_router, b_router, w_up, b_up, w_down, b_down, ln2_g, ln2_b):
    l = 0
    n_p = SEQ
    n_s = DEC_BATCH * DEC_SEQ

    c_all = jnp.concatenate([c_prompt, c_sample, jnp.zeros((7, D_MODEL), F32)], axis=0)
    mod = _adaln(c_all, w_ada[l], b_ada[l])
    mod_p = mod[0:1]
    mod_s = jnp.repeat(mod[1:1 + DEC_BATCH], DEC_SEQ, axis=0)

    wi = w_in[l]
    wts = ((wi[:, :OFF_KV] * (HEAD_DIM ** -0.5)).astype(BF16),
           wi[:, OFF_KV:OFF_GLU].astype(BF16),
           wi[:, OFF_GLU:OFF_NSA_G].astype(BF16),
           jnp.pad(wi[:, OFF_NSA_G:OFF_MERGE], ((0, 0), (0, GATE_PAD - N_HEADS * N_NSA_BRANCH))).astype(BF16),
           wi[:, OFF_MERGE:].astype(BF16))
    gate_expand = _gate_expand()
    w_out_b = w_out[l].astype(BF16)
    w_pw_b = w_conv_out[l].astype(BF16)
    w_up_b = w_up[l].astype(BF16)
    w_down_b = w_down[l].astype(BF16)
    w_router_t = w_router[l].T
    cmp_k = _compress_weights(cmp_pe_k[l], cmp_w1_k[l], cmp_w2_k[l])
    cmp_v = _compress_weights(cmp_pe_v[l], cmp_w1_v[l], cmp_w2_v[l])
    conv_w = (conv_dw_w[l], conv_dw_b[l], conv_ln_g[l], conv_ln_b[l], w_pw_b)

    def tail(x, conv_y, o3, g_nsa, merge_g, m):
        x1, h2, wt, rank, cnt = _merge(x, conv_y, *o3, g_nsa, merge_g, m, gate_expand, w_out_b,
                                       ln1_g[l], ln1_b[l], w_router_t, b_router[l])
        per = MOE_TILE // MERGE_TILE
        cnt = cnt[:, :, 0].reshape(-1, per, N_EXPERTS)
        before = jnp.cumsum(cnt, axis=1) - cnt
        rank = rank.reshape(-1, per, N_EXPERTS, MERGE_TILE) + before[..., None].astype(F32)
        rank = rank.transpose(0, 2, 1, 3).reshape(-1, N_EXPERTS, MOE_TILE)
        wt = wt.reshape(-1, per, N_EXPERTS, MERGE_TILE).transpose(0, 2, 1, 3).reshape(-1, N_EXPERTS, MOE_TILE)
        return _moe(x1, h2, wt, rank, cnt.sum(1).reshape(-1), m, w_up_b, b_up[l], w_down_b, b_down[l],
                    ln2_g[l], ln2_b[l])

    xp = x_prompt.reshape(n_p, D_MODEL)
    q, kv4, win2, kvb, u, g_nsa, merge_g = _project(xp, mod_p, mod_p, wts, 256)
    ct = 512
    u3 = u.reshape(n_p // ct, ct, D_CONV)
    halo = jnp.concatenate([jnp.zeros((1, CONV_HALO, D_CONV), F32), u3[:-1, ct - CONV_HALO:]], axis=0)
    conv_y = _conv_branch(u3, halo, *conv_w, 1).reshape(n_p, D_MODEL)
    kc = _compress_prompt(kv4, 0, *cmp_k)
    vc = _compress_prompt(kv4, 1, *cmp_v)
    o_cmp, sel, any_sel = _cmp_attn_prompt(q, kc, vc)
    o_slc = _slc_attn_prompt(q, kvb, sel, _tile_flags(any_sel))
    kw_pad = jnp.pad(kvb[:, 4 * KV_WIDTH:5 * KV_WIDTH], ((WINDOW, 0), (0, 0)))
    vw_pad = jnp.pad(kvb[:, 5 * KV_WIDTH:], ((WINDOW, 0), (0, 0)))
    o_win = _win_attn_prompt(q, kw_pad, vw_pad)
    y_p = tail(xp, conv_y, (o_cmp, o_slc, o_win), g_nsa, merge_g, mod_p)
    out_kv_p = kv4.reshape(1, 1, n_p, 4, N_KV_HEADS, HEAD_DIM)
    out_win_p = win2[n_p - WINDOW:].reshape(1, 1, WINDOW, 2, N_KV_HEADS, HEAD_DIM)
    out_conv_p = u[n_p - (CONV_WIDTH - 1):].reshape(1, 1, CONV_WIDTH - 1, D_CONV)

    xs = x_sample.reshape(n_s, D_MODEL)
    q, kv4, win2, kvb, u, g_nsa, merge_g = _project(xs, mod_s, mod_s, wts, 256)
    u3 = u.reshape(DEC_BATCH, DEC_SEQ, D_CONV)
    st = state_conv[l]
    halo = jnp.concatenate([jnp.zeros((DEC_BATCH, CONV_HALO - (CONV_WIDTH - 1), D_CONV), F32), st], axis=1)
    conv_y = _conv_branch(u3, halo, *conv_w, DEC_BATCH).reshape(n_s, D_MODEL)
    qh = q.reshape(DEC_BATCH, DEC_SEQ, N_KV_HEADS, GROUP, HEAD_DIM).transpose(0, 2, 3, 1, 4)
    qbd = jnp.einsum('bgrqd,gh->bgrqhd', qh, jnp.eye(N_KV_HEADS, dtype=BF16)).reshape(DEC_BATCH, S_ROWS, KV_WIDTH)
    slope_col = jnp.repeat(jnp.asarray(SLOPES, F32), DEC_SEQ).reshape(S_ROWS, 1)
    cache = cache_kv[l].transpose(0, 2, 3, 4, 1).reshape(-1, 4, KV_WIDTH, PAGE_SIZE)
    cwin = cache_win[l].transpose(0, 2, 3, 4, 1).reshape(DEC_BATCH, 2, KV_WIDTH, -1)
    kc = _compress_sample(page_table, cache, kv4, 0, *cmp_k)
    vc = _compress_sample(page_table, cache, kv4, 1, *cmp_v)
    o_cmp, sel = _cmp_attn_sample(qbd, slope_col, kc, vc)
    o_slc, o_win = _slc_win_sample(page_table, cache, qbd, slope_col, sel, kv4, cwin, win2)
    o3 = tuple(_rows_from_heads(o) for o in (o_cmp, o_slc, o_win))
    y_s = tail(xs, conv_y, o3, g_nsa, merge_g, mod_s)
    out_kv_s = kv4.reshape(1, DEC_BATCH, DEC_SEQ, 4, N_KV_HEADS, HEAD_DIM)
    win_new = win2.reshape(DEC_BATCH, DEC_SEQ, 2, N_KV_HEADS, HEAD_DIM)
    out_win_s = jnp.concatenate([cache_win[l], win_new], axis=1)[:, DEC_SEQ:][None]
    out_conv_s = jnp.concatenate([st, u3], axis=1)[:, DEC_SEQ:][None]

    return (y_p.reshape(1, n_p, D_MODEL), y_s.reshape(DEC_BATCH, DEC_SEQ, D_MODEL),
            out_kv_p, out_kv_s, out_win_p, out_win_s, out_conv_p, out_conv_s)
```

```python
import functools

import jax
import jax.numpy as jnp
from jax import lax
from jax.experimental import pallas as pl
from jax.experimental.pallas import tpu as pltpu

D_MODEL = 1024
SEQ = 16384
DEC_BATCH = 128
DEC_SEQ = 8
PAST_LEN = 8192
PAGE_SIZE = 128
N_PAGES = PAST_LEN // PAGE_SIZE
N_HEADS = 16
HEAD_DIM = 64
N_KV_HEADS = 4
GROUP = N_HEADS // N_KV_HEADS
KV_WIDTH = N_KV_HEADS * HEAD_DIM
Q_WIDTH = N_HEADS * HEAD_DIM
N_KV_SETS = 6
N_NSA_BRANCH = 3
CMP_LEN = 32
CMP_STRIDE = 16
PHI_HIDDEN = 2 * HEAD_DIM
SEL_LEN = 64
SEL_RATIO = SEL_LEN // CMP_STRIDE
N_SEL = 16
N_LOCAL = 2
WINDOW = 512
D_CONV = D_MODEL // 2
CONV_WIDTH = 31
N_EXPERTS = 32
TOP_K = 4
D_EXPERT = D_MODEL
SWIGLU_LIMIT = 7.0
SWIGLU_ALPHA = 1.702
DN_ALPHA = 2.0 ** 0.25
LN_EPS = 1e-5
NEG_INF = -1e30
FORCE_SCORE = 1e9
OFF_KV = Q_WIDTH
OFF_GLU = OFF_KV + N_KV_SETS * KV_WIDTH
OFF_NSA_G = OFF_GLU + 2 * D_CONV
OFF_MERGE = OFF_NSA_G + N_HEADS * N_NSA_BRANCH

LANES = 128
GATE_PAD = LANES
CONV_HALO = 32
VMEM_LIMIT = 56 * 1024 * 1024
Q_TILE = 128
KV_TILE = 512
SLC_Q_TILE = 256
SEL_LANES = 256
HEAD_PAIRS = KV_WIDTH // LANES
ANY_ROWS = 8
FLAG_BITS = 32
FLAG_WORDS = SEQ // KV_TILE // FLAG_BITS
SLOPES = tuple(2.0 ** (-8.0 * (h + 1) / N_HEADS) for h in range(N_HEADS))

F32 = jnp.float32
BF16 = jnp.bfloat16


def _params(*sem):
    return pltpu.CompilerParams(dimension_semantics=sem, vmem_limit_bytes=VMEM_LIMIT)


def _dot(a, b):
    return jnp.dot(a, b, preferred_element_type=F32)


def _dot_nt(a, b):
    return lax.dot_general(a, b, (((1,), (1,)), ((), ())), preferred_element_type=F32)


def _dot_tn(a, b):
    return lax.dot_general(a, b, (((0,), (0,)), ((), ())), preferred_element_type=F32)


def _split(x):
    hi = x.astype(BF16)
    lo = (x - hi.astype(F32)).astype(BF16)
    return hi, lo


def _dot3(a, b, dot=_dot):
    ah, al = _split(a)
    bh, bl = _split(b)
    return dot(ah, bh) + (dot(ah, bl) + dot(al, bh))


def _sigmoid(x):
    return 1.0 / (1.0 + jnp.exp(-x))


def _layer_norm(x, g, b):
    mu = jnp.mean(x, axis=-1, keepdims=True)
    xc = x - mu
    var = jnp.mean(xc * xc, axis=-1, keepdims=True)
    return xc * lax.rsqrt(var + LN_EPS) * g + b


def _const_spec(shape):
    return pl.BlockSpec(shape, lambda *_: (0,) * len(shape))


def _iota(shape, axis):
    return lax.broadcasted_iota(jnp.int32, shape, axis)


def _ada_kernel(c_ref, w_ref, b_ref, o_ref):
    o_ref[...] = _dot3(c_ref[...], w_ref[...]) + b_ref[...]


def _adaln(c_all, w_ada, b_ada):
    n = c_all.shape[0]
    tn = 1536
    return pl.pallas_call(
        _ada_kernel,
        grid=(6 * D_MODEL // tn,),
        in_specs=[pl.BlockSpec((n, D_MODEL), lambda j: (0, 0)),
                  pl.BlockSpec((D_MODEL, tn), lambda j: (0, j)),
                  pl.BlockSpec((1, tn), lambda j: (0, j))],
        out_specs=pl.BlockSpec((n, tn), lambda j: (0, j)),
        out_shape=jax.ShapeDtypeStruct((n, 6 * D_MODEL), F32),
        compiler_params=_params("parallel"),
    )(c_all, w_ada, b_ada.reshape(1, -1))


def _proj_kernel(x_ref, sc_ref, sh_ref, wq_ref, wkv_ref, wglu_ref, wg_ref, wm_ref,
                 q_ref, kv_ref, win_ref, kvb_ref, u_ref, g_ref, m_ref):
    h = (x_ref[...] * (1.0 + sc_ref[...]) + sh_ref[...]).astype(BF16)
    q_ref[...] = _dot(h, wq_ref[...]).astype(BF16)
    kv = _dot(h, wkv_ref[...])
    kv_ref[...] = kv[:, :4 * KV_WIDTH]
    win_ref[...] = kv[:, 4 * KV_WIDTH:]
    kvb_ref[...] = kv.astype(BF16)
    glu = _dot(h, wglu_ref[...])
    u_ref[...] = glu[:, :D_CONV] * _sigmoid(glu[:, D_CONV:])
    g_ref[...] = _sigmoid(_dot(h, wg_ref[...]))
    m_ref[...] = _sigmoid(_dot(h, wm_ref[...]))


def _project(x, sc, sh, wts, tm):
    n = x.shape[0]
    per_row = sc.shape[0] != 1
    mod_rows = tm if per_row else 1

    def mod_spec(col):
        return pl.BlockSpec((mod_rows, D_MODEL), (lambda i: (i, col)) if per_row else (lambda i: (0, col)))

    widths = (Q_WIDTH, 4 * KV_WIDTH, 2 * KV_WIDTH, N_KV_SETS * KV_WIDTH, D_CONV, GATE_PAD, 2 * D_MODEL)
    dtypes = (BF16, F32, F32, BF16, F32, F32, F32)
    return pl.pallas_call(
        _proj_kernel,
        grid=(n // tm,),
        in_specs=[pl.BlockSpec((tm, D_MODEL), lambda i: (i, 0)), mod_spec(1), mod_spec(0)]
                 + [_const_spec(w.shape) for w in wts],
        out_specs=[pl.BlockSpec((tm, w), lambda i: (i, 0)) for w in widths],
        out_shape=[jax.ShapeDtypeStruct((n, w), d) for w, d in zip(widths, dtypes)],
        compiler_params=_params("parallel"),
    )(x, sc, sh, *wts)


def _conv_kernel(cur_ref, halo_ref, dw_ref, dwb_ref, lg_ref, lb_ref, wpw_ref, o_ref, ext_ref):
    bb, tm, _ = cur_ref.shape
    ext_ref[:, :CONV_HALO, :] = halo_ref[...]
    ext_ref[:, CONV_HALO:, :] = cur_ref[...]
    first = CONV_HALO - (CONV_WIDTH - 1)
    acc = jnp.zeros((bb, tm, D_CONV), F32)
    for j in range(CONV_WIDTH):
        acc = acc + ext_ref[:, first + j:first + j + tm, :] * dw_ref[j:j + 1, :]
    y = _layer_norm(acc + dwb_ref[...], lg_ref[...], lb_ref[...])
    y = (y * _sigmoid(y)).reshape(bb * tm, D_CONV).astype(BF16)
    o_ref[...] = _dot(y, wpw_ref[...]).reshape(bb, tm, D_MODEL)


def _conv_branch(cur, halo, dw_w, dw_b, ln_g, ln_b, w_pw, bb):
    b, tm, _ = cur.shape
    return pl.pallas_call(
        _conv_kernel,
        grid=(b // bb,),
        in_specs=[pl.BlockSpec((bb, tm, D_CONV), lambda i: (i, 0, 0)),
                  pl.BlockSpec((bb, CONV_HALO, D_CONV), lambda i: (i, 0, 0)),
                  _const_spec((CONV_WIDTH, D_CONV)), _const_spec((1, D_CONV)),
                  _const_spec((1, D_CONV)), _const_spec((1, D_CONV)),
                  _const_spec((D_CONV, D_MODEL))],
        out_specs=pl.BlockSpec((bb, tm, D_MODEL), lambda i: (i, 0, 0)),
        out_shape=jax.ShapeDtypeStruct((b, tm, D_MODEL), F32),
        scratch_shapes=[pltpu.VMEM((bb, CONV_HALO + tm, D_CONV), F32)],
        compiler_params=_params("parallel"),
    )(cur, halo, dw_w, dw_b.reshape(1, -1), ln_g.reshape(1, -1), ln_b.reshape(1, -1), w_pw)


MERGE_TILE = 512
MOE_TILE = 1024
MOE_CHUNK = 256


def _merge_kernel(x_ref, cy_ref, oc_ref, os_ref, ow_ref, g_ref, mg_ref, g1_ref, sc2_ref, sh2_ref,
                  exp_ref, wo_ref, lg_ref, lb_ref, wr_ref, br_ref,
                  x1_ref, h2_ref, wt_ref, rank_ref, cnt_ref):
    tm = x_ref.shape[0]
    gh, gl = _split(g_ref[...])
    nsa = jnp.zeros((tm, D_MODEL), F32)
    for n, o_ref in enumerate((oc_ref, os_ref, ow_ref)):
        e = exp_ref[n]
        nsa = nsa + o_ref[...] * (_dot(gh, e) + _dot(gl, e))
    mix = mg_ref[:, :D_MODEL] * cy_ref[...] + mg_ref[:, D_MODEL:] * nsa
    y = _dot(mix.astype(BF16), wo_ref[...])
    x1 = _layer_norm(DN_ALPHA * x_ref[...] + g1_ref[...] * y, lg_ref[...], lb_ref[...])
    x1_ref[...] = x1
    h2 = x1 * (1.0 + sc2_ref[...]) + sh2_ref[...]
    h2_ref[...] = h2.astype(BF16)
    logits = _dot3(wr_ref[...], h2, _dot_nt) + br_ref[...]
    eidx = _iota(logits.shape, 0)
    picked = jnp.zeros(logits.shape, F32)
    wsum = jnp.zeros((1, tm), F32)
    wts = jnp.zeros(logits.shape, F32)
    v0 = None
    for k in range(TOP_K):
        v = jnp.max(logits, axis=0, keepdims=True)
        first = jnp.min(jnp.where(logits == v, eidx, N_EXPERTS), axis=0, keepdims=True)
        hit = eidx == first
        if k == 0:
            v0 = v
        ev = jnp.exp(v - v0)
        wts = wts + jnp.where(hit, ev, 0.0)
        wsum = wsum + ev
        picked = picked + jnp.where(hit, 1.0, 0.0)
        logits = jnp.where(hit, -jnp.inf, logits)
    wt_ref[0] = wts / wsum
    upper = jnp.where(_iota((tm, tm), 0) < _iota((tm, tm), 1), 1.0, 0.0).astype(BF16)
    rank_ref[0] = _dot(picked.astype(BF16), upper)
    cnt = jnp.sum(picked, axis=1, keepdims=True)
    cnt_ref[0] = jnp.broadcast_to(cnt, (N_EXPERTS, LANES)).astype(jnp.int32)


def _merge(x, conv_y, o_cmp, o_slc, o_win, g_nsa, merge_g, mod, gate_expand, w_out, ln_g, ln_b,
           w_router_t, b_router):
    n = x.shape[0]
    tm = MERGE_TILE
    nt = n // tm
    per_row = mod.shape[0] != 1

    def row_spec(w):
        return pl.BlockSpec((tm, w), lambda i: (i, 0))

    def mod_spec(col):
        return pl.BlockSpec((tm if per_row else 1, D_MODEL),
                            (lambda i: (i, col)) if per_row else (lambda i: (0, col)))

    tile_spec = pl.BlockSpec((1, N_EXPERTS, tm), lambda i: (i, 0, 0))
    return pl.pallas_call(
        _merge_kernel,
        grid=(nt,),
        in_specs=[row_spec(D_MODEL)] * 5 + [row_spec(GATE_PAD), row_spec(2 * D_MODEL),
                  mod_spec(2), mod_spec(4), mod_spec(3),
                  _const_spec(gate_expand.shape), _const_spec(w_out.shape),
                  _const_spec((1, D_MODEL)), _const_spec((1, D_MODEL)),
                  _const_spec(w_router_t.shape), _const_spec((N_EXPERTS, 1))],
        out_specs=[row_spec(D_MODEL), row_spec(D_MODEL), tile_spec, tile_spec,
                   pl.BlockSpec((1, N_EXPERTS, LANES), lambda i: (i, 0, 0))],
        out_shape=[jax.ShapeDtypeStruct((n, D_MODEL), F32), jax.ShapeDtypeStruct((n, D_MODEL), BF16),
                   jax.ShapeDtypeStruct((nt, N_EXPERTS, tm), F32),
                   jax.ShapeDtypeStruct((nt, N_EXPERTS, tm), F32),
                   jax.ShapeDtypeStruct((nt, N_EXPERTS, LANES), jnp.int32)],
        compiler_params=_params("parallel"),
    )(x, conv_y, o_cmp, o_slc, o_win, g_nsa, merge_g, mod, mod, mod, gate_expand, w_out,
      ln_g.reshape(1, -1), ln_b.reshape(1, -1), w_router_t, b_router.reshape(-1, 1))


def _moe_kernel(cnt_ref, x1_ref, h2_ref, wt_ref, rank_ref, g2_ref, wup_ref, bup_ref, wdn_ref, bdn_ref,
                lg_ref, lb_ref, o_ref, acc_ref):
    i = pl.program_id(0)
    e = pl.program_id(1)
    tm = x1_ref.shape[0]

    @pl.when(e == 0)
    def _():
        acc_ref[...] = jnp.zeros_like(acc_ref)

    cnt = cnt_ref[i * N_EXPERTS + e]
    for ch in range(tm // MOE_CHUNK):
        @pl.when(cnt > ch * MOE_CHUNK)
        def _():
            w_e = wt_ref[0, pl.ds(e, 1), :]
            slot = rank_ref[0, pl.ds(e, 1), :] - float(ch * MOE_CHUNK)
            rows = _iota((MOE_CHUNK, tm), 0).astype(F32)
            onehot = jnp.where((rows == slot) & (w_e > 0.0), 1.0, 0.0)
            wc = jnp.sum(onehot * w_e, axis=1, keepdims=True)
            sel = onehot.astype(BF16)
            xc = _dot(sel, h2_ref[...]).astype(BF16)
            u = _dot(xc, wup_ref[0]) + bup_ref[0]
            x_glu = jnp.minimum(u[:, :D_EXPERT], SWIGLU_LIMIT)
            x_lin = jnp.clip(u[:, D_EXPERT:], -SWIGLU_LIMIT, SWIGLU_LIMIT)
            a = x_glu * _sigmoid(SWIGLU_ALPHA * x_glu) * (x_lin + 1.0)
            y = (_dot(a.astype(BF16), wdn_ref[0]) + bdn_ref[0]) * wc
            acc_ref[...] += _dot_tn(sel, y.astype(BF16))

    @pl.when(e == N_EXPERTS - 1)
    def _():
        o_ref[...] = _layer_norm(DN_ALPHA * x1_ref[...] + g2_ref[...] * acc_ref[...],
                                 lg_ref[...], lb_ref[...])


def _moe(x1, h2, wt, rank, cnt, mod, w_up, b_up, w_down, b_down, ln_g, ln_b):
    n = x1.shape[0]
    tm = MOE_TILE
    nt = n // tm
    per_row = mod.shape[0] != 1
    g2_spec = pl.BlockSpec((tm if per_row else 1, D_MODEL),
                           (lambda i, e, c: (i, 5)) if per_row else (lambda i, e, c: (0, 5)))
    tile_spec = pl.BlockSpec((1, N_EXPERTS, tm), lambda i, e, c: (i, 0, 0))
    grid_spec = pltpu.PrefetchScalarGridSpec(
        num_scalar_prefetch=1,
        grid=(nt, N_EXPERTS),
        in_specs=[pl.BlockSpec((tm, D_MODEL), lambda i, e, c: (i, 0)),
                  pl.BlockSpec((tm, D_MODEL), lambda i, e, c: (i, 0)),
                  tile_spec, tile_spec, g2_spec,
                  pl.BlockSpec((1, D_MODEL, 2 * D_EXPERT), lambda i, e, c: (e, 0, 0)),
                  pl.BlockSpec((1, 1, 2 * D_EXPERT), lambda i, e, c: (e, 0, 0)),
                  pl.BlockSpec((1, D_EXPERT, D_MODEL), lambda i, e, c: (e, 0, 0)),
                  pl.BlockSpec((1, 1, D_MODEL), lambda i, e, c: (e, 0, 0)),
                  pl.BlockSpec((1, D_MODEL), lambda i, e, c: (0, 0)),
                  pl.BlockSpec((1, D_MODEL), lambda i, e, c: (0, 0))],
        out_specs=pl.BlockSpec((tm, D_MODEL), lambda i, e, c: (i, 0)),
        scratch_shapes=[pltpu.VMEM((tm, D_MODEL), F32)],
    )
    return pl.pallas_call(
        _moe_kernel,
        grid_spec=grid_spec,
        out_shape=jax.ShapeDtypeStruct((n, D_MODEL), F32),
        compiler_params=_params("parallel", "arbitrary"),
    )(cnt, x1, h2, wt, rank, mod, w_up, b_up.reshape(N_EXPERTS, 1, -1), w_down,
      b_down.reshape(N_EXPERTS, 1, -1), ln_g.reshape(1, -1), ln_b.reshape(1, -1))


def _softmax_rows(s, valid):
    s = jnp.where(valid, s, NEG_INF)
    m = jnp.max(s, axis=-1, keepdims=True)
    e = jnp.exp(s - m)
    l = jnp.sum(e, axis=-1, keepdims=True)
    return e * jnp.where(m > 0.5 * NEG_INF, 1.0 / l, 0.0)


def _flash_update(s, valid, v, m_old, l_old, acc_old, pv=_dot):
    s = jnp.where(valid, s, NEG_INF)
    m_new = jnp.maximum(m_old, jnp.max(s, axis=-1, keepdims=True))
    p = jnp.where(valid, jnp.exp(s - m_new), 0.0)
    alpha = jnp.exp(m_old - m_new)
    l_new = alpha * l_old + jnp.sum(p, axis=-1, keepdims=True)
    acc_new = alpha * acc_old + pv(p.astype(BF16), v)
    return m_new, l_new, acc_new


def _shift_right_one(x):
    return jnp.where(_iota(x.shape, 1) == 0, 0.0, pltpu.roll(x, 1, 1))


def _block_scores(imp, nj):
    parts = [imp[:, k * nj:(k + 1) * nj] for k in range(SEL_RATIO)]
    return (parts[0] + parts[1]) + (parts[2] + parts[3]) + _shift_right_one(parts[3])


def _select_blocks(score, qblk, n_blocks):
    j = _iota(score.shape, 1)
    back = qblk - j
    forced = (j == 0) | ((back >= 0) & (back < N_LOCAL))
    score = jnp.where(forced, FORCE_SCORE, jnp.where(back >= 0, score, -1.0))
    score = jnp.where(j < n_blocks, score, -jnp.inf)
    picked = jnp.zeros(score.shape, F32)
    lane = j.astype(F32)
    for _ in range(N_SEL):
        m = jnp.max(score, axis=-1, keepdims=True)
        first = jnp.min(jnp.where(score == m, lane, float(SEL_LANES)), axis=-1, keepdims=True)
        hit = lane == first
        picked = jnp.where(hit, 1.0, picked)
        score = jnp.where(hit, -jnp.inf, score)
    return picked


def _gelu_tanh(x):
    return x * (0.5 * (1.0 + jnp.tanh(0.7978845608028654 * (x + 0.044715 * (x * x * x)))))


def _compress_rows(buf_ref, pe_ref, w1_ref, w2_ref, nj):
    outs = []
    for pair in range(HEAD_PAIRS):
        acc = jnp.zeros((SEL_RATIO * nj, 2 * PHI_HIDDEN), F32)
        for s in range(CMP_LEN):
            parts = [buf_ref[pair, pl.ds(CMP_STRIDE * k + s, nj, stride=SEL_LEN), :] for k in range(SEL_RATIO)]
            x = (jnp.concatenate(parts, axis=0) + pe_ref[s:s + 1, :]).astype(BF16)
            acc = acc + _dot(x, w1_ref[s])
        outs.append(_dot(_gelu_tanh(acc).astype(BF16), w2_ref[...]))
    return jnp.concatenate(outs, axis=1)


CMP_TILE = 4096


def _cmp_prompt_kernel(x_ref, halo_ref, pe_ref, w1_ref, w2_ref, o_ref, buf_ref):
    t = pl.program_id(0)
    last = t == pl.num_programs(0) - 1
    for pair in range(HEAD_PAIRS):
        cols = slice(pair * LANES, (pair + 1) * LANES)
        buf_ref[pair, :CMP_TILE, :] = x_ref[:, cols]
        buf_ref[pair, CMP_TILE:, :] = jnp.where(last, 0.0, halo_ref[:, cols])
    nj = CMP_TILE // SEL_LEN
    out = _compress_rows(buf_ref, pe_ref, w1_ref, w2_ref, nj)
    o_ref[...] = out.reshape(SEL_RATIO, nj, KV_WIDTH).astype(BF16)


def _compress_prompt(kv4, kv_set, pe, w1, w2):
    t_len = kv4.shape[0]
    nt = t_len // CMP_TILE
    nj = CMP_TILE // SEL_LEN
    halo_blocks = CMP_TILE // SEL_LEN
    last_halo = t_len // SEL_LEN - 1
    out = pl.pallas_call(
        _cmp_prompt_kernel,
        grid=(nt,),
        in_specs=[pl.BlockSpec((CMP_TILE, KV_WIDTH), lambda t: (t, kv_set)),
                  pl.BlockSpec((SEL_LEN, KV_WIDTH),
                               lambda t: (jnp.minimum((t + 1) * halo_blocks, last_halo), kv_set)),
                  _const_spec(pe.shape), _const_spec(w1.shape), _const_spec(w2.shape)],
        out_specs=pl.BlockSpec((SEL_RATIO, nj, KV_WIDTH), lambda t: (0, t, 0)),
        out_shape=jax.ShapeDtypeStruct((SEL_RATIO, t_len // SEL_LEN, KV_WIDTH), BF16),
        scratch_shapes=[pltpu.VMEM((HEAD_PAIRS, CMP_TILE + SEL_LEN, LANES), F32)],
        compiler_params=_params("arbitrary"),
    )(kv4, kv4, pe, w1, w2)
    return out.reshape(t_len // CMP_STRIDE, KV_WIDTH)


def _stack_heads(q_ref, g):
    return jnp.concatenate([q_ref[:, (g * GROUP + r) * HEAD_DIM:(g * GROUP + r + 1) * HEAD_DIM]
                            for r in range(GROUP)], axis=0)


def _head_bias(g, rel):
    return jnp.concatenate([jnp.broadcast_to(SLOPES[g * GROUP + r] * rel, (Q_TILE, rel.shape[1]))
                            for r in range(GROUP)], axis=0)


def _cmp_attn_prompt_kernel(q_ref, kc_ref, vc_ref, o_ref, sel_ref, any_ref):
    nc = kc_ref.shape[0]
    nj = nc // SEL_RATIO
    s0 = pl.program_id(0) * Q_TILE
    tq = s0 + _iota((Q_TILE, 1), 0)
    col = _iota((1, nc), 1)
    ends = (SEL_RATIO * (col % nj) + col // nj) * CMP_STRIDE + (CMP_LEN - 1)
    rel = (ends - s0).astype(F32)
    valid = jnp.concatenate([ends <= tq] * GROUP, axis=0)
    any_rows = []
    scores = []
    for g in range(N_KV_HEADS):
        kc = kc_ref[:, g * HEAD_DIM:(g + 1) * HEAD_DIM]
        vc = vc_ref[:, g * HEAD_DIM:(g + 1) * HEAD_DIM]
        p = _softmax_rows(_dot_nt(_stack_heads(q_ref, g), kc) + _head_bias(g, rel), valid)
        o = _dot(p.astype(BF16), vc)
        imp = jnp.zeros((Q_TILE, nc), F32)
        for r in range(GROUP):
            h = g * GROUP + r
            o_ref[:, h * HEAD_DIM:(h + 1) * HEAD_DIM] = o[r * Q_TILE:(r + 1) * Q_TILE]
            imp = imp + p[r * Q_TILE:(r + 1) * Q_TILE]
        scores.append(_block_scores(imp, nj))
    picked = _select_blocks(jnp.concatenate(scores, axis=0),
                            jnp.concatenate([tq // SEL_LEN] * N_KV_HEADS, axis=0), nj)
    for g in range(N_KV_HEADS):
        sel_ref[g] = picked[g * Q_TILE:(g + 1) * Q_TILE].astype(BF16)
        any_rows.append(jnp.max(picked[g * Q_TILE:(g + 1) * Q_TILE], axis=0, keepdims=True))
    any_ref[0] = jnp.concatenate(any_rows + [jnp.zeros((ANY_ROWS - N_KV_HEADS, SEL_LANES), F32)], axis=0)


def _cmp_attn_prompt(q, kc, vc):
    t_len = q.shape[0]
    nt = t_len // Q_TILE
    return pl.pallas_call(
        _cmp_attn_prompt_kernel,
        grid=(nt,),
        in_specs=[pl.BlockSpec((Q_TILE, Q_WIDTH), lambda i: (i, 0)),
                  _const_spec(kc.shape), _const_spec(vc.shape)],
        out_specs=[pl.BlockSpec((Q_TILE, Q_WIDTH), lambda i: (i, 0)),
                   pl.BlockSpec((N_KV_HEADS, Q_TILE, SEL_LANES), lambda i: (0, i, 0)),
                   pl.BlockSpec((1, ANY_ROWS, SEL_LANES), lambda i: (i, 0, 0))],
        out_shape=[jax.ShapeDtypeStruct((t_len, Q_WIDTH), F32),
                   jax.ShapeDtypeStruct((N_KV_HEADS, t_len, SEL_LANES), BF16),
                   jax.ShapeDtypeStruct((nt, ANY_ROWS, SEL_LANES), F32)],
        compiler_params=_params("parallel"),
    )(q, kc, vc)


def _tile_flags(any_sel):
    nt = any_sel.shape[0]
    per_tile = KV_TILE // SEL_LEN
    hit = any_sel[:, :N_KV_HEADS].reshape(nt, N_KV_HEADS, SEL_LANES // per_tile, per_tile).max(-1) > 0.0
    bits = hit.reshape(nt, N_KV_HEADS, -1, FLAG_BITS).astype(jnp.uint32) << jnp.arange(FLAG_BITS, dtype=jnp.uint32)
    words = lax.bitcast_convert_type(bits.sum(-1, dtype=jnp.uint32), jnp.int32)
    per = SLC_Q_TILE // Q_TILE
    words = words.reshape(nt // per, per, N_KV_HEADS, -1)
    joined = words[:, 0]
    for k in range(1, per):
        joined = joined | words[:, k]
    return joined.reshape(-1)


def _slc_attn_prompt_kernel(flag_ref, q_ref, k_ref, v_ref, sel_ref, o_ref, qs_ref, m_ref, l_ref, acc_ref):
    i = pl.program_id(0)
    tq = i * SLC_Q_TILE + _iota((SLC_Q_TILE, 1), 0)
    for h in range(N_HEADS):
        qs_ref[h // GROUP, (h % GROUP) * SLC_Q_TILE:(h % GROUP + 1) * SLC_Q_TILE, :] = q_ref[:, h * HEAD_DIM:(h + 1) * HEAD_DIM]
    m_ref[...] = jnp.full(m_ref.shape, NEG_INF, F32)
    l_ref[...] = jnp.zeros(l_ref.shape, F32)
    acc_ref[...] = jnp.zeros(acc_ref.shape, F32)

    def tile(t, carry):
        base = pl.multiple_of(t * KV_TILE, KV_TILE)
        causal = tq - (base + _iota((1, KV_TILE), 1)) >= 0
        rel = (base - i * SLC_Q_TILE + _iota((1, KV_TILE), 1)).astype(F32)
        expand = jnp.where(_iota((SEL_LANES, KV_TILE), 0)
                           == t * (KV_TILE // SEL_LEN) + _iota((SEL_LANES, KV_TILE), 1) // SEL_LEN,
                           1.0, 0.0).astype(BF16)
        for g in range(N_KV_HEADS):
            word = flag_ref[(i * N_KV_HEADS + g) * FLAG_WORDS + t // FLAG_BITS]

            @pl.when(((word >> (t % FLAG_BITS)) & 1) == 1)
            def _():
                valid = (_dot(sel_ref[g], expand) > 0.5) & causal
                kt = k_ref[pl.ds(base, KV_TILE), g * HEAD_DIM:(g + 1) * HEAD_DIM]
                vt = v_ref[pl.ds(base, KV_TILE), g * HEAD_DIM:(g + 1) * HEAD_DIM]
                raw = _dot_nt(qs_ref[g], kt)
                s = jnp.concatenate(
                    [jnp.where(valid, raw[r * SLC_Q_TILE:(r + 1) * SLC_Q_TILE] + SLOPES[g * GROUP + r] * rel, NEG_INF)
                     for r in range(GROUP)], axis=0)
                m_old = m_ref[g]
                m_new = jnp.maximum(m_old, jnp.max(s, axis=-1, keepdims=True))
                p = jnp.exp(s - jnp.concatenate([m_new] * (KV_TILE // LANES), axis=1))
                alpha = jnp.exp(m_old - m_new)
                l_ref[g] = alpha * l_ref[g] + jnp.sum(p, axis=-1, keepdims=True)
                acc_ref[g] = alpha[:, :HEAD_DIM] * acc_ref[g] + _dot(p.astype(BF16), vt)
                m_ref[g] = m_new
        return carry

    lax.fori_loop(0, (i * SLC_Q_TILE) // KV_TILE + 1, tile, 0)
    for h in range(N_HEADS):
        rows = slice((h % GROUP) * SLC_Q_TILE, (h % GROUP + 1) * SLC_Q_TILE)
        l = l_ref[h // GROUP, rows, :HEAD_DIM]
        o_ref[:, h * HEAD_DIM:(h + 1) * HEAD_DIM] = acc_ref[h // GROUP, rows, :] / jnp.where(l > 0.0, l, 1.0)


def _slc_attn_prompt(q, kvb, sel, flags):
    t_len = q.shape[0]
    resident = functools.partial(pl.BlockSpec, pipeline_mode=pl.Buffered(1))
    rows = GROUP * SLC_Q_TILE
    grid_spec = pltpu.PrefetchScalarGridSpec(
        num_scalar_prefetch=1,
        grid=(t_len // SLC_Q_TILE,),
        in_specs=[pl.BlockSpec((SLC_Q_TILE, Q_WIDTH), lambda i, f: (i, 0)),
                  resident((t_len, KV_WIDTH), lambda i, f: (0, 2)),
                  resident((t_len, KV_WIDTH), lambda i, f: (0, 3)),
                  pl.BlockSpec((N_KV_HEADS, SLC_Q_TILE, SEL_LANES), lambda i, f: (0, i, 0))],
        out_specs=pl.BlockSpec((SLC_Q_TILE, Q_WIDTH), lambda i, f: (i, 0)),
        scratch_shapes=[pltpu.VMEM((N_KV_HEADS, rows, HEAD_DIM), BF16),
                        pltpu.VMEM((N_KV_HEADS, rows, LANES), F32), pltpu.VMEM((N_KV_HEADS, rows, LANES), F32),
                        pltpu.VMEM((N_KV_HEADS, rows, HEAD_DIM), F32)],
    )
    return pl.pallas_call(
        _slc_attn_prompt_kernel,
        grid_spec=grid_spec,
        out_shape=jax.ShapeDtypeStruct((t_len, Q_WIDTH), F32),
        compiler_params=_params("parallel"),
    )(flags, q, kvb, kvb, sel)


def _win_attn_prompt_kernel(q_ref, k_ref, v_ref, o_ref):
    s0 = pl.multiple_of(pl.program_id(0) * Q_TILE, Q_TILE)
    n_keys = WINDOW + Q_TILE
    tq = s0 + _iota((Q_TILE, 1), 0)
    pw = s0 - WINDOW + _iota((1, n_keys), 1)
    dw = tq - pw
    valid = (dw >= 0) & (dw < WINDOW) & (pw >= 0)
    valid = jnp.concatenate([valid] * GROUP, axis=0)
    rel = (_iota((1, n_keys), 1) - WINDOW).astype(F32)
    for g in range(N_KV_HEADS):
        kw = k_ref[pl.ds(s0, n_keys), g * HEAD_DIM:(g + 1) * HEAD_DIM]
        vw = v_ref[pl.ds(s0, n_keys), g * HEAD_DIM:(g + 1) * HEAD_DIM]
        p = _softmax_rows(_dot_nt(_stack_heads(q_ref, g), kw) + _head_bias(g, rel), valid)
        o = _dot(p.astype(BF16), vw)
        for r in range(GROUP):
            h = g * GROUP + r
            o_ref[:, h * HEAD_DIM:(h + 1) * HEAD_DIM] = o[r * Q_TILE:(r + 1) * Q_TILE]


def _win_attn_prompt(q, kw_pad, vw_pad):
    t_len = q.shape[0]
    resident = functools.partial(pl.BlockSpec, pipeline_mode=pl.Buffered(1))
    return pl.pallas_call(
        _win_attn_prompt_kernel,
        grid=(t_len // Q_TILE,),
        in_specs=[pl.BlockSpec((Q_TILE, Q_WIDTH), lambda i: (i, 0)),
                  resident(kw_pad.shape, lambda i: (0, 0)), resident(vw_pad.shape, lambda i: (0, 0))],
        out_specs=pl.BlockSpec((Q_TILE, Q_WIDTH), lambda i: (i, 0)),
        out_shape=jax.ShapeDtypeStruct((t_len, Q_WIDTH), F32),
        compiler_params=_params("parallel"),
    )(q, kw_pad, vw_pad)


S_ROWS = N_HEADS * DEC_SEQ
S_TAIL = SEL_LEN
S_CMP_BLOCKS = PAST_LEN // CMP_STRIDE
S_CHUNK = 2048


def _fetch_pages(pt_ref, sem_ref, page_copy, meanwhile=None):
    b = pl.program_id(0)
    slot = b % 2

    def copy(seq, p, dst_slot):
        return page_copy(pt_ref[seq * N_PAGES + p], p, dst_slot, sem_ref.at[dst_slot])

    def start(seq, dst_slot):
        def body(p, c):
            copy(seq, p, dst_slot).start()
            return c
        lax.fori_loop(0, N_PAGES, body, 0)

    @pl.when(b == 0)
    def _():
        start(b, slot)

    @pl.when(b + 1 < pl.num_programs(0))
    def _():
        start(b + 1, 1 - slot)

    done = meanwhile() if meanwhile is not None else None

    def wait(p, c):
        copy(b, p, slot).wait()
        return c
    lax.fori_loop(0, N_PAGES, wait, 0)
    return slot, done


PAGES_PER_STEP = 8
TAPS_PER_DOT = 8
S_CHUNKS = PAST_LEN // CMP_STRIDE + ANY_ROWS


def _cmp_sample_kernel(kv_set, pt_ref, cache_ref, new_ref, pet_ref, w1_ref, w2_ref, o_ref,
                       stage_ref, buf_ref, sem_ref):
    def page_copy(page, p, slot, sem):
        return pltpu.make_async_copy(cache_ref.at[page, kv_set],
                                     stage_ref.at[slot, pl.ds(p * KV_WIDTH, KV_WIDTH), :], sem)

    slot, _ = _fetch_pages(pt_ref, sem_ref, page_copy)
    per_page = PAGE_SIZE // CMP_STRIDE
    regroup = jnp.where((_iota((PAGE_SIZE, PAGE_SIZE), 1) % CMP_STRIDE) * per_page
                        + _iota((PAGE_SIZE, PAGE_SIZE), 1) // CMP_STRIDE == _iota((PAGE_SIZE, PAGE_SIZE), 0),
                        1.0, 0.0).astype(BF16)

    def to_rows(i, c):
        for u in range(PAGES_PER_STEP):
            p = i * PAGES_PER_STEP + u
            src = pl.multiple_of(p * KV_WIDTH, KV_WIDTH)
            dst = pl.multiple_of(p * per_page, per_page)
            x_t = stage_ref[slot, pl.ds(src, KV_WIDTH), :].astype(BF16)
            rows = _dot_nt(regroup, x_t)
            for pair in range(HEAD_PAIRS):
                for s in range(CMP_STRIDE):
                    buf_ref[pair, s, pl.ds(dst, per_page), :] = rows[s * per_page:(s + 1) * per_page,
                                                                     pair * LANES:(pair + 1) * LANES]
        return c
    lax.fori_loop(0, N_PAGES // PAGES_PER_STEP, to_rows, 0)
    first_row = _iota((ANY_ROWS, LANES), 0) == 0
    for pair in range(HEAD_PAIRS):
        for s in range(CMP_STRIDE):
            tail = jnp.zeros((ANY_ROWS, LANES), F32)
            if s < DEC_SEQ:
                row = new_ref[s:s + 1, pair * LANES:(pair + 1) * LANES].astype(BF16).astype(F32)
                tail = jnp.where(first_row, row, 0.0)
            buf_ref[pair, s, PAST_LEN // CMP_STRIDE:, :] = tail

    bias = jnp.zeros((1, 2 * PHI_HIDDEN), F32)
    for s in range(CMP_LEN):
        bias = bias + jnp.sum(pet_ref[:, s:s + 1] * w1_ref[s].astype(F32), axis=0, keepdims=True)
    nj = S_CMP_BLOCKS // SEL_RATIO
    outs = []
    for pair in range(HEAD_PAIRS):
        acc = jnp.zeros((SEL_RATIO * nj, 2 * PHI_HIDDEN), F32)
        for s0 in range(0, CMP_LEN, TAPS_PER_DOT):
            taps = [jnp.concatenate(
                [buf_ref[pair, s % CMP_STRIDE, pl.ds(k + s // CMP_STRIDE, nj, stride=SEL_RATIO), :]
                 for k in range(SEL_RATIO)], axis=0).astype(BF16) for s in range(s0, s0 + TAPS_PER_DOT)]
            w = w1_ref[s0:s0 + TAPS_PER_DOT].reshape(TAPS_PER_DOT * LANES, 2 * PHI_HIDDEN)
            acc = acc + _dot(jnp.concatenate(taps, axis=1), w)
        outs.append(_dot(_gelu_tanh(acc + bias).astype(BF16), w2_ref[...]))
    o_ref[0] = jnp.concatenate(outs, axis=1).astype(BF16)


def _compress_sample(page_table, cache, kv4_new, kv_set, pe, w1, w2):
    pe_t = pe.T
    grid_spec = pltpu.PrefetchScalarGridSpec(
        num_scalar_prefetch=1,
        grid=(DEC_BATCH,),
        in_specs=[pl.BlockSpec(memory_space=pl.ANY),
                  pl.BlockSpec((DEC_SEQ, KV_WIDTH), lambda b, pt: (b, kv_set)),
                  pl.BlockSpec(pe_t.shape, lambda b, pt: (0, 0)),
                  pl.BlockSpec(w1.shape, lambda b, pt: (0, 0, 0)),
                  pl.BlockSpec(w2.shape, lambda b, pt: (0, 0))],
        out_specs=pl.BlockSpec((1, S_CMP_BLOCKS, KV_WIDTH), lambda b, pt: (b, 0, 0)),
        scratch_shapes=[pltpu.VMEM((2, N_PAGES * KV_WIDTH, PAGE_SIZE), F32),
                        pltpu.VMEM((HEAD_PAIRS, CMP_STRIDE, S_CHUNKS, LANES), F32),
                        pltpu.SemaphoreType.DMA((2,))],
    )
    return pl.pallas_call(
        functools.partial(_cmp_sample_kernel, kv_set),
        grid_spec=grid_spec,
        out_shape=jax.ShapeDtypeStruct((DEC_BATCH, S_CMP_BLOCKS, KV_WIDTH), BF16),
        compiler_params=_params("arbitrary"),
    )(page_table.reshape(-1), cache, kv4_new, pe_t, w1, w2)


def _row_queries():
    return PAST_LEN + _iota((S_ROWS, 1), 0) % DEC_SEQ


def _cmp_attend_sample(q, slope, kc, vc):
    nc = S_CMP_BLOCKS
    nj = nc // SEL_RATIO
    tq = _row_queries()
    col = _iota((1, nc), 1)
    ends = (SEL_RATIO * (col % nj) + col // nj) * CMP_STRIDE + (CMP_LEN - 1)
    s = _dot_nt(q, kc) - slope * (tq - ends).astype(F32)
    p = _softmax_rows(s, ends <= tq)
    out = _dot(p.astype(BF16), vc)
    rows_g = GROUP * DEC_SEQ
    n_blocks = (PAST_LEN + DEC_SEQ + SEL_LEN - 1) // SEL_LEN
    qblk = (PAST_LEN + _iota((DEC_SEQ, 1), 0)) // SEL_LEN
    scores = []
    for g in range(N_KV_HEADS):
        imp = jnp.zeros((DEC_SEQ, nc), F32)
        for r in range(GROUP):
            imp = imp + p[g * rows_g + r * DEC_SEQ:g * rows_g + (r + 1) * DEC_SEQ]
        scores.append(jnp.concatenate([_block_scores(imp, nj), jnp.zeros((DEC_SEQ, SEL_LANES - nj), F32)], axis=1))
    picked = _select_blocks(jnp.concatenate(scores, axis=0), jnp.concatenate([qblk] * N_KV_HEADS, axis=0), n_blocks)
    sel = jnp.concatenate([picked[g * DEC_SEQ:(g + 1) * DEC_SEQ] for g in range(N_KV_HEADS)
                           for _ in range(GROUP)], axis=0).astype(BF16)
    return out, sel


def _pad_new_rows(x):
    return jnp.concatenate([x, jnp.zeros((LANES - DEC_SEQ, x.shape[1]), F32)], axis=0).astype(BF16)


def _attn_sample_kernel(pt_ref, cache_ref, q_ref, slope_ref, kc_ref, vc_ref, new_ref, cwin_ref, wnew_ref,
                        oc_ref, os_ref, ow_ref, buf_ref, sem_ref):
    def page_copy(page, p, slot, sem):
        return pltpu.make_async_copy(cache_ref.at[page, pl.ds(2, 2)],
                                     buf_ref.at[slot, :, :, pl.ds(p * PAGE_SIZE, PAGE_SIZE)], sem)

    q = q_ref[0]
    slope = slope_ref[...]
    slot, (o_cmp, sel) = _fetch_pages(pt_ref, sem_ref, page_copy,
                                      lambda: _cmp_attend_sample(q, slope, kc_ref[0], vc_ref[0]))
    oc_ref[0] = o_cmp
    tq = _row_queries()
    new_pos = PAST_LEN + _iota((1, LANES), 1)
    new_ok = (tq - new_pos >= 0) & (new_pos < PAST_LEN + DEC_SEQ)
    new_dist = (tq - new_pos).astype(F32)

    m = jnp.full((S_ROWS, 1), NEG_INF, F32)
    l = jnp.zeros((S_ROWS, 1), F32)
    acc = jnp.zeros((S_ROWS, KV_WIDTH), F32)
    for c in range(PAST_LEN // S_CHUNK):
        k_t = buf_ref[slot, 0, :, c * S_CHUNK:(c + 1) * S_CHUNK].astype(BF16)
        v_t = buf_ref[slot, 1, :, c * S_CHUNK:(c + 1) * S_CHUNK].astype(BF16)
        pos = c * S_CHUNK + _iota((1, S_CHUNK), 1)
        expand = jnp.where(_iota((SEL_LANES, S_CHUNK), 0)
                           == (c * S_CHUNK + _iota((SEL_LANES, S_CHUNK), 1)) // SEL_LEN, 1.0, 0.0).astype(BF16)
        valid = (_dot(sel, expand) > 0.5) & (tq - pos >= 0)
        s = _dot(q, k_t) - slope * (tq - pos).astype(F32)
        m, l, acc = _flash_update(s, valid, v_t, m, l, acc, pv=_dot_nt)
    new = _pad_new_rows(new_ref[...])
    in_last = jnp.sum(jnp.where(_iota((S_ROWS, SEL_LANES), 1) == PAST_LEN // SEL_LEN, sel.astype(F32), 0.0),
                      axis=1, keepdims=True) > 0.5
    s = _dot_nt(q, new[:, :KV_WIDTH]) - slope * new_dist
    m, l, acc = _flash_update(s, in_last & new_ok, new[:, KV_WIDTH:], m, l, acc)
    os_ref[0] = acc / jnp.where(l > 0.0, l, 1.0)

    wb = cwin_ref.shape[-1]
    pw = PAST_LEN - wb + _iota((1, wb), 1)
    dw = tq - pw
    s_past = _dot(q, cwin_ref[0, 0].astype(BF16)) - slope * dw.astype(F32)
    wnew = _pad_new_rows(wnew_ref[...])
    s_new = _dot_nt(q, wnew[:, :KV_WIDTH]) - slope * new_dist
    valid = jnp.concatenate([(dw >= 0) & (dw < WINDOW) & (pw >= 0), new_ok & (tq - new_pos < WINDOW)], axis=1)
    p = _softmax_rows(jnp.concatenate([s_past, s_new], axis=1), valid).astype(BF16)
    ow_ref[0] = _dot_nt(p[:, :wb], cwin_ref[0, 1].astype(BF16)) + _dot(p[:, wb:], wnew[:, KV_WIDTH:])


def _attn_sample(page_table, cache, qbd, slope_col, kc, vc, kv4_new, cache_win, win_new):
    wb = cache_win.shape[-1]
    seq_spec = lambda rows, w: pl.BlockSpec((1, rows, w), lambda b, pt: (b, 0, 0))
    grid_spec = pltpu.PrefetchScalarGridSpec(
        num_scalar_prefetch=1,
        grid=(DEC_BATCH,),
        in_specs=[pl.BlockSpec(memory_space=pl.ANY),
                  seq_spec(S_ROWS, KV_WIDTH),
                  pl.BlockSpec((S_ROWS, 1), lambda b, pt: (0, 0)),
                  seq_spec(S_CMP_BLOCKS, KV_WIDTH), seq_spec(S_CMP_BLOCKS, KV_WIDTH),
                  pl.BlockSpec((DEC_SEQ, 2 * KV_WIDTH), lambda b, pt: (b, 1)),
                  pl.BlockSpec((1, 2, KV_WIDTH, wb), lambda b, pt: (b, 0, 0, 0)),
                  pl.BlockSpec((DEC_SEQ, 2 * KV_WIDTH), lambda b, pt: (b, 0))],
        out_specs=[seq_spec(S_ROWS, KV_WIDTH)] * 3,
        scratch_shapes=[pltpu.VMEM((2, 2, KV_WIDTH, PAST_LEN), F32), pltpu.SemaphoreType.DMA((2,))],
    )
    return pl.pallas_call(
        _attn_sample_kernel,
        grid_spec=grid_spec,
        out_shape=[jax.ShapeDtypeStruct((DEC_BATCH, S_ROWS, KV_WIDTH), F32)] * 3,
        compiler_params=_params("arbitrary"),
    )(page_table.reshape(-1), cache, qbd, slope_col, kc, vc, kv4_new, cache_win, win_new)


def _gate_expand():
    rows = jnp.arange(GATE_PAD)[:, None]
    cols = jnp.arange(D_MODEL)[None, :]
    return jnp.stack([(rows == (cols // HEAD_DIM) * N_NSA_BRANCH + n) for n in range(N_NSA_BRANCH)]
                     ).astype(BF16)


def _compress_weights(pe, w1, w2):
    per_tile = LANES // HEAD_DIM
    eye = jnp.eye(per_tile, dtype=F32)
    w1_big = jnp.einsum('sde,gh->sgdhe', w1, eye).reshape(CMP_LEN, LANES, per_tile * PHI_HIDDEN).astype(BF16)
    w2_big = jnp.einsum('ed,gh->gehd', w2, eye).reshape(per_tile * PHI_HIDDEN, LANES).astype(BF16)
    return jnp.tile(pe, (1, per_tile)), w1_big, w2_big


def _rows_from_heads(o):
    o = o.reshape(DEC_BATCH, N_KV_HEADS, GROUP, DEC_SEQ, N_KV_HEADS, HEAD_DIM)
    idx = jnp.arange(N_KV_HEADS)
    own = o[:, idx, :, :, idx]
    return own.transpose(1, 3, 0, 2, 4).reshape(DEC_BATCH * DEC_SEQ, Q_WIDTH)


def kernel(x_prompt, x_sample, c_prompt, c_sample, cache_kv, cache_win, state_conv, page_table, w_ada, b_ada, w_in, conv_dw_w, conv_dw_b, conv_ln_g, conv_ln_b, w_conv_out, cmp_pe_k, cmp_w1_k, cmp_w2_k, cmp_pe_v, cmp_w1_v, cmp_w2_v, w_out, ln1_g, ln1_b, w_router, b_router, w_up, b_up, w_down, b_down, ln2_g, ln2_b):
    l = 0
    n_p = SEQ
    n_s = DEC_BATCH * DEC_SEQ

    c_all = jnp.concatenate([c_prompt, c_sample, jnp.zeros((7, D_MODEL), F32)], axis=0)
    mod = _adaln(c_all, w_ada[l], b_ada[l])
    mod_p = mod[0:1]
    mod_s = jnp.repeat(mod[1:1 + DEC_BATCH], DEC_SEQ, axis=0)

    wi = w_in[l]
    wts = ((wi[:, :OFF_KV] * (HEAD_DIM ** -0.5)).astype(BF16),
           wi[:, OFF_KV:OFF_GLU].astype(BF16),
           wi[:, OFF_GLU:OFF_NSA_G].astype(BF16),
           jnp.pad(wi[:, OFF_NSA_G:OFF_MERGE], ((0, 0), (0, GATE_PAD - N_HEADS * N_NSA_BRANCH))).astype(BF16),
           wi[:, OFF_MERGE:].astype(BF16))
    gate_expand = _gate_expand()
    w_out_b = w_out[l].astype(BF16)
    w_pw_b = w_conv_out[l].astype(BF16)
    w_up_b = w_up[l].astype(BF16)
    w_down_b = w_down[l].astype(BF16)
    w_router_t = w_router[l].T
    cmp_k = _compress_weights(cmp_pe_k[l], cmp_w1_k[l], cmp_w2_k[l])
    cmp_v = _compress_weights(cmp_pe_v[l], cmp_w1_v[l], cmp_w2_v[l])
    conv_w = (conv_dw_w[l], conv_dw_b[l], conv_ln_g[l], conv_ln_b[l], w_pw_b)

    def tail(x, conv_y, o3, g_nsa, merge_g, m):
        x1, h2, wt, rank, cnt = _merge(x, conv_y, *o3, g_nsa, merge_g, m, gate_expand, w_out_b,
                                       ln1_g[l], ln1_b[l], w_router_t, b_router[l])
        per = MOE_TILE // MERGE_TILE
        cnt = cnt[:, :, 0].reshape(-1, per, N_EXPERTS)
        before = jnp.cumsum(cnt, axis=1) - cnt
        rank = rank.reshape(-1, per, N_EXPERTS, MERGE_TILE) + before[..., None].astype(F32)
        rank = rank.transpose(0, 2, 1, 3).reshape(-1, N_EXPERTS, MOE_TILE)
        wt = wt.reshape(-1, per, N_EXPERTS, MERGE_TILE).transpose(0, 2, 1, 3).reshape(-1, N_EXPERTS, MOE_TILE)
        return _moe(x1, h2, wt, rank, cnt.sum(1).reshape(-1), m, w_up_b, b_up[l], w_down_b, b_down[l],
                    ln2_g[l], ln2_b[l])

    xp = x_prompt.reshape(n_p, D_MODEL)
    q, kv4, win2, kvb, u, g_nsa, merge_g = _project(xp, mod_p, mod_p, wts, 256)
    ct = 512
    u3 = u.reshape(n_p // ct, ct, D_CONV)
    halo = jnp.concatenate([jnp.zeros((1, CONV_HALO, D_CONV), F32), u3[:-1, ct - CONV_HALO:]], axis=0)
    conv_y = _conv_branch(u3, halo, *conv_w, 1).reshape(n_p, D_MODEL)
    kc = _compress_prompt(kv4, 0, *cmp_k)
    vc = _compress_prompt(kv4, 1, *cmp_v)
    o_cmp, sel, any_sel = _cmp_attn_prompt(q, kc, vc)
    o_slc = _slc_attn_prompt(q, kvb, sel, _tile_flags(any_sel))
    kw_pad = jnp.pad(kvb[:, 4 * KV_WIDTH:5 * KV_WIDTH], ((WINDOW, 0), (0, 0)))
    vw_pad = jnp.pad(kvb[:, 5 * KV_WIDTH:], ((WINDOW, 0), (0, 0)))
    o_win = _win_attn_prompt(q, kw_pad, vw_pad)
    y_p = tail(xp, conv_y, (o_cmp, o_slc, o_win), g_nsa, merge_g, mod_p)
    out_kv_p = kv4.reshape(1, 1, n_p, 4, N_KV_HEADS, HEAD_DIM)
    out_win_p = win2[n_p - WINDOW:].reshape(1, 1, WINDOW, 2, N_KV_HEADS, HEAD_DIM)
    out_conv_p = u[n_p - (CONV_WIDTH - 1):].reshape(1, 1, CONV_WIDTH - 1, D_CONV)

    xs = x_sample.reshape(n_s, D_MODEL)
    q, kv4, win2, kvb, u, g_nsa, merge_g = _project(xs, mod_s, mod_s, wts, 256)
    u3 = u.reshape(DEC_BATCH, DEC_SEQ, D_CONV)
    st = state_conv[l]
    halo = jnp.concatenate([jnp.zeros((DEC_BATCH, CONV_HALO - (CONV_WIDTH - 1), D_CONV), F32), st], axis=1)
    conv_y = _conv_branch(u3, halo, *conv_w, DEC_BATCH).reshape(n_s, D_MODEL)
    qh = q.reshape(DEC_BATCH, DEC_SEQ, N_KV_HEADS, GROUP, HEAD_DIM).transpose(0, 2, 3, 1, 4)
    qbd = jnp.einsum('bgrqd,gh->bgrqhd', qh, jnp.eye(N_KV_HEADS, dtype=BF16)).reshape(DEC_BATCH, S_ROWS, KV_WIDTH)
    slope_col = jnp.repeat(jnp.asarray(SLOPES, F32), DEC_SEQ).reshape(S_ROWS, 1)
    cache = cache_kv[l].transpose(0, 2, 3, 4, 1).reshape(-1, 4, KV_WIDTH, PAGE_SIZE)
    cwin = cache_win[l].transpose(0, 2, 3, 4, 1).reshape(DEC_BATCH, 2, KV_WIDTH, -1)
    kc = _compress_sample(page_table, cache, kv4, 0, *cmp_k)
    vc = _compress_sample(page_table, cache, kv4, 1, *cmp_v)
    o_cmp, o_slc, o_win = _attn_sample(page_table, cache, qbd, slope_col, kc, vc, kv4, cwin, win2)
    o3 = tuple(_rows_from_heads(o) for o in (o_cmp, o_slc, o_win))
    y_s = tail(xs, conv_y, o3, g_nsa, merge_g, mod_s)
    out_kv_s = kv4.reshape(1, DEC_BATCH, DEC_SEQ, 4, N_KV_HEADS, HEAD_DIM)
    win_new = win2.reshape(DEC_BATCH, DEC_SEQ, 2, N_KV_HEADS, HEAD_DIM)
    out_win_s = jnp.concatenate([cache_win[l], win_new], axis=1)[:, DEC_SEQ:][None]
    out_conv_s = jnp.concatenate([st, u3], axis=1)[:, DEC_SEQ:][None]

    return (y_p.reshape(1, n_p, D_MODEL), y_s.reshape(DEC_BATCH, DEC_SEQ, D_MODEL),
            out_kv_p, out_kv_s, out_win_p, out_win_s, out_conv_p, out_conv_s)
```

```python
import functools

import jax
import jax.numpy as jnp
from jax import lax
from jax.experimental import pallas as pl
from jax.experimental.pallas import tpu as pltpu

D_MODEL = 1024
SEQ = 16384
DEC_BATCH = 128
DEC_SEQ = 8
PAST_LEN = 8192
PAGE_SIZE = 128
N_PAGES = PAST_LEN // PAGE_SIZE
N_HEADS = 16
HEAD_DIM = 64
N_KV_HEADS = 4
GROUP = N_HEADS // N_KV_HEADS
KV_WIDTH = N_KV_HEADS * HEAD_DIM
Q_WIDTH = N_HEADS * HEAD_DIM
N_KV_SETS = 6
N_NSA_BRANCH = 3
CMP_LEN = 32
CMP_STRIDE = 16
PHI_HIDDEN = 2 * HEAD_DIM
SEL_LEN = 64
SEL_RATIO = SEL_LEN // CMP_STRIDE
N_SEL = 16
N_LOCAL = 2
WINDOW = 512
D_CONV = D_MODEL // 2
CONV_WIDTH = 31
N_EXPERTS = 32
TOP_K = 4
D_EXPERT = D_MODEL
SWIGLU_LIMIT = 7.0
SWIGLU_ALPHA = 1.702
DN_ALPHA = 2.0 ** 0.25
LN_EPS = 1e-5
NEG_INF = -1e30
FORCE_SCORE = 1e9
OFF_KV = Q_WIDTH
OFF_GLU = OFF_KV + N_KV_SETS * KV_WIDTH
OFF_NSA_G = OFF_GLU + 2 * D_CONV
OFF_MERGE = OFF_NSA_G + N_HEADS * N_NSA_BRANCH

LANES = 128
GATE_PAD = LANES
CONV_HALO = 32
VMEM_LIMIT = 56 * 1024 * 1024
Q_TILE = 128
KV_TILE = 512
SLC_Q_TILE = 256
SEL_LANES = 256
HEAD_PAIRS = KV_WIDTH // LANES
ANY_ROWS = 8
FLAG_BITS = 32
FLAG_WORDS = SEQ // KV_TILE // FLAG_BITS
SLOPES = tuple(2.0 ** (-8.0 * (h + 1) / N_HEADS) for h in range(N_HEADS))

F32 = jnp.float32
BF16 = jnp.bfloat16


def _params(*sem):
    return pltpu.CompilerParams(dimension_semantics=sem, vmem_limit_bytes=VMEM_LIMIT)


def _dot(a, b):
    return jnp.dot(a, b, preferred_element_type=F32)


def _dot_nt(a, b):
    return lax.dot_general(a, b, (((1,), (1,)), ((), ())), preferred_element_type=F32)


def _dot_tn(a, b):
    return lax.dot_general(a, b, (((0,), (0,)), ((), ())), preferred_element_type=F32)


def _split(x):
    hi = x.astype(BF16)
    lo = (x - hi.astype(F32)).astype(BF16)
    return hi, lo


def _dot3(a, b, dot=_dot):
    ah, al = _split(a)
    bh, bl = _split(b)
    return dot(ah, bh) + (dot(ah, bl) + dot(al, bh))


def _sigmoid(x):
    return 1.0 / (1.0 + jnp.exp(-x))


def _layer_norm(x, g, b):
    mu = jnp.mean(x, axis=-1, keepdims=True)
    xc = x - mu
    var = jnp.mean(xc * xc, axis=-1, keepdims=True)
    return xc * lax.rsqrt(var + LN_EPS) * g + b


def _const_spec(shape):
    return pl.BlockSpec(shape, lambda *_: (0,) * len(shape))


def _iota(shape, axis):
    return lax.broadcasted_iota(jnp.int32, shape, axis)


def _ada_kernel(c_ref, w_ref, b_ref, o_ref):
    o_ref[...] = _dot3(c_ref[...], w_ref[...]) + b_ref[...]


def _adaln(c_all, w_ada, b_ada):
    n = c_all.shape[0]
    tn = 1536
    return pl.pallas_call(
        _ada_kernel,
        grid=(6 * D_MODEL // tn,),
        in_specs=[pl.BlockSpec((n, D_MODEL), lambda j: (0, 0)),
                  pl.BlockSpec((D_MODEL, tn), lambda j: (0, j)),
                  pl.BlockSpec((1, tn), lambda j: (0, j))],
        out_specs=pl.BlockSpec((n, tn), lambda j: (0, j)),
        out_shape=jax.ShapeDtypeStruct((n, 6 * D_MODEL), F32),
        compiler_params=_params("parallel"),
    )(c_all, w_ada, b_ada.reshape(1, -1))


def _proj_kernel(x_ref, sc_ref, sh_ref, wq_ref, wkv_ref, wglu_ref, wg_ref, wm_ref,
                 q_ref, kv_ref, win_ref, kvb_ref, u_ref, g_ref, m_ref):
    h = (x_ref[...] * (1.0 + sc_ref[...]) + sh_ref[...]).astype(BF16)
    q_ref[...] = _dot(h, wq_ref[...]).astype(BF16)
    kv = _dot(h, wkv_ref[...])
    kv_ref[...] = kv[:, :4 * KV_WIDTH]
    win_ref[...] = kv[:, 4 * KV_WIDTH:]
    kvb_ref[...] = kv.astype(BF16)
    glu = _dot(h, wglu_ref[...])
    u_ref[...] = glu[:, :D_CONV] * _sigmoid(glu[:, D_CONV:])
    g_ref[...] = _sigmoid(_dot(h, wg_ref[...]))
    m_ref[...] = _sigmoid(_dot(h, wm_ref[...]))


def _project(x, sc, sh, wts, tm):
    n = x.shape[0]
    per_row = sc.shape[0] != 1
    mod_rows = tm if per_row else 1

    def mod_spec(col):
        return pl.BlockSpec((mod_rows, D_MODEL), (lambda i: (i, col)) if per_row else (lambda i: (0, col)))

    widths = (Q_WIDTH, 4 * KV_WIDTH, 2 * KV_WIDTH, N_KV_SETS * KV_WIDTH, D_CONV, GATE_PAD, 2 * D_MODEL)
    dtypes = (BF16, F32, F32, BF16, F32, F32, F32)
    return pl.pallas_call(
        _proj_kernel,
        grid=(n // tm,),
        in_specs=[pl.BlockSpec((tm, D_MODEL), lambda i: (i, 0)), mod_spec(1), mod_spec(0)]
                 + [_const_spec(w.shape) for w in wts],
        out_specs=[pl.BlockSpec((tm, w), lambda i: (i, 0)) for w in widths],
        out_shape=[jax.ShapeDtypeStruct((n, w), d) for w, d in zip(widths, dtypes)],
        compiler_params=_params("parallel"),
    )(x, sc, sh, *wts)


def _conv_kernel(cur_ref, halo_ref, dw_ref, dwb_ref, lg_ref, lb_ref, wpw_ref, o_ref, ext_ref):
    bb, tm, _ = cur_ref.shape
    ext_ref[:, :CONV_HALO, :] = halo_ref[...]
    ext_ref[:, CONV_HALO:, :] = cur_ref[...]
    first = CONV_HALO - (CONV_WIDTH - 1)
    acc = jnp.zeros((bb, tm, D_CONV), F32)
    for j in range(CONV_WIDTH):
        acc = acc + ext_ref[:, first + j:first + j + tm, :] * dw_ref[j:j + 1, :]
    y = _layer_norm(acc + dwb_ref[...], lg_ref[...], lb_ref[...])
    y = (y * _sigmoid(y)).reshape(bb * tm, D_CONV).astype(BF16)
    o_ref[...] = _dot(y, wpw_ref[...]).reshape(bb, tm, D_MODEL)


def _conv_branch(cur, halo, dw_w, dw_b, ln_g, ln_b, w_pw, bb):
    b, tm, _ = cur.shape
    return pl.pallas_call(
        _conv_kernel,
        grid=(b // bb,),
        in_specs=[pl.BlockSpec((bb, tm, D_CONV), lambda i: (i, 0, 0)),
                  pl.BlockSpec((bb, CONV_HALO, D_CONV), lambda i: (i, 0, 0)),
                  _const_spec((CONV_WIDTH, D_CONV)), _const_spec((1, D_CONV)),
                  _const_spec((1, D_CONV)), _const_spec((1, D_CONV)),
                  _const_spec((D_CONV, D_MODEL))],
        out_specs=pl.BlockSpec((bb, tm, D_MODEL), lambda i: (i, 0, 0)),
        out_shape=jax.ShapeDtypeStruct((b, tm, D_MODEL), F32),
        scratch_shapes=[pltpu.VMEM((bb, CONV_HALO + tm, D_CONV), F32)],
        compiler_params=_params("parallel"),
    )(cur, halo, dw_w, dw_b.reshape(1, -1), ln_g.reshape(1, -1), ln_b.reshape(1, -1), w_pw)


MERGE_TILE = 512
MOE_TILE = 1024
MOE_CHUNK = 160
MOE_PAD = 256


def _merge_kernel(x_ref, cy_ref, oc_ref, os_ref, ow_ref, g_ref, mg_ref, g1_ref, sc2_ref, sh2_ref,
                  exp_ref, wo_ref, lg_ref, lb_ref, wr_ref, br_ref,
                  x1_ref, h2_ref, wt_ref, rank_ref, cnt_ref):
    tm = x_ref.shape[0]
    gh, gl = _split(g_ref[...])
    nsa = jnp.zeros((tm, D_MODEL), F32)
    for n, o_ref in enumerate((oc_ref, os_ref, ow_ref)):
        e = exp_ref[n]
        nsa = nsa + o_ref[...] * (_dot(gh, e) + _dot(gl, e))
    mix = mg_ref[:, :D_MODEL] * cy_ref[...] + mg_ref[:, D_MODEL:] * nsa
    y = _dot(mix.astype(BF16), wo_ref[...])
    x1 = _layer_norm(DN_ALPHA * x_ref[...] + g1_ref[...] * y, lg_ref[...], lb_ref[...])
    x1_ref[...] = x1
    h2 = x1 * (1.0 + sc2_ref[...]) + sh2_ref[...]
    h2_ref[...] = h2.astype(BF16)
    logits = _dot3(wr_ref[...], h2, _dot_nt) + br_ref[...]
    eidx = _iota(logits.shape, 0)
    picked = jnp.zeros(logits.shape, F32)
    wsum = jnp.zeros((1, tm), F32)
    wts = jnp.zeros(logits.shape, F32)
    v0 = None
    for k in range(TOP_K):
        v = jnp.max(logits, axis=0, keepdims=True)
        first = jnp.min(jnp.where(logits == v, eidx, N_EXPERTS), axis=0, keepdims=True)
        hit = eidx == first
        if k == 0:
            v0 = v
        ev = jnp.exp(v - v0)
        wts = wts + jnp.where(hit, ev, 0.0)
        wsum = wsum + ev
        picked = picked + jnp.where(hit, 1.0, 0.0)
        logits = jnp.where(hit, -jnp.inf, logits)
    wt_ref[0] = wts / wsum
    upper = jnp.where(_iota((tm, tm), 0) < _iota((tm, tm), 1), 1.0, 0.0).astype(BF16)
    rank_ref[0] = _dot(picked.astype(BF16), upper)
    cnt = jnp.sum(picked, axis=1, keepdims=True)
    cnt_ref[0] = jnp.broadcast_to(cnt, (N_EXPERTS, LANES)).astype(jnp.int32)


def _merge(x, conv_y, o_cmp, o_slc, o_win, g_nsa, merge_g, mod, gate_expand, w_out, ln_g, ln_b,
           w_router_t, b_router):
    n = x.shape[0]
    tm = MERGE_TILE
    nt = n // tm
    per_row = mod.shape[0] != 1

    def row_spec(w):
        return pl.BlockSpec((tm, w), lambda i: (i, 0))

    def mod_spec(col):
        return pl.BlockSpec((tm if per_row else 1, D_MODEL),
                            (lambda i: (i, col)) if per_row else (lambda i: (0, col)))

    tile_spec = pl.BlockSpec((1, N_EXPERTS, tm), lambda i: (i, 0, 0))
    return pl.pallas_call(
        _merge_kernel,
        grid=(nt,),
        in_specs=[row_spec(D_MODEL)] * 5 + [row_spec(GATE_PAD), row_spec(2 * D_MODEL),
                  mod_spec(2), mod_spec(4), mod_spec(3),
                  _const_spec(gate_expand.shape), _const_spec(w_out.shape),
                  _const_spec((1, D_MODEL)), _const_spec((1, D_MODEL)),
                  _const_spec(w_router_t.shape), _const_spec((N_EXPERTS, 1))],
        out_specs=[row_spec(D_MODEL), row_spec(D_MODEL), tile_spec, tile_spec,
                   pl.BlockSpec((1, N_EXPERTS, LANES), lambda i: (i, 0, 0))],
        out_shape=[jax.ShapeDtypeStruct((n, D_MODEL), F32), jax.ShapeDtypeStruct((n, D_MODEL), BF16),
                   jax.ShapeDtypeStruct((nt, N_EXPERTS, tm), F32),
                   jax.ShapeDtypeStruct((nt, N_EXPERTS, tm), F32),
                   jax.ShapeDtypeStruct((nt, N_EXPERTS, LANES), jnp.int32)],
        compiler_params=_params("parallel"),
    )(x, conv_y, o_cmp, o_slc, o_win, g_nsa, merge_g, mod, mod, mod, gate_expand, w_out,
      ln_g.reshape(1, -1), ln_b.reshape(1, -1), w_router_t, b_router.reshape(-1, 1))


def _moe_kernel(cnt_ref, x1_ref, h2_ref, wt_ref, rank_ref, g2_ref, wup_ref, bup_ref, wdn_ref, bdn_ref,
                lg_ref, lb_ref, o_ref, acc_ref):
    i = pl.program_id(0)
    e = pl.program_id(1)
    tm = x1_ref.shape[0]

    @pl.when(e == 0)
    def _():
        acc_ref[...] = jnp.zeros_like(acc_ref)

    cnt = cnt_ref[i * N_EXPERTS + e]

    def chunk(ch, carry):
        w_e = wt_ref[0, pl.ds(e, 1), :]
        slot = rank_ref[0, pl.ds(e, 1), :] - (ch * MOE_CHUNK).astype(F32)
        rows = _iota((MOE_PAD, tm), 0).astype(F32)
        onehot = jnp.where((rows == slot) & (rows < float(MOE_CHUNK)) & (w_e > 0.0), 1.0, 0.0)
        wc = jnp.sum(onehot[:MOE_CHUNK] * w_e, axis=1, keepdims=True)
        sel = onehot.astype(BF16)
        xc = _dot(sel, h2_ref[...])[:MOE_CHUNK].astype(BF16)
        u = _dot(xc, wup_ref[0]) + bup_ref[0]
        x_glu = jnp.minimum(u[:, :D_EXPERT], SWIGLU_LIMIT)
        x_lin = jnp.clip(u[:, D_EXPERT:], -SWIGLU_LIMIT, SWIGLU_LIMIT)
        a = x_glu * _sigmoid(SWIGLU_ALPHA * x_glu) * (x_lin + 1.0)
        y = ((_dot(a.astype(BF16), wdn_ref[0]) + bdn_ref[0]) * wc).astype(BF16)
        y = jnp.concatenate([y, jnp.zeros((MOE_PAD - MOE_CHUNK, D_MODEL), BF16)], axis=0)
        acc_ref[...] += _dot_tn(sel, y)
        return carry

    lax.fori_loop(0, (cnt + MOE_CHUNK - 1) // MOE_CHUNK, chunk, 0)

    @pl.when(e == N_EXPERTS - 1)
    def _():
        o_ref[...] = _layer_norm(DN_ALPHA * x1_ref[...] + g2_ref[...] * acc_ref[...],
                                 lg_ref[...], lb_ref[...])


def _moe(x1, h2, wt, rank, cnt, mod, w_up, b_up, w_down, b_down, ln_g, ln_b):
    n = x1.shape[0]
    tm = MOE_TILE
    nt = n // tm
    per_row = mod.shape[0] != 1
    g2_spec = pl.BlockSpec((tm if per_row else 1, D_MODEL),
                           (lambda i, e, c: (i, 5)) if per_row else (lambda i, e, c: (0, 5)))
    tile_spec = pl.BlockSpec((1, N_EXPERTS, tm), lambda i, e, c: (i, 0, 0))
    grid_spec = pltpu.PrefetchScalarGridSpec(
        num_scalar_prefetch=1,
        grid=(nt, N_EXPERTS),
        in_specs=[pl.BlockSpec((tm, D_MODEL), lambda i, e, c: (i, 0)),
                  pl.BlockSpec((tm, D_MODEL), lambda i, e, c: (i, 0)),
                  tile_spec, tile_spec, g2_spec,
                  pl.BlockSpec((1, D_MODEL, 2 * D_EXPERT), lambda i, e, c: (e, 0, 0)),
                  pl.BlockSpec((1, 1, 2 * D_EXPERT), lambda i, e, c: (e, 0, 0)),
                  pl.BlockSpec((1, D_EXPERT, D_MODEL), lambda i, e, c: (e, 0, 0)),
                  pl.BlockSpec((1, 1, D_MODEL), lambda i, e, c: (e, 0, 0)),
                  pl.BlockSpec((1, D_MODEL), lambda i, e, c: (0, 0)),
                  pl.BlockSpec((1, D_MODEL), lambda i, e, c: (0, 0))],
        out_specs=pl.BlockSpec((tm, D_MODEL), lambda i, e, c: (i, 0)),
        scratch_shapes=[pltpu.VMEM((tm, D_MODEL), F32)],
    )
    return pl.pallas_call(
        _moe_kernel,
        grid_spec=grid_spec,
        out_shape=jax.ShapeDtypeStruct((n, D_MODEL), F32),
        compiler_params=_params("parallel", "arbitrary"),
    )(cnt, x1, h2, wt, rank, mod, w_up, b_up.reshape(N_EXPERTS, 1, -1), w_down,
      b_down.reshape(N_EXPERTS, 1, -1), ln_g.reshape(1, -1), ln_b.reshape(1, -1))


def _softmax_rows(s, valid):
    s = jnp.where(valid, s, NEG_INF)
    m = jnp.max(s, axis=-1, keepdims=True)
    e = jnp.exp(s - m)
    l = jnp.sum(e, axis=-1, keepdims=True)
    return e * jnp.where(m > 0.5 * NEG_INF, 1.0 / l, 0.0)


def _flash_update(s, valid, v, m_old, l_old, acc_old, pv=_dot):
    s = jnp.where(valid, s, NEG_INF)
    m_new = jnp.maximum(m_old, jnp.max(s, axis=-1, keepdims=True))
    p = jnp.where(valid, jnp.exp(s - m_new), 0.0)
    alpha = jnp.exp(m_old - m_new)
    l_new = alpha * l_old + jnp.sum(p, axis=-1, keepdims=True)
    acc_new = alpha * acc_old + pv(p.astype(BF16), v)
    return m_new, l_new, acc_new


def _shift_right_one(x):
    return jnp.where(_iota(x.shape, 1) == 0, 0.0, pltpu.roll(x, 1, 1))


def _block_scores(imp, nj):
    parts = [imp[:, k * nj:(k + 1) * nj] for k in range(SEL_RATIO)]
    return (parts[0] + parts[1]) + (parts[2] + parts[3]) + _shift_right_one(parts[3])


def _select_blocks(score, qblk, n_blocks):
    j = _iota(score.shape, 1)
    back = qblk - j
    forced = (j == 0) | ((back >= 0) & (back < N_LOCAL))
    score = jnp.where(forced, FORCE_SCORE, jnp.where(back >= 0, score, -1.0))
    score = jnp.where(j < n_blocks, score, -jnp.inf)
    picked = jnp.zeros(score.shape, F32)
    lane = j.astype(F32)
    for _ in range(N_SEL):
        m = jnp.max(score, axis=-1, keepdims=True)
        first = jnp.min(jnp.where(score == m, lane, float(SEL_LANES)), axis=-1, keepdims=True)
        hit = lane == first
        picked = jnp.where(hit, 1.0, picked)
        score = jnp.where(hit, -jnp.inf, score)
    return picked


def _gelu_tanh(x):
    return x * (0.5 * (1.0 + jnp.tanh(0.7978845608028654 * (x + 0.044715 * (x * x * x)))))


def _compress_rows(buf_ref, pe_ref, w1_ref, w2_ref, nj):
    outs = []
    for pair in range(HEAD_PAIRS):
        acc = jnp.zeros((SEL_RATIO * nj, 2 * PHI_HIDDEN), F32)
        for s in range(CMP_LEN):
            parts = [buf_ref[pair, pl.ds(CMP_STRIDE * k + s, nj, stride=SEL_LEN), :] for k in range(SEL_RATIO)]
            x = (jnp.concatenate(parts, axis=0) + pe_ref[s:s + 1, :]).astype(BF16)
            acc = acc + _dot(x, w1_ref[s])
        outs.append(_dot(_gelu_tanh(acc).astype(BF16), w2_ref[...]))
    return jnp.concatenate(outs, axis=1)


CMP_TILE = 4096


def _cmp_prompt_kernel(x_ref, halo_ref, pe_ref, w1_ref, w2_ref, o_ref, buf_ref):
    t = pl.program_id(0)
    last = t == pl.num_programs(0) - 1
    for pair in range(HEAD_PAIRS):
        cols = slice(pair * LANES, (pair + 1) * LANES)
        buf_ref[pair, :CMP_TILE, :] = x_ref[:, cols]
        buf_ref[pair, CMP_TILE:, :] = jnp.where(last, 0.0, halo_ref[:, cols])
    nj = CMP_TILE // SEL_LEN
    out = _compress_rows(buf_ref, pe_ref, w1_ref, w2_ref, nj)
    o_ref[...] = out.reshape(SEL_RATIO, nj, KV_WIDTH).astype(BF16)


def _compress_prompt(kv4, kv_set, pe, w1, w2):
    t_len = kv4.shape[0]
    nt = t_len // CMP_TILE
    nj = CMP_TILE // SEL_LEN
    halo_blocks = CMP_TILE // SEL_LEN
    last_halo = t_len // SEL_LEN - 1
    out = pl.pallas_call(
        _cmp_prompt_kernel,
        grid=(nt,),
        in_specs=[pl.BlockSpec((CMP_TILE, KV_WIDTH), lambda t: (t, kv_set)),
                  pl.BlockSpec((SEL_LEN, KV_WIDTH),
                               lambda t: (jnp.minimum((t + 1) * halo_blocks, last_halo), kv_set)),
                  _const_spec(pe.shape), _const_spec(w1.shape), _const_spec(w2.shape)],
        out_specs=pl.BlockSpec((SEL_RATIO, nj, KV_WIDTH), lambda t: (0, t, 0)),
        out_shape=jax.ShapeDtypeStruct((SEL_RATIO, t_len // SEL_LEN, KV_WIDTH), BF16),
        scratch_shapes=[pltpu.VMEM((HEAD_PAIRS, CMP_TILE + SEL_LEN, LANES), F32)],
        compiler_params=_params("arbitrary"),
    )(kv4, kv4, pe, w1, w2)
    return out.reshape(t_len // CMP_STRIDE, KV_WIDTH)


def _stack_heads(q_ref, g):
    return jnp.concatenate([q_ref[:, (g * GROUP + r) * HEAD_DIM:(g * GROUP + r + 1) * HEAD_DIM]
                            for r in range(GROUP)], axis=0)


def _head_bias(g, rel):
    return jnp.concatenate([jnp.broadcast_to(SLOPES[g * GROUP + r] * rel, (Q_TILE, rel.shape[1]))
                            for r in range(GROUP)], axis=0)


def _cmp_attn_prompt_kernel(q_ref, kc_ref, vc_ref, o_ref, sel_ref, any_ref):
    nc = kc_ref.shape[0]
    nj = nc // SEL_RATIO
    s0 = pl.program_id(0) * Q_TILE
    tq = s0 + _iota((Q_TILE, 1), 0)
    col = _iota((1, nc), 1)
    ends = (SEL_RATIO * (col % nj) + col // nj) * CMP_STRIDE + (CMP_LEN - 1)
    rel = (ends - s0).astype(F32)
    valid = jnp.concatenate([ends <= tq] * GROUP, axis=0)
    any_rows = []
    scores = []
    for g in range(N_KV_HEADS):
        kc = kc_ref[:, g * HEAD_DIM:(g + 1) * HEAD_DIM]
        vc = vc_ref[:, g * HEAD_DIM:(g + 1) * HEAD_DIM]
        p = _softmax_rows(_dot_nt(_stack_heads(q_ref, g), kc) + _head_bias(g, rel), valid)
        o = _dot(p.astype(BF16), vc)
        imp = jnp.zeros((Q_TILE, nc), F32)
        for r in range(GROUP):
            h = g * GROUP + r
            o_ref[:, h * HEAD_DIM:(h + 1) * HEAD_DIM] = o[r * Q_TILE:(r + 1) * Q_TILE]
            imp = imp + p[r * Q_TILE:(r + 1) * Q_TILE]
        scores.append(_block_scores(imp, nj))
    picked = _select_blocks(jnp.concatenate(scores, axis=0),
                            jnp.concatenate([tq // SEL_LEN] * N_KV_HEADS, axis=0), nj)
    for g in range(N_KV_HEADS):
        sel_ref[g] = picked[g * Q_TILE:(g + 1) * Q_TILE].astype(BF16)
        any_rows.append(jnp.max(picked[g * Q_TILE:(g + 1) * Q_TILE], axis=0, keepdims=True))
    any_ref[0] = jnp.concatenate(any_rows + [jnp.zeros((ANY_ROWS - N_KV_HEADS, SEL_LANES), F32)], axis=0)


def _cmp_attn_prompt(q, kc, vc):
    t_len = q.shape[0]
    nt = t_len // Q_TILE
    return pl.pallas_call(
        _cmp_attn_prompt_kernel,
        grid=(nt,),
        in_specs=[pl.BlockSpec((Q_TILE, Q_WIDTH), lambda i: (i, 0)),
                  _const_spec(kc.shape), _const_spec(vc.shape)],
        out_specs=[pl.BlockSpec((Q_TILE, Q_WIDTH), lambda i: (i, 0)),
                   pl.BlockSpec((N_KV_HEADS, Q_TILE, SEL_LANES), lambda i: (0, i, 0)),
                   pl.BlockSpec((1, ANY_ROWS, SEL_LANES), lambda i: (i, 0, 0))],
        out_shape=[jax.ShapeDtypeStruct((t_len, Q_WIDTH), F32),
                   jax.ShapeDtypeStruct((N_KV_HEADS, t_len, SEL_LANES), BF16),
                   jax.ShapeDtypeStruct((nt, ANY_ROWS, SEL_LANES), F32)],
        compiler_params=_params("parallel"),
    )(q, kc, vc)


def _tile_flags(any_sel):
    nt = any_sel.shape[0]
    per_tile = KV_TILE // SEL_LEN
    hit = any_sel[:, :N_KV_HEADS].reshape(nt, N_KV_HEADS, SEL_LANES // per_tile, per_tile).max(-1) > 0.0
    bits = hit.reshape(nt, N_KV_HEADS, -1, FLAG_BITS).astype(jnp.uint32) << jnp.arange(FLAG_BITS, dtype=jnp.uint32)
    words = lax.bitcast_convert_type(bits.sum(-1, dtype=jnp.uint32), jnp.int32)
    per = SLC_Q_TILE // Q_TILE
    words = words.reshape(nt // per, per, N_KV_HEADS, -1)
    joined = words[:, 0]
    for k in range(1, per):
        joined = joined | words[:, k]
    return joined.reshape(-1)


def _slc_attn_prompt_kernel(flag_ref, q_ref, k_ref, v_ref, sel_ref, o_ref, qs_ref, m_ref, l_ref, acc_ref):
    i = pl.program_id(0)
    tq = i * SLC_Q_TILE + _iota((SLC_Q_TILE, 1), 0)
    for h in range(N_HEADS):
        qs_ref[h // GROUP, (h % GROUP) * SLC_Q_TILE:(h % GROUP + 1) * SLC_Q_TILE, :] = q_ref[:, h * HEAD_DIM:(h + 1) * HEAD_DIM]
    m_ref[...] = jnp.full(m_ref.shape, NEG_INF, F32)
    l_ref[...] = jnp.zeros(l_ref.shape, F32)
    acc_ref[...] = jnp.zeros(acc_ref.shape, F32)

    def tile(t, carry):
        base = pl.multiple_of(t * KV_TILE, KV_TILE)
        causal = tq - (base + _iota((1, KV_TILE), 1)) >= 0
        rel = (base - i * SLC_Q_TILE + _iota((1, KV_TILE), 1)).astype(F32)
        expand = jnp.where(_iota((SEL_LANES, KV_TILE), 0)
                           == t * (KV_TILE // SEL_LEN) + _iota((SEL_LANES, KV_TILE), 1) // SEL_LEN,
                           1.0, 0.0).astype(BF16)
        for g in range(N_KV_HEADS):
            word = flag_ref[(i * N_KV_HEADS + g) * FLAG_WORDS + t // FLAG_BITS]

            @pl.when(((word >> (t % FLAG_BITS)) & 1) == 1)
            def _():
                valid = (_dot(sel_ref[g], expand) > 0.5) & causal
                kt = k_ref[pl.ds(base, KV_TILE), g * HEAD_DIM:(g + 1) * HEAD_DIM]
                vt = v_ref[pl.ds(base, KV_TILE), g * HEAD_DIM:(g + 1) * HEAD_DIM]
                raw = _dot_nt(qs_ref[g], kt)
                s = jnp.concatenate(
                    [jnp.where(valid, raw[r * SLC_Q_TILE:(r + 1) * SLC_Q_TILE] + SLOPES[g * GROUP + r] * rel, NEG_INF)
                     for r in range(GROUP)], axis=0)
                m_old = m_ref[g]
                m_new = jnp.maximum(m_old, jnp.max(s, axis=-1, keepdims=True))
                p = jnp.exp(s - jnp.concatenate([m_new] * (KV_TILE // LANES), axis=1))
                alpha = jnp.exp(m_old - m_new)
                l_ref[g] = alpha * l_ref[g] + jnp.sum(p, axis=-1, keepdims=True)
                acc_ref[g] = alpha[:, :HEAD_DIM] * acc_ref[g] + _dot(p.astype(BF16), vt)
                m_ref[g] = m_new
        return carry

    lax.fori_loop(0, (i * SLC_Q_TILE) // KV_TILE + 1, tile, 0)
    for h in range(N_HEADS):
        rows = slice((h % GROUP) * SLC_Q_TILE, (h % GROUP + 1) * SLC_Q_TILE)
        l = l_ref[h // GROUP, rows, :HEAD_DIM]
        o_ref[:, h * HEAD_DIM:(h + 1) * HEAD_DIM] = acc_ref[h // GROUP, rows, :] / jnp.where(l > 0.0, l, 1.0)


def _slc_attn_prompt(q, kvb, sel, flags):
    t_len = q.shape[0]
    resident = functools.partial(pl.BlockSpec, pipeline_mode=pl.Buffered(1))
    rows = GROUP * SLC_Q_TILE
    grid_spec = pltpu.PrefetchScalarGridSpec(
        num_scalar_prefetch=1,
        grid=(t_len // SLC_Q_TILE,),
        in_specs=[pl.BlockSpec((SLC_Q_TILE, Q_WIDTH), lambda i, f: (i, 0)),
                  resident((t_len, KV_WIDTH), lambda i, f: (0, 2)),
                  resident((t_len, KV_WIDTH), lambda i, f: (0, 3)),
                  pl.BlockSpec((N_KV_HEADS, SLC_Q_TILE, SEL_LANES), lambda i, f: (0, i, 0))],
        out_specs=pl.BlockSpec((SLC_Q_TILE, Q_WIDTH), lambda i, f: (i, 0)),
        scratch_shapes=[pltpu.VMEM((N_KV_HEADS, rows, HEAD_DIM), BF16),
                        pltpu.VMEM((N_KV_HEADS, rows, LANES), F32), pltpu.VMEM((N_KV_HEADS, rows, LANES), F32),
                        pltpu.VMEM((N_KV_HEADS, rows, HEAD_DIM), F32)],
    )
    return pl.pallas_call(
        _slc_attn_prompt_kernel,
        grid_spec=grid_spec,
        out_shape=jax.ShapeDtypeStruct((t_len, Q_WIDTH), F32),
        compiler_params=_params("parallel"),
    )(flags, q, kvb, kvb, sel)


def _win_attn_prompt_kernel(q_ref, k_ref, v_ref, o_ref):
    s0 = pl.multiple_of(pl.program_id(0) * Q_TILE, Q_TILE)
    n_keys = WINDOW + Q_TILE
    tq = s0 + _iota((Q_TILE, 1), 0)
    pw = s0 - WINDOW + _iota((1, n_keys), 1)
    dw = tq - pw
    valid = (dw >= 0) & (dw < WINDOW) & (pw >= 0)
    valid = jnp.concatenate([valid] * GROUP, axis=0)
    rel = (_iota((1, n_keys), 1) - WINDOW).astype(F32)
    for g in range(N_KV_HEADS):
        kw = k_ref[pl.ds(s0, n_keys), g * HEAD_DIM:(g + 1) * HEAD_DIM]
        vw = v_ref[pl.ds(s0, n_keys), g * HEAD_DIM:(g + 1) * HEAD_DIM]
        p = _softmax_rows(_dot_nt(_stack_heads(q_ref, g), kw) + _head_bias(g, rel), valid)
        o = _dot(p.astype(BF16), vw)
        for r in range(GROUP):
            h = g * GROUP + r
            o_ref[:, h * HEAD_DIM:(h + 1) * HEAD_DIM] = o[r * Q_TILE:(r + 1) * Q_TILE]


def _win_attn_prompt(q, kw_pad, vw_pad):
    t_len = q.shape[0]
    resident = functools.partial(pl.BlockSpec, pipeline_mode=pl.Buffered(1))
    return pl.pallas_call(
        _win_attn_prompt_kernel,
        grid=(t_len // Q_TILE,),
        in_specs=[pl.BlockSpec((Q_TILE, Q_WIDTH), lambda i: (i, 0)),
                  resident(kw_pad.shape, lambda i: (0, 0)), resident(vw_pad.shape, lambda i: (0, 0))],
        out_specs=pl.BlockSpec((Q_TILE, Q_WIDTH), lambda i: (i, 0)),
        out_shape=jax.ShapeDtypeStruct((t_len, Q_WIDTH), F32),
        compiler_params=_params("parallel"),
    )(q, kw_pad, vw_pad)


S_ROWS = N_HEADS * DEC_SEQ
S_TAIL = SEL_LEN
S_CMP_BLOCKS = PAST_LEN // CMP_STRIDE
S_CHUNK = 2048


def _fetch_pages(pt_ref, sem_ref, page_copy, meanwhile=None):
    b = pl.program_id(0)
    slot = b % 2

    def copy(seq, p, dst_slot):
        return page_copy(pt_ref[seq * N_PAGES + p], p, dst_slot, sem_ref.at[dst_slot])

    def start(seq, dst_slot):
        def body(p, c):
            copy(seq, p, dst_slot).start()
            return c
        lax.fori_loop(0, N_PAGES, body, 0)

    @pl.when(b == 0)
    def _():
        start(b, slot)

    @pl.when(b + 1 < pl.num_programs(0))
    def _():
        start(b + 1, 1 - slot)

    done = meanwhile() if meanwhile is not None else None

    def wait(p, c):
        copy(b, p, slot).wait()
        return c
    lax.fori_loop(0, N_PAGES, wait, 0)
    return slot, done


PAGES_PER_STEP = 8
TAPS_PER_DOT = 8
S_CHUNKS = PAST_LEN // CMP_STRIDE + ANY_ROWS


def _cmp_sample_kernel(kv_set, pt_ref, cache_ref, new_ref, pet_ref, w1_ref, w2_ref, o_ref,
                       stage_ref, buf_ref, sem_ref):
    def page_copy(page, p, slot, sem):
        return pltpu.make_async_copy(cache_ref.at[page, kv_set],
                                     stage_ref.at[slot, pl.ds(p * KV_WIDTH, KV_WIDTH), :], sem)

    slot, _ = _fetch_pages(pt_ref, sem_ref, page_copy)
    per_page = PAGE_SIZE // CMP_STRIDE
    regroup = jnp.where((_iota((PAGE_SIZE, PAGE_SIZE), 1) % CMP_STRIDE) * per_page
                        + _iota((PAGE_SIZE, PAGE_SIZE), 1) // CMP_STRIDE == _iota((PAGE_SIZE, PAGE_SIZE), 0),
                        1.0, 0.0).astype(BF16)

    def to_rows(i, c):
        for u in range(PAGES_PER_STEP):
            p = i * PAGES_PER_STEP + u
            src = pl.multiple_of(p * KV_WIDTH, KV_WIDTH)
            dst = pl.multiple_of(p * per_page, per_page)
            x_t = stage_ref[slot, pl.ds(src, KV_WIDTH), :].astype(BF16)
            rows = _dot_nt(regroup, x_t)
            for pair in range(HEAD_PAIRS):
                for s in range(CMP_STRIDE):
                    buf_ref[pair, s, pl.ds(dst, per_page), :] = rows[s * per_page:(s + 1) * per_page,
                                                                     pair * LANES:(pair + 1) * LANES]
        return c
    lax.fori_loop(0, N_PAGES // PAGES_PER_STEP, to_rows, 0)
    first_row = _iota((ANY_ROWS, LANES), 0) == 0
    for pair in range(HEAD_PAIRS):
        for s in range(CMP_STRIDE):
            tail = jnp.zeros((ANY_ROWS, LANES), F32)
            if s < DEC_SEQ:
                row = new_ref[s:s + 1, pair * LANES:(pair + 1) * LANES].astype(BF16).astype(F32)
                tail = jnp.where(first_row, row, 0.0)
            buf_ref[pair, s, PAST_LEN // CMP_STRIDE:, :] = tail

    bias = jnp.zeros((1, 2 * PHI_HIDDEN), F32)
    for s in range(CMP_LEN):
        bias = bias + jnp.sum(pet_ref[:, s:s + 1] * w1_ref[s].astype(F32), axis=0, keepdims=True)
    nj = S_CMP_BLOCKS // SEL_RATIO
    outs = []
    for pair in range(HEAD_PAIRS):
        acc = jnp.zeros((SEL_RATIO * nj, 2 * PHI_HIDDEN), F32)
        for s0 in range(0, CMP_LEN, TAPS_PER_DOT):
            taps = [jnp.concatenate(
                [buf_ref[pair, s % CMP_STRIDE, pl.ds(k + s // CMP_STRIDE, nj, stride=SEL_RATIO), :]
                 for k in range(SEL_RATIO)], axis=0).astype(BF16) for s in range(s0, s0 + TAPS_PER_DOT)]
            w = w1_ref[s0:s0 + TAPS_PER_DOT].reshape(TAPS_PER_DOT * LANES, 2 * PHI_HIDDEN)
            acc = acc + _dot(jnp.concatenate(taps, axis=1), w)
        outs.append(_dot(_gelu_tanh(acc + bias).astype(BF16), w2_ref[...]))
    o_ref[0] = jnp.concatenate(outs, axis=1).astype(BF16)


def _compress_sample(page_table, cache, kv4_new, kv_set, pe, w1, w2):
    pe_t = pe.T
    grid_spec = pltpu.PrefetchScalarGridSpec(
        num_scalar_prefetch=1,
        grid=(DEC_BATCH,),
        in_specs=[pl.BlockSpec(memory_space=pl.ANY),
                  pl.BlockSpec((DEC_SEQ, KV_WIDTH), lambda b, pt: (b, kv_set)),
                  pl.BlockSpec(pe_t.shape, lambda b, pt: (0, 0)),
                  pl.BlockSpec(w1.shape, lambda b, pt: (0, 0, 0)),
                  pl.BlockSpec(w2.shape, lambda b, pt: (0, 0))],
        out_specs=pl.BlockSpec((1, S_CMP_BLOCKS, KV_WIDTH), lambda b, pt: (b, 0, 0)),
        scratch_shapes=[pltpu.VMEM((2, N_PAGES * KV_WIDTH, PAGE_SIZE), F32),
                        pltpu.VMEM((HEAD_PAIRS, CMP_STRIDE, S_CHUNKS, LANES), F32),
                        pltpu.SemaphoreType.DMA((2,))],
    )
    return pl.pallas_call(
        functools.partial(_cmp_sample_kernel, kv_set),
        grid_spec=grid_spec,
        out_shape=jax.ShapeDtypeStruct((DEC_BATCH, S_CMP_BLOCKS, KV_WIDTH), BF16),
        compiler_params=_params("arbitrary"),
    )(page_table.reshape(-1), cache, kv4_new, pe_t, w1, w2)


def _row_queries():
    return PAST_LEN + _iota((S_ROWS, 1), 0) % DEC_SEQ


def _cmp_attend_sample(q, slope, kc, vc):
    nc = S_CMP_BLOCKS
    nj = nc // SEL_RATIO
    tq = _row_queries()
    col = _iota((1, nc), 1)
    ends = (SEL_RATIO * (col % nj) + col // nj) * CMP_STRIDE + (CMP_LEN - 1)
    s = _dot_nt(q, kc) - slope * (tq - ends).astype(F32)
    p = _softmax_rows(s, ends <= tq)
    out = _dot(p.astype(BF16), vc)
    rows_g = GROUP * DEC_SEQ
    n_blocks = (PAST_LEN + DEC_SEQ + SEL_LEN - 1) // SEL_LEN
    qblk = (PAST_LEN + _iota((DEC_SEQ, 1), 0)) // SEL_LEN
    scores = []
    for g in range(N_KV_HEADS):
        imp = jnp.zeros((DEC_SEQ, nc), F32)
        for r in range(GROUP):
            imp = imp + p[g * rows_g + r * DEC_SEQ:g * rows_g + (r + 1) * DEC_SEQ]
        scores.append(jnp.concatenate([_block_scores(imp, nj), jnp.zeros((DEC_SEQ, SEL_LANES - nj), F32)], axis=1))
    picked = _select_blocks(jnp.concatenate(scores, axis=0), jnp.concatenate([qblk] * N_KV_HEADS, axis=0), n_blocks)
    sel = jnp.concatenate([picked[g * DEC_SEQ:(g + 1) * DEC_SEQ] for g in range(N_KV_HEADS)
                           for _ in range(GROUP)], axis=0).astype(BF16)
    return out, sel


def _pad_new_rows(x):
    return jnp.concatenate([x, jnp.zeros((LANES - DEC_SEQ, x.shape[1]), F32)], axis=0).astype(BF16)


def _attn_sample_kernel(pt_ref, cache_ref, q_ref, slope_ref, kc_ref, vc_ref, new_ref, cwin_ref, wnew_ref,
                        oc_ref, os_ref, ow_ref, buf_ref, sem_ref):
    def page_copy(page, p, slot, sem):
        return pltpu.make_async_copy(cache_ref.at[page, pl.ds(2, 2)],
                                     buf_ref.at[slot, :, :, pl.ds(p * PAGE_SIZE, PAGE_SIZE)], sem)

    q = q_ref[0]
    slope = slope_ref[...]
    slot, (o_cmp, sel) = _fetch_pages(pt_ref, sem_ref, page_copy,
                                      lambda: _cmp_attend_sample(q, slope, kc_ref[0], vc_ref[0]))
    oc_ref[0] = o_cmp
    tq = _row_queries()
    new_pos = PAST_LEN + _iota((1, LANES), 1)
    new_ok = (tq - new_pos >= 0) & (new_pos < PAST_LEN + DEC_SEQ)
    new_dist = (tq - new_pos).astype(F32)

    m = jnp.full((S_ROWS, 1), NEG_INF, F32)
    l = jnp.zeros((S_ROWS, 1), F32)
    acc = jnp.zeros((S_ROWS, KV_WIDTH), F32)
    for c in range(PAST_LEN // S_CHUNK):
        k_t = buf_ref[slot, 0, :, c * S_CHUNK:(c + 1) * S_CHUNK].astype(BF16)
        v_t = buf_ref[slot, 1, :, c * S_CHUNK:(c + 1) * S_CHUNK].astype(BF16)
        pos = c * S_CHUNK + _iota((1, S_CHUNK), 1)
        expand = jnp.where(_iota((SEL_LANES, S_CHUNK), 0)
                           == (c * S_CHUNK + _iota((SEL_LANES, S_CHUNK), 1)) // SEL_LEN, 1.0, 0.0).astype(BF16)
        valid = (_dot(sel, expand) > 0.5) & (tq - pos >= 0)
        s = _dot(q, k_t) - slope * (tq - pos).astype(F32)
        m, l, acc = _flash_update(s, valid, v_t, m, l, acc, pv=_dot_nt)
    new = _pad_new_rows(new_ref[...])
    in_last = jnp.sum(jnp.where(_iota((S_ROWS, SEL_LANES), 1) == PAST_LEN // SEL_LEN, sel.astype(F32), 0.0),
                      axis=1, keepdims=True) > 0.5
    s = _dot_nt(q, new[:, :KV_WIDTH]) - slope * new_dist
    m, l, acc = _flash_update(s, in_last & new_ok, new[:, KV_WIDTH:], m, l, acc)
    os_ref[0] = acc / jnp.where(l > 0.0, l, 1.0)

    wb = cwin_ref.shape[-1]
    pw = PAST_LEN - wb + _iota((1, wb), 1)
    dw = tq - pw
    s_past = _dot(q, cwin_ref[0, 0].astype(BF16)) - slope * dw.astype(F32)
    wnew = _pad_new_rows(wnew_ref[...])
    s_new = _dot_nt(q, wnew[:, :KV_WIDTH]) - slope * new_dist
    valid = jnp.concatenate([(dw >= 0) & (dw < WINDOW) & (pw >= 0), new_ok & (tq - new_pos < WINDOW)], axis=1)
    p = _softmax_rows(jnp.concatenate([s_past, s_new], axis=1), valid).astype(BF16)
    ow_ref[0] = _dot_nt(p[:, :wb], cwin_ref[0, 1].astype(BF16)) + _dot(p[:, wb:], wnew[:, KV_WIDTH:])


def _attn_sample(page_table, cache, qbd, slope_col, kc, vc, kv4_new, cache_win, win_new):
    wb = cache_win.shape[-1]
    seq_spec = lambda rows, w: pl.BlockSpec((1, rows, w), lambda b, pt: (b, 0, 0))
    grid_spec = pltpu.PrefetchScalarGridSpec(
        num_scalar_prefetch=1,
        grid=(DEC_BATCH,),
        in_specs=[pl.BlockSpec(memory_space=pl.ANY),
                  seq_spec(S_ROWS, KV_WIDTH),
                  pl.BlockSpec((S_ROWS, 1), lambda b, pt: (0, 0)),
                  seq_spec(S_CMP_BLOCKS, KV_WIDTH), seq_spec(S_CMP_BLOCKS, KV_WIDTH),
                  pl.BlockSpec((DEC_SEQ, 2 * KV_WIDTH), lambda b, pt: (b, 1)),
                  pl.BlockSpec((1, 2, KV_WIDTH, wb), lambda b, pt: (b, 0, 0, 0)),
                  pl.BlockSpec((DEC_SEQ, 2 * KV_WIDTH), lambda b, pt: (b, 0))],
        out_specs=[seq_spec(S_ROWS, KV_WIDTH)] * 3,
        scratch_shapes=[pltpu.VMEM((2, 2, KV_WIDTH, PAST_LEN), F32), pltpu.SemaphoreType.DMA((2,))],
    )
    return pl.pallas_call(
        _attn_sample_kernel,
        grid_spec=grid_spec,
        out_shape=[jax.ShapeDtypeStruct((DEC_BATCH, S_ROWS, KV_WIDTH), F32)] * 3,
        compiler_params=_params("arbitrary"),
    )(page_table.reshape(-1), cache, qbd, slope_col, kc, vc, kv4_new, cache_win, win_new)


def _gate_expand():
    rows = jnp.arange(GATE_PAD)[:, None]
    cols = jnp.arange(D_MODEL)[None, :]
    return jnp.stack([(rows == (cols // HEAD_DIM) * N_NSA_BRANCH + n) for n in range(N_NSA_BRANCH)]
                     ).astype(BF16)


def _compress_weights(pe, w1, w2):
    per_tile = LANES // HEAD_DIM
    eye = jnp.eye(per_tile, dtype=F32)
    w1_big = jnp.einsum('sde,gh->sgdhe', w1, eye).reshape(CMP_LEN, LANES, per_tile * PHI_HIDDEN).astype(BF16)
    w2_big = jnp.einsum('ed,gh->gehd', w2, eye).reshape(per_tile * PHI_HIDDEN, LANES).astype(BF16)
    return jnp.tile(pe, (1, per_tile)), w1_big, w2_big


def _rows_from_heads(o):
    o = o.reshape(DEC_BATCH, N_KV_HEADS, GROUP, DEC_SEQ, N_KV_HEADS, HEAD_DIM)
    idx = jnp.arange(N_KV_HEADS)
    own = o[:, idx, :, :, idx]
    return own.transpose(1, 3, 0, 2, 4).reshape(DEC_BATCH * DEC_SEQ, Q_WIDTH)


def kernel(x_prompt, x_sample, c_prompt, c_sample, cache_kv, cache_win, state_conv, page_table, w_ada, b_ada, w_in, conv_dw_w, conv_dw_b, conv_ln_g, conv_ln_b, w_conv_out, cmp_pe_k, cmp_w1_k, cmp_w2_k, cmp_pe_v, cmp_w1_v, cmp_w2_v, w_out, ln1_g, ln1_b, w_router, b_router, w_up, b_up, w_down, b_down, ln2_g, ln2_b):
    l = 0
    n_p = SEQ
    n_s = DEC_BATCH * DEC_SEQ

    c_all = jnp.concatenate([c_prompt, c_sample, jnp.zeros((7, D_MODEL), F32)], axis=0)
    mod = _adaln(c_all, w_ada[l], b_ada[l])
    mod_p = mod[0:1]
    mod_s = jnp.repeat(mod[1:1 + DEC_BATCH], DEC_SEQ, axis=0)

    wi = w_in[l]
    wts = ((wi[:, :OFF_KV] * (HEAD_DIM ** -0.5)).astype(BF16),
           wi[:, OFF_KV:OFF_GLU].astype(BF16),
           wi[:, OFF_GLU:OFF_NSA_G].astype(BF16),
           jnp.pad(wi[:, OFF_NSA_G:OFF_MERGE], ((0, 0), (0, GATE_PAD - N_HEADS * N_NSA_BRANCH))).astype(BF16),
           wi[:, OFF_MERGE:].astype(BF16))
    gate_expand = _gate_expand()
    w_out_b = w_out[l].astype(BF16)
    w_pw_b = w_conv_out[l].astype(BF16)
    w_up_b = w_up[l].astype(BF16)
    w_down_b = w_down[l].astype(BF16)
    w_router_t = w_router[l].T
    cmp_k = _compress_weights(cmp_pe_k[l], cmp_w1_k[l], cmp_w2_k[l])
    cmp_v = _compress_weights(cmp_pe_v[l], cmp_w1_v[l], cmp_w2_v[l])
    conv_w = (conv_dw_w[l], conv_dw_b[l], conv_ln_g[l], conv_ln_b[l], w_pw_b)

    def tail(x, conv_y, o3, g_nsa, merge_g, m):
        x1, h2, wt, rank, cnt = _merge(x, conv_y, *o3, g_nsa, merge_g, m, gate_expand, w_out_b,
                                       ln1_g[l], ln1_b[l], w_router_t, b_router[l])
        per = MOE_TILE // MERGE_TILE
        cnt = cnt[:, :, 0].reshape(-1, per, N_EXPERTS)
        before = jnp.cumsum(cnt, axis=1) - cnt
        rank = rank.reshape(-1, per, N_EXPERTS, MERGE_TILE) + before[..., None].astype(F32)
        rank = rank.transpose(0, 2, 1, 3).reshape(-1, N_EXPERTS, MOE_TILE)
        wt = wt.reshape(-1, per, N_EXPERTS, MERGE_TILE).transpose(0, 2, 1, 3).reshape(-1, N_EXPERTS, MOE_TILE)
        return _moe(x1, h2, wt, rank, cnt.sum(1).reshape(-1), m, w_up_b, b_up[l], w_down_b, b_down[l],
                    ln2_g[l], ln2_b[l])

    xp = x_prompt.reshape(n_p, D_MODEL)
    q, kv4, win2, kvb, u, g_nsa, merge_g = _project(xp, mod_p, mod_p, wts, 256)
    ct = 512
    u3 = u.reshape(n_p // ct, ct, D_CONV)
    halo = jnp.concatenate([jnp.zeros((1, CONV_HALO, D_CONV), F32), u3[:-1, ct - CONV_HALO:]], axis=0)
    conv_y = _conv_branch(u3, halo, *conv_w, 1).reshape(n_p, D_MODEL)
    kc = _compress_prompt(kv4, 0, *cmp_k)
    vc = _compress_prompt(kv4, 1, *cmp_v)
    o_cmp, sel, any_sel = _cmp_attn_prompt(q, kc, vc)
    o_slc = _slc_attn_prompt(q, kvb, sel, _tile_flags(any_sel))
    kw_pad = jnp.pad(kvb[:, 4 * KV_WIDTH:5 * KV_WIDTH], ((WINDOW, 0), (0, 0)))
    vw_pad = jnp.pad(kvb[:, 5 * KV_WIDTH:], ((WINDOW, 0), (0, 0)))
    o_win = _win_attn_prompt(q, kw_pad, vw_pad)
    y_p = tail(xp, conv_y, (o_cmp, o_slc, o_win), g_nsa, merge_g, mod_p)
    out_kv_p = kv4.reshape(1, 1, n_p, 4, N_KV_HEADS, HEAD_DIM)
    out_win_p = win2[n_p - WINDOW:].reshape(1, 1, WINDOW, 2, N_KV_HEADS, HEAD_DIM)
    out_conv_p = u[n_p - (CONV_WIDTH - 1):].reshape(1, 1, CONV_WIDTH - 1, D_CONV)

    xs = x_sample.reshape(n_s, D_MODEL)
    q, kv4, win2, kvb, u, g_nsa, merge_g = _project(xs, mod_s, mod_s, wts, 256)
    u3 = u.reshape(DEC_BATCH, DEC_SEQ, D_CONV)
    st = state_conv[l]
    halo = jnp.concatenate([jnp.zeros((DEC_BATCH, CONV_HALO - (CONV_WIDTH - 1), D_CONV), F32), st], axis=1)
    conv_y = _conv_branch(u3, halo, *conv_w, DEC_BATCH).reshape(n_s, D_MODEL)
    qh = q.reshape(DEC_BATCH, DEC_SEQ, N_KV_HEADS, GROUP, HEAD_DIM).transpose(0, 2, 3, 1, 4)
    qbd = jnp.einsum('bgrqd,gh->bgrqhd', qh, jnp.eye(N_KV_HEADS, dtype=BF16)).reshape(DEC_BATCH, S_ROWS, KV_WIDTH)
    slope_col = jnp.repeat(jnp.asarray(SLOPES, F32), DEC_SEQ).reshape(S_ROWS, 1)
    cache = cache_kv[l].transpose(0, 2, 3, 4, 1).reshape(-1, 4, KV_WIDTH, PAGE_SIZE)
    cwin = cache_win[l].transpose(0, 2, 3, 4, 1).reshape(DEC_BATCH, 2, KV_WIDTH, -1)
    kc = _compress_sample(page_table, cache, kv4, 0, *cmp_k)
    vc = _compress_sample(page_table, cache, kv4, 1, *cmp_v)
    o_cmp, o_slc, o_win = _attn_sample(page_table, cache, qbd, slope_col, kc, vc, kv4, cwin, win2)
    o3 = tuple(_rows_from_heads(o) for o in (o_cmp, o_slc, o_win))
    y_s = tail(xs, conv_y, o3, g_nsa, merge_g, mod_s)
    out_kv_s = kv4.reshape(1, DEC_BATCH, DEC_SEQ, 4, N_KV_HEADS, HEAD_DIM)
    win_new = win2.reshape(DEC_BATCH, DEC_SEQ, 2, N_KV_HEADS, HEAD_DIM)
    out_win_s = jnp.concatenate([cache_win[l], win_new], axis=1)[:, DEC_SEQ:][None]
    out_conv_s = jnp.concatenate([st, u3], axis=1)[:, DEC_SEQ:][None]

    return (y_p.reshape(1, n_p, D_MODEL), y_s.reshape(DEC_BATCH, DEC_SEQ, D_MODEL),
            out_kv_p, out_kv_s, out_win_p, out_win_s, out_conv_p, out_conv_s)
```

```python
import functools

import jax
import jax.numpy as jnp
from jax import lax
from jax.experimental import pallas as pl
from jax.experimental.pallas import tpu as pltpu

D_MODEL = 1024
SEQ = 16384
DEC_BATCH = 128
DEC_SEQ = 8
PAST_LEN = 8192
PAGE_SIZE = 128
N_PAGES = PAST_LEN // PAGE_SIZE
N_HEADS = 16
HEAD_DIM = 64
N_KV_HEADS = 4
GROUP = N_HEADS // N_KV_HEADS
KV_WIDTH = N_KV_HEADS * HEAD_DIM
Q_WIDTH = N_HEADS * HEAD_DIM
N_KV_SETS = 6
N_NSA_BRANCH = 3
CMP_LEN = 32
CMP_STRIDE = 16
PHI_HIDDEN = 2 * HEAD_DIM
SEL_LEN = 64
SEL_RATIO = SEL_LEN // CMP_STRIDE
N_SEL = 16
N_LOCAL = 2
WINDOW = 512
D_CONV = D_MODEL // 2
CONV_WIDTH = 31
N_EXPERTS = 32
TOP_K = 4
D_EXPERT = D_MODEL
SWIGLU_LIMIT = 7.0
SWIGLU_ALPHA = 1.702
DN_ALPHA = 2.0 ** 0.25
LN_EPS = 1e-5
NEG_INF = -1e30
FORCE_SCORE = 1e9
OFF_KV = Q_WIDTH
OFF_GLU = OFF_KV + N_KV_SETS * KV_WIDTH
OFF_NSA_G = OFF_GLU + 2 * D_CONV
OFF_MERGE = OFF_NSA_G + N_HEADS * N_NSA_BRANCH

LANES = 128
GATE_PAD = LANES
CONV_HALO = 32
VMEM_LIMIT = 56 * 1024 * 1024
Q_TILE = 128
KV_TILE = 512
SLC_Q_TILE = 256
SEL_LANES = 256
HEAD_PAIRS = KV_WIDTH // LANES
ANY_ROWS = 8
FLAG_BITS = 32
FLAG_WORDS = SEQ // KV_TILE // FLAG_BITS
SLOPES = tuple(2.0 ** (-8.0 * (h + 1) / N_HEADS) for h in range(N_HEADS))

F32 = jnp.float32
BF16 = jnp.bfloat16


def _params(*sem):
    return pltpu.CompilerParams(dimension_semantics=sem, vmem_limit_bytes=VMEM_LIMIT)


def _dot(a, b):
    return jnp.dot(a, b, preferred_element_type=F32)


def _dot_nt(a, b):
    return lax.dot_general(a, b, (((1,), (1,)), ((), ())), preferred_element_type=F32)


def _dot_tn(a, b):
    return lax.dot_general(a, b, (((0,), (0,)), ((), ())), preferred_element_type=F32)


def _split(x):
    hi = x.astype(BF16)
    lo = (x - hi.astype(F32)).astype(BF16)
    return hi, lo


def _dot3(a, b, dot=_dot):
    ah, al = _split(a)
    bh, bl = _split(b)
    return dot(ah, bh) + (dot(ah, bl) + dot(al, bh))


def _sigmoid(x):
    return 1.0 / (1.0 + jnp.exp(-x))


def _layer_norm(x, g, b):
    mu = jnp.mean(x, axis=-1, keepdims=True)
    xc = x - mu
    var = jnp.mean(xc * xc, axis=-1, keepdims=True)
    return xc * lax.rsqrt(var + LN_EPS) * g + b


def _const_spec(shape):
    return pl.BlockSpec(shape, lambda *_: (0,) * len(shape))


def _iota(shape, axis):
    return lax.broadcasted_iota(jnp.int32, shape, axis)


def _ada_kernel(c_ref, w_ref, b_ref, o_ref):
    o_ref[...] = _dot3(c_ref[...], w_ref[...]) + b_ref[...]


def _adaln(c_all, w_ada, b_ada):
    n = c_all.shape[0]
    tn = 1536
    return pl.pallas_call(
        _ada_kernel,
        grid=(6 * D_MODEL // tn,),
        in_specs=[pl.BlockSpec((n, D_MODEL), lambda j: (0, 0)),
                  pl.BlockSpec((D_MODEL, tn), lambda j: (0, j)),
                  pl.BlockSpec((1, tn), lambda j: (0, j))],
        out_specs=pl.BlockSpec((n, tn), lambda j: (0, j)),
        out_shape=jax.ShapeDtypeStruct((n, 6 * D_MODEL), F32),
        compiler_params=_params("parallel"),
    )(c_all, w_ada, b_ada.reshape(1, -1))


def _proj_kernel(x_ref, sc_ref, sh_ref, wq_ref, wkv_ref, wglu_ref, wg_ref, wm_ref,
                 q_ref, kv_ref, win_ref, kvb_ref, u_ref, g_ref, m_ref):
    h = (x_ref[...] * (1.0 + sc_ref[...]) + sh_ref[...]).astype(BF16)
    q_ref[...] = _dot(h, wq_ref[...]).astype(BF16)
    kv = _dot(h, wkv_ref[...])
    kv_ref[...] = kv[:, :4 * KV_WIDTH]
    win_ref[...] = kv[:, 4 * KV_WIDTH:]
    kvb_ref[...] = kv.astype(BF16)
    glu = _dot(h, wglu_ref[...])
    u_ref[...] = glu[:, :D_CONV] * _sigmoid(glu[:, D_CONV:])
    g_ref[...] = _sigmoid(_dot(h, wg_ref[...]))
    m_ref[...] = _sigmoid(_dot(h, wm_ref[...]))


def _project(x, sc, sh, wts, tm):
    n = x.shape[0]
    per_row = sc.shape[0] != 1
    mod_rows = tm if per_row else 1

    def mod_spec(col):
        return pl.BlockSpec((mod_rows, D_MODEL), (lambda i: (i, col)) if per_row else (lambda i: (0, col)))

    widths = (Q_WIDTH, 4 * KV_WIDTH, 2 * KV_WIDTH, N_KV_SETS * KV_WIDTH, D_CONV, GATE_PAD, 2 * D_MODEL)
    dtypes = (BF16, F32, F32, BF16, F32, F32, F32)
    return pl.pallas_call(
        _proj_kernel,
        grid=(n // tm,),
        in_specs=[pl.BlockSpec((tm, D_MODEL), lambda i: (i, 0)), mod_spec(1), mod_spec(0)]
                 + [_const_spec(w.shape) for w in wts],
        out_specs=[pl.BlockSpec((tm, w), lambda i: (i, 0)) for w in widths],
        out_shape=[jax.ShapeDtypeStruct((n, w), d) for w, d in zip(widths, dtypes)],
        compiler_params=_params("parallel"),
    )(x, sc, sh, *wts)


def _conv_kernel(cur_ref, halo_ref, dw_ref, dwb_ref, lg_ref, lb_ref, wpw_ref, o_ref, ext_ref):
    bb, tm, _ = cur_ref.shape
    ext_ref[:, :CONV_HALO, :] = halo_ref[...]
    ext_ref[:, CONV_HALO:, :] = cur_ref[...]
    first = CONV_HALO - (CONV_WIDTH - 1)
    acc = jnp.zeros((bb, tm, D_CONV), F32)
    for j in range(CONV_WIDTH):
        acc = acc + ext_ref[:, first + j:first + j + tm, :] * dw_ref[j:j + 1, :]
    y = _layer_norm(acc + dwb_ref[...], lg_ref[...], lb_ref[...])
    y = (y * _sigmoid(y)).reshape(bb * tm, D_CONV).astype(BF16)
    o_ref[...] = _dot(y, wpw_ref[...]).reshape(bb, tm, D_MODEL)


def _conv_branch(cur, halo, dw_w, dw_b, ln_g, ln_b, w_pw, bb):
    b, tm, _ = cur.shape
    return pl.pallas_call(
        _conv_kernel,
        grid=(b // bb,),
        in_specs=[pl.BlockSpec((bb, tm, D_CONV), lambda i: (i, 0, 0)),
                  pl.BlockSpec((bb, CONV_HALO, D_CONV), lambda i: (i, 0, 0)),
                  _const_spec((CONV_WIDTH, D_CONV)), _const_spec((1, D_CONV)),
                  _const_spec((1, D_CONV)), _const_spec((1, D_CONV)),
                  _const_spec((D_CONV, D_MODEL))],
        out_specs=pl.BlockSpec((bb, tm, D_MODEL), lambda i: (i, 0, 0)),
        out_shape=jax.ShapeDtypeStruct((b, tm, D_MODEL), F32),
        scratch_shapes=[pltpu.VMEM((bb, CONV_HALO + tm, D_CONV), F32)],
        compiler_params=_params("parallel"),
    )(cur, halo, dw_w, dw_b.reshape(1, -1), ln_g.reshape(1, -1), ln_b.reshape(1, -1), w_pw)


MERGE_TILE = 512
MOE_TILE = 2048
MOE_CHUNK = 256


def _merge_kernel(x_ref, cy_ref, oc_ref, os_ref, ow_ref, g_ref, mg_ref, g1_ref, sc2_ref, sh2_ref,
                  exp_ref, wo_ref, lg_ref, lb_ref, wr_ref, br_ref,
                  x1_ref, h2_ref, wt_ref, rank_ref, cnt_ref):
    tm = x_ref.shape[0]
    gh, gl = _split(g_ref[...])
    nsa = jnp.zeros((tm, D_MODEL), F32)
    for n, o_ref in enumerate((oc_ref, os_ref, ow_ref)):
        e = exp_ref[n]
        nsa = nsa + o_ref[...] * (_dot(gh, e) + _dot(gl, e))
    mix = mg_ref[:, :D_MODEL] * cy_ref[...] + mg_ref[:, D_MODEL:] * nsa
    y = _dot(mix.astype(BF16), wo_ref[...])
    x1 = _layer_norm(DN_ALPHA * x_ref[...] + g1_ref[...] * y, lg_ref[...], lb_ref[...])
    x1_ref[...] = x1
    h2 = x1 * (1.0 + sc2_ref[...]) + sh2_ref[...]
    h2_ref[...] = h2.astype(BF16)
    logits = _dot3(wr_ref[...], h2, _dot_nt) + br_ref[...]
    eidx = _iota(logits.shape, 0)
    picked = jnp.zeros(logits.shape, F32)
    wsum = jnp.zeros((1, tm), F32)
    wts = jnp.zeros(logits.shape, F32)
    v0 = None
    for k in range(TOP_K):
        v = jnp.max(logits, axis=0, keepdims=True)
        first = jnp.min(jnp.where(logits == v, eidx, N_EXPERTS), axis=0, keepdims=True)
        hit = eidx == first
        if k == 0:
            v0 = v
        ev = jnp.exp(v - v0)
        wts = wts + jnp.where(hit, ev, 0.0)
        wsum = wsum + ev
        picked = picked + jnp.where(hit, 1.0, 0.0)
        logits = jnp.where(hit, -jnp.inf, logits)
    wt_ref[0] = wts / wsum
    upper = jnp.where(_iota((tm, tm), 0) < _iota((tm, tm), 1), 1.0, 0.0).astype(BF16)
    rank_ref[0] = _dot(picked.astype(BF16), upper)
    cnt = jnp.sum(picked, axis=1, keepdims=True)
    cnt_ref[0] = jnp.broadcast_to(cnt, (N_EXPERTS, LANES)).astype(jnp.int32)


def _merge(x, conv_y, o_cmp, o_slc, o_win, g_nsa, merge_g, mod, gate_expand, w_out, ln_g, ln_b,
           w_router_t, b_router):
    n = x.shape[0]
    tm = MERGE_TILE
    nt = n // tm
    per_row = mod.shape[0] != 1

    def row_spec(w):
        return pl.BlockSpec((tm, w), lambda i: (i, 0))

    def mod_spec(col):
        return pl.BlockSpec((tm if per_row else 1, D_MODEL),
                            (lambda i: (i, col)) if per_row else (lambda i: (0, col)))

    tile_spec = pl.BlockSpec((1, N_EXPERTS, tm), lambda i: (i, 0, 0))
    return pl.pallas_call(
        _merge_kernel,
        grid=(nt,),
        in_specs=[row_spec(D_MODEL)] * 5 + [row_spec(GATE_PAD), row_spec(2 * D_MODEL),
                  mod_spec(2), mod_spec(4), mod_spec(3),
                  _const_spec(gate_expand.shape), _const_spec(w_out.shape),
                  _const_spec((1, D_MODEL)), _const_spec((1, D_MODEL)),
                  _const_spec(w_router_t.shape), _const_spec((N_EXPERTS, 1))],
        out_specs=[row_spec(D_MODEL), row_spec(D_MODEL), tile_spec, tile_spec,
                   pl.BlockSpec((1, N_EXPERTS, LANES), lambda i: (i, 0, 0))],
        out_shape=[jax.ShapeDtypeStruct((n, D_MODEL), F32), jax.ShapeDtypeStruct((n, D_MODEL), BF16),
                   jax.ShapeDtypeStruct((nt, N_EXPERTS, tm), F32),
                   jax.ShapeDtypeStruct((nt, N_EXPERTS, tm), F32),
                   jax.ShapeDtypeStruct((nt, N_EXPERTS, LANES), jnp.int32)],
        compiler_params=_params("parallel"),
    )(x, conv_y, o_cmp, o_slc, o_win, g_nsa, merge_g, mod, mod, mod, gate_expand, w_out,
      ln_g.reshape(1, -1), ln_b.reshape(1, -1), w_router_t, b_router.reshape(-1, 1))


def _moe_kernel(cnt_ref, x1_ref, h2_ref, wt_ref, rank_ref, g2_ref, wup_ref, bup_ref, wdn_ref, bdn_ref,
                lg_ref, lb_ref, o_ref, acc_ref):
    i = pl.program_id(0)
    e = pl.program_id(1)
    tm = x1_ref.shape[0]

    @pl.when(e == 0)
    def _():
        acc_ref[...] = jnp.zeros_like(acc_ref)

    cnt = cnt_ref[i * N_EXPERTS + e]
    for ch in range(tm // MOE_CHUNK):
        @pl.when(cnt > ch * MOE_CHUNK)
        def _():
            w_e = wt_ref[0, pl.ds(e, 1), :]
            slot = rank_ref[0, pl.ds(e, 1), :] - float(ch * MOE_CHUNK)
            rows = _iota((MOE_CHUNK, tm), 0).astype(F32)
            onehot = jnp.where((rows == slot) & (w_e > 0.0), 1.0, 0.0)
            wc = jnp.sum(onehot * w_e, axis=1, keepdims=True)
            sel = onehot.astype(BF16)
            xc = _dot(sel, h2_ref[...]).astype(BF16)
            u = _dot(xc, wup_ref[0]) + bup_ref[0]
            x_glu = jnp.minimum(u[:, :D_EXPERT], SWIGLU_LIMIT)
            x_lin = jnp.clip(u[:, D_EXPERT:], -SWIGLU_LIMIT, SWIGLU_LIMIT)
            a = x_glu * _sigmoid(SWIGLU_ALPHA * x_glu) * (x_lin + 1.0)
            y = (_dot(a.astype(BF16), wdn_ref[0]) + bdn_ref[0]) * wc
            acc_ref[...] += _dot_tn(sel, y.astype(BF16))

    @pl.when(e == N_EXPERTS - 1)
    def _():
        o_ref[...] = _layer_norm(DN_ALPHA * x1_ref[...] + g2_ref[...] * acc_ref[...],
                                 lg_ref[...], lb_ref[...])


def _moe(x1, h2, wt, rank, cnt, mod, w_up, b_up, w_down, b_down, ln_g, ln_b):
    n = x1.shape[0]
    tm = min(MOE_TILE, n)
    nt = n // tm
    per_row = mod.shape[0] != 1
    once = functools.partial(pl.BlockSpec, pipeline_mode=pl.Buffered(1))
    g2_spec = pl.BlockSpec((tm if per_row else 1, D_MODEL),
                           (lambda i, e, c: (i, 5)) if per_row else (lambda i, e, c: (0, 5)))
    tile_spec = pl.BlockSpec((1, N_EXPERTS, tm), lambda i, e, c: (i, 0, 0))
    grid_spec = pltpu.PrefetchScalarGridSpec(
        num_scalar_prefetch=1,
        grid=(nt, N_EXPERTS),
        in_specs=[once((tm, D_MODEL), lambda i, e, c: (i, 0)),
                  once((tm, D_MODEL), lambda i, e, c: (i, 0)),
                  tile_spec, tile_spec, g2_spec,
                  pl.BlockSpec((1, D_MODEL, 2 * D_EXPERT), lambda i, e, c: (e, 0, 0)),
                  pl.BlockSpec((1, 1, 2 * D_EXPERT), lambda i, e, c: (e, 0, 0)),
                  pl.BlockSpec((1, D_EXPERT, D_MODEL), lambda i, e, c: (e, 0, 0)),
                  pl.BlockSpec((1, 1, D_MODEL), lambda i, e, c: (e, 0, 0)),
                  pl.BlockSpec((1, D_MODEL), lambda i, e, c: (0, 0)),
                  pl.BlockSpec((1, D_MODEL), lambda i, e, c: (0, 0))],
        out_specs=once((tm, D_MODEL), lambda i, e, c: (i, 0)),
        scratch_shapes=[pltpu.VMEM((tm, D_MODEL), F32)],
    )
    return pl.pallas_call(
        _moe_kernel,
        grid_spec=grid_spec,
        out_shape=jax.ShapeDtypeStruct((n, D_MODEL), F32),
        compiler_params=_params("parallel", "arbitrary"),
    )(cnt, x1, h2, wt, rank, mod, w_up, b_up.reshape(N_EXPERTS, 1, -1), w_down,
      b_down.reshape(N_EXPERTS, 1, -1), ln_g.reshape(1, -1), ln_b.reshape(1, -1))


def _softmax_rows(s, valid):
    s = jnp.where(valid, s, NEG_INF)
    m = jnp.max(s, axis=-1, keepdims=True)
    e = jnp.exp(s - m)
    l = jnp.sum(e, axis=-1, keepdims=True)
    return e * jnp.where(m > 0.5 * NEG_INF, 1.0 / l, 0.0)


def _flash_update(s, valid, v, m_old, l_old, acc_old, pv=_dot):
    s = jnp.where(valid, s, NEG_INF)
    m_new = jnp.maximum(m_old, jnp.max(s, axis=-1, keepdims=True))
    p = jnp.where(valid, jnp.exp(s - m_new), 0.0)
    alpha = jnp.exp(m_old - m_new)
    l_new = alpha * l_old + jnp.sum(p, axis=-1, keepdims=True)
    acc_new = alpha * acc_old + pv(p.astype(BF16), v)
    return m_new, l_new, acc_new


def _shift_right_one(x):
    return jnp.where(_iota(x.shape, 1) == 0, 0.0, pltpu.roll(x, 1, 1))


def _block_scores(imp, nj):
    parts = [imp[:, k * nj:(k + 1) * nj] for k in range(SEL_RATIO)]
    return (parts[0] + parts[1]) + (parts[2] + parts[3]) + _shift_right_one(parts[3])


def _select_blocks(score, qblk, n_blocks):
    j = _iota(score.shape, 1)
    back = qblk - j
    forced = (j == 0) | ((back >= 0) & (back < N_LOCAL))
    score = jnp.where(forced, FORCE_SCORE, jnp.where(back >= 0, score, -1.0))
    score = jnp.where(j < n_blocks, score, -jnp.inf)
    picked = jnp.zeros(score.shape, F32)
    lane = j.astype(F32)
    for _ in range(N_SEL):
        m = jnp.max(score, axis=-1, keepdims=True)
        first = jnp.min(jnp.where(score == m, lane, float(SEL_LANES)), axis=-1, keepdims=True)
        hit = lane == first
        picked = jnp.where(hit, 1.0, picked)
        score = jnp.where(hit, -jnp.inf, score)
    return picked


def _gelu_tanh(x):
    return x * (0.5 * (1.0 + jnp.tanh(0.7978845608028654 * (x + 0.044715 * (x * x * x)))))


def _compress_rows(buf_ref, pe_ref, w1_ref, w2_ref, nj):
    outs = []
    for pair in range(HEAD_PAIRS):
        acc = jnp.zeros((SEL_RATIO * nj, 2 * PHI_HIDDEN), F32)
        for s in range(CMP_LEN):
            parts = [buf_ref[pair, pl.ds(CMP_STRIDE * k + s, nj, stride=SEL_LEN), :] for k in range(SEL_RATIO)]
            x = (jnp.concatenate(parts, axis=0) + pe_ref[s:s + 1, :]).astype(BF16)
            acc = acc + _dot(x, w1_ref[s])
        outs.append(_dot(_gelu_tanh(acc).astype(BF16), w2_ref[...]))
    return jnp.concatenate(outs, axis=1)


CMP_TILE = 4096


def _cmp_prompt_kernel(x_ref, halo_ref, pe_ref, w1_ref, w2_ref, o_ref, buf_ref):
    t = pl.program_id(0)
    last = t == pl.num_programs(0) - 1
    for pair in range(HEAD_PAIRS):
        cols = slice(pair * LANES, (pair + 1) * LANES)
        buf_ref[pair, :CMP_TILE, :] = x_ref[:, cols]
        buf_ref[pair, CMP_TILE:, :] = jnp.where(last, 0.0, halo_ref[:, cols])
    nj = CMP_TILE // SEL_LEN
    out = _compress_rows(buf_ref, pe_ref, w1_ref, w2_ref, nj)
    o_ref[...] = out.reshape(SEL_RATIO, nj, KV_WIDTH).astype(BF16)


def _compress_prompt(kv4, kv_set, pe, w1, w2):
    t_len = kv4.shape[0]
    nt = t_len // CMP_TILE
    nj = CMP_TILE // SEL_LEN
    halo_blocks = CMP_TILE // SEL_LEN
    last_halo = t_len // SEL_LEN - 1
    out = pl.pallas_call(
        _cmp_prompt_kernel,
        grid=(nt,),
        in_specs=[pl.BlockSpec((CMP_TILE, KV_WIDTH), lambda t: (t, kv_set)),
                  pl.BlockSpec((SEL_LEN, KV_WIDTH),
                               lambda t: (jnp.minimum((t + 1) * halo_blocks, last_halo), kv_set)),
                  _const_spec(pe.shape), _const_spec(w1.shape), _const_spec(w2.shape)],
        out_specs=pl.BlockSpec((SEL_RATIO, nj, KV_WIDTH), lambda t: (0, t, 0)),
        out_shape=jax.ShapeDtypeStruct((SEL_RATIO, t_len // SEL_LEN, KV_WIDTH), BF16),
        scratch_shapes=[pltpu.VMEM((HEAD_PAIRS, CMP_TILE + SEL_LEN, LANES), F32)],
        compiler_params=_params("arbitrary"),
    )(kv4, kv4, pe, w1, w2)
    return out.reshape(t_len // CMP_STRIDE, KV_WIDTH)


def _stack_heads(q_ref, g):
    return jnp.concatenate([q_ref[:, (g * GROUP + r) * HEAD_DIM:(g * GROUP + r + 1) * HEAD_DIM]
                            for r in range(GROUP)], axis=0)


def _head_bias(g, rel):
    return jnp.concatenate([jnp.broadcast_to(SLOPES[g * GROUP + r] * rel, (Q_TILE, rel.shape[1]))
                            for r in range(GROUP)], axis=0)


def _cmp_attn_prompt_kernel(q_ref, kc_ref, vc_ref, o_ref, sel_ref, any_ref):
    nc = kc_ref.shape[0]
    nj = nc // SEL_RATIO
    s0 = pl.program_id(0) * Q_TILE
    tq = s0 + _iota((Q_TILE, 1), 0)
    col = _iota((1, nc), 1)
    ends = (SEL_RATIO * (col % nj) + col // nj) * CMP_STRIDE + (CMP_LEN - 1)
    rel = (ends - s0).astype(F32)
    valid = jnp.concatenate([ends <= tq] * GROUP, axis=0)
    any_rows = []
    scores = []
    for g in range(N_KV_HEADS):
        kc = kc_ref[:, g * HEAD_DIM:(g + 1) * HEAD_DIM]
        vc = vc_ref[:, g * HEAD_DIM:(g + 1) * HEAD_DIM]
        p = _softmax_rows(_dot_nt(_stack_heads(q_ref, g), kc) + _head_bias(g, rel), valid)
        o = _dot(p.astype(BF16), vc)
        imp = jnp.zeros((Q_TILE, nc), F32)
        for r in range(GROUP):
            h = g * GROUP + r
            o_ref[:, h * HEAD_DIM:(h + 1) * HEAD_DIM] = o[r * Q_TILE:(r + 1) * Q_TILE]
            imp = imp + p[r * Q_TILE:(r + 1) * Q_TILE]
        scores.append(_block_scores(imp, nj))
    picked = _select_blocks(jnp.concatenate(scores, axis=0),
                            jnp.concatenate([tq // SEL_LEN] * N_KV_HEADS, axis=0), nj)
    for g in range(N_KV_HEADS):
        sel_ref[g] = picked[g * Q_TILE:(g + 1) * Q_TILE].astype(BF16)
        any_rows.append(jnp.max(picked[g * Q_TILE:(g + 1) * Q_TILE], axis=0, keepdims=True))
    any_ref[0] = jnp.concatenate(any_rows + [jnp.zeros((ANY_ROWS - N_KV_HEADS, SEL_LANES), F32)], axis=0)


def _cmp_attn_prompt(q, kc, vc):
    t_len = q.shape[0]
    nt = t_len // Q_TILE
    return pl.pallas_call(
        _cmp_attn_prompt_kernel,
        grid=(nt,),
        in_specs=[pl.BlockSpec((Q_TILE, Q_WIDTH), lambda i: (i, 0)),
                  _const_spec(kc.shape), _const_spec(vc.shape)],
        out_specs=[pl.BlockSpec((Q_TILE, Q_WIDTH), lambda i: (i, 0)),
                   pl.BlockSpec((N_KV_HEADS, Q_TILE, SEL_LANES), lambda i: (0, i, 0)),
                   pl.BlockSpec((1, ANY_ROWS, SEL_LANES), lambda i: (i, 0, 0))],
        out_shape=[jax.ShapeDtypeStruct((t_len, Q_WIDTH), F32),
                   jax.ShapeDtypeStruct((N_KV_HEADS, t_len, SEL_LANES), BF16),
                   jax.ShapeDtypeStruct((nt, ANY_ROWS, SEL_LANES), F32)],
        compiler_params=_params("parallel"),
    )(q, kc, vc)


def _tile_flags(any_sel):
    nt = any_sel.shape[0]
    per_tile = KV_TILE // SEL_LEN
    hit = any_sel[:, :N_KV_HEADS].reshape(nt, N_KV_HEADS, SEL_LANES // per_tile, per_tile).max(-1) > 0.0
    bits = hit.reshape(nt, N_KV_HEADS, -1, FLAG_BITS).astype(jnp.uint32) << jnp.arange(FLAG_BITS, dtype=jnp.uint32)
    words = lax.bitcast_convert_type(bits.sum(-1, dtype=jnp.uint32), jnp.int32)
    per = SLC_Q_TILE // Q_TILE
    words = words.reshape(nt // per, per, N_KV_HEADS, -1)
    joined = words[:, 0]
    for k in range(1, per):
        joined = joined | words[:, k]
    return joined.reshape(-1)


def _slc_attn_prompt_kernel(flag_ref, q_ref, k_ref, v_ref, sel_ref, o_ref, qs_ref, m_ref, l_ref, acc_ref):
    i = pl.program_id(0)
    tq = i * SLC_Q_TILE + _iota((SLC_Q_TILE, 1), 0)
    for h in range(N_HEADS):
        qs_ref[h // GROUP, (h % GROUP) * SLC_Q_TILE:(h % GROUP + 1) * SLC_Q_TILE, :] = q_ref[:, h * HEAD_DIM:(h + 1) * HEAD_DIM]
    m_ref[...] = jnp.full(m_ref.shape, NEG_INF, F32)
    l_ref[...] = jnp.zeros(l_ref.shape, F32)
    acc_ref[...] = jnp.zeros(acc_ref.shape, F32)

    def tile(t, carry):
        base = pl.multiple_of(t * KV_TILE, KV_TILE)
        causal = tq - (base + _iota((1, KV_TILE), 1)) >= 0
        rel = (base - i * SLC_Q_TILE + _iota((1, KV_TILE), 1)).astype(F32)
        expand = jnp.where(_iota((SEL_LANES, KV_TILE), 0)
                           == t * (KV_TILE // SEL_LEN) + _iota((SEL_LANES, KV_TILE), 1) // SEL_LEN,
                           1.0, 0.0).astype(BF16)
        for g in range(N_KV_HEADS):
            word = flag_ref[(i * N_KV_HEADS + g) * FLAG_WORDS + t // FLAG_BITS]

            @pl.when(((word >> (t % FLAG_BITS)) & 1) == 1)
            def _():
                valid = (_dot(sel_ref[g], expand) > 0.5) & causal
                kt = k_ref[pl.ds(base, KV_TILE), g * HEAD_DIM:(g + 1) * HEAD_DIM]
                vt = v_ref[pl.ds(base, KV_TILE), g * HEAD_DIM:(g + 1) * HEAD_DIM]
                raw = _dot_nt(qs_ref[g], kt)
                s = jnp.concatenate(
                    [jnp.where(valid, raw[r * SLC_Q_TILE:(r + 1) * SLC_Q_TILE] + SLOPES[g * GROUP + r] * rel, NEG_INF)
                     for r in range(GROUP)], axis=0)
                m_old = m_ref[g]
                m_new = jnp.maximum(m_old, jnp.max(s, axis=-1, keepdims=True))
                p = jnp.exp(s - jnp.concatenate([m_new] * (KV_TILE // LANES), axis=1))
                alpha = jnp.exp(m_old - m_new)
                l_ref[g] = alpha * l_ref[g] + jnp.sum(p, axis=-1, keepdims=True)
                acc_ref[g] = alpha[:, :HEAD_DIM] * acc_ref[g] + _dot(p.astype(BF16), vt)
                m_ref[g] = m_new
        return carry

    lax.fori_loop(0, (i * SLC_Q_TILE) // KV_TILE + 1, tile, 0)
    for h in range(N_HEADS):
        rows = slice((h % GROUP) * SLC_Q_TILE, (h % GROUP + 1) * SLC_Q_TILE)
        l = l_ref[h // GROUP, rows, :HEAD_DIM]
        o_ref[:, h * HEAD_DIM:(h + 1) * HEAD_DIM] = acc_ref[h // GROUP, rows, :] / jnp.where(l > 0.0, l, 1.0)


def _slc_attn_prompt(q, kvb, sel, flags):
    t_len = q.shape[0]
    resident = functools.partial(pl.BlockSpec, pipeline_mode=pl.Buffered(1))
    rows = GROUP * SLC_Q_TILE
    grid_spec = pltpu.PrefetchScalarGridSpec(
        num_scalar_prefetch=1,
        grid=(t_len // SLC_Q_TILE,),
        in_specs=[pl.BlockSpec((SLC_Q_TILE, Q_WIDTH), lambda i, f: (i, 0)),
                  resident((t_len, KV_WIDTH), lambda i, f: (0, 2)),
                  resident((t_len, KV_WIDTH), lambda i, f: (0, 3)),
                  pl.BlockSpec((N_KV_HEADS, SLC_Q_TILE, SEL_LANES), lambda i, f: (0, i, 0))],
        out_specs=pl.BlockSpec((SLC_Q_TILE, Q_WIDTH), lambda i, f: (i, 0)),
        scratch_shapes=[pltpu.VMEM((N_KV_HEADS, rows, HEAD_DIM), BF16),
                        pltpu.VMEM((N_KV_HEADS, rows, LANES), F32), pltpu.VMEM((N_KV_HEADS, rows, LANES), F32),
                        pltpu.VMEM((N_KV_HEADS, rows, HEAD_DIM), F32)],
    )
    return pl.pallas_call(
        _slc_attn_prompt_kernel,
        grid_spec=grid_spec,
        out_shape=jax.ShapeDtypeStruct((t_len, Q_WIDTH), F32),
        compiler_params=_params("parallel"),
    )(flags, q, kvb, kvb, sel)


def _win_attn_prompt_kernel(q_ref, k_ref, v_ref, o_ref):
    s0 = pl.multiple_of(pl.program_id(0) * Q_TILE, Q_TILE)
    n_keys = WINDOW + Q_TILE
    tq = s0 + _iota((Q_TILE, 1), 0)
    pw = s0 - WINDOW + _iota((1, n_keys), 1)
    dw = tq - pw
    valid = (dw >= 0) & (dw < WINDOW) & (pw >= 0)
    valid = jnp.concatenate([valid] * GROUP, axis=0)
    rel = (_iota((1, n_keys), 1) - WINDOW).astype(F32)
    for g in range(N_KV_HEADS):
        kw = k_ref[pl.ds(s0, n_keys), g * HEAD_DIM:(g + 1) * HEAD_DIM]
        vw = v_ref[pl.ds(s0, n_keys), g * HEAD_DIM:(g + 1) * HEAD_DIM]
        p = _softmax_rows(_dot_nt(_stack_heads(q_ref, g), kw) + _head_bias(g, rel), valid)
        o = _dot(p.astype(BF16), vw)
        for r in range(GROUP):
            h = g * GROUP + r
            o_ref[:, h * HEAD_DIM:(h + 1) * HEAD_DIM] = o[r * Q_TILE:(r + 1) * Q_TILE]


def _win_attn_prompt(q, kw_pad, vw_pad):
    t_len = q.shape[0]
    resident = functools.partial(pl.BlockSpec, pipeline_mode=pl.Buffered(1))
    return pl.pallas_call(
        _win_attn_prompt_kernel,
        grid=(t_len // Q_TILE,),
        in_specs=[pl.BlockSpec((Q_TILE, Q_WIDTH), lambda i: (i, 0)),
                  resident(kw_pad.shape, lambda i: (0, 0)), resident(vw_pad.shape, lambda i: (0, 0))],
        out_specs=pl.BlockSpec((Q_TILE, Q_WIDTH), lambda i: (i, 0)),
        out_shape=jax.ShapeDtypeStruct((t_len, Q_WIDTH), F32),
        compiler_params=_params("parallel"),
    )(q, kw_pad, vw_pad)


S_ROWS = N_HEADS * DEC_SEQ
S_TAIL = SEL_LEN
S_CMP_BLOCKS = PAST_LEN // CMP_STRIDE
S_CHUNK = 2048


def _fetch_pages(pt_ref, sem_ref, page_copy, meanwhile=None):
    b = pl.program_id(0)
    slot = b % 2

    def copy(seq, p, dst_slot):
        return page_copy(pt_ref[seq * N_PAGES + p], p, dst_slot, sem_ref.at[dst_slot])

    def start(seq, dst_slot):
        def body(p, c):
            copy(seq, p, dst_slot).start()
            return c
        lax.fori_loop(0, N_PAGES, body, 0)

    @pl.when(b == 0)
    def _():
        start(b, slot)

    @pl.when(b + 1 < pl.num_programs(0))
    def _():
        start(b + 1, 1 - slot)

    done = meanwhile() if meanwhile is not None else None

    def wait(p, c):
        copy(b, p, slot).wait()
        return c
    lax.fori_loop(0, N_PAGES, wait, 0)
    return slot, done


PAGES_PER_STEP = 8
TAPS_PER_DOT = 8
S_CHUNKS = PAST_LEN // CMP_STRIDE + ANY_ROWS


def _cmp_sample_kernel(kv_set, pt_ref, cache_ref, new_ref, pet_ref, w1_ref, w2_ref, o_ref,
                       stage_ref, buf_ref, sem_ref):
    def page_copy(page, p, slot, sem):
        return pltpu.make_async_copy(cache_ref.at[page, kv_set],
                                     stage_ref.at[slot, pl.ds(p * KV_WIDTH, KV_WIDTH), :], sem)

    slot, _ = _fetch_pages(pt_ref, sem_ref, page_copy)
    per_page = PAGE_SIZE // CMP_STRIDE
    regroup = jnp.where((_iota((PAGE_SIZE, PAGE_SIZE), 1) % CMP_STRIDE) * per_page
                        + _iota((PAGE_SIZE, PAGE_SIZE), 1) // CMP_STRIDE == _iota((PAGE_SIZE, PAGE_SIZE), 0),
                        1.0, 0.0).astype(BF16)

    def to_rows(i, c):
        for u in range(PAGES_PER_STEP):
            p = i * PAGES_PER_STEP + u
            src = pl.multiple_of(p * KV_WIDTH, KV_WIDTH)
            dst = pl.multiple_of(p * per_page, per_page)
            x_t = stage_ref[slot, pl.ds(src, KV_WIDTH), :].astype(BF16)
            rows = _dot_nt(regroup, x_t)
            for pair in range(HEAD_PAIRS):
                for s in range(CMP_STRIDE):
                    buf_ref[pair, s, pl.ds(dst, per_page), :] = rows[s * per_page:(s + 1) * per_page,
                                                                     pair * LANES:(pair + 1) * LANES]
        return c
    lax.fori_loop(0, N_PAGES // PAGES_PER_STEP, to_rows, 0)
    first_row = _iota((ANY_ROWS, LANES), 0) == 0
    for pair in range(HEAD_PAIRS):
        for s in range(CMP_STRIDE):
            tail = jnp.zeros((ANY_ROWS, LANES), F32)
            if s < DEC_SEQ:
                row = new_ref[s:s + 1, pair * LANES:(pair + 1) * LANES].astype(BF16).astype(F32)
                tail = jnp.where(first_row, row, 0.0)
            buf_ref[pair, s, PAST_LEN // CMP_STRIDE:, :] = tail

    bias = jnp.zeros((1, 2 * PHI_HIDDEN), F32)
    for s in range(CMP_LEN):
        bias = bias + jnp.sum(pet_ref[:, s:s + 1] * w1_ref[s].astype(F32), axis=0, keepdims=True)
    nj = S_CMP_BLOCKS // SEL_RATIO
    outs = []
    for pair in range(HEAD_PAIRS):
        acc = jnp.zeros((SEL_RATIO * nj, 2 * PHI_HIDDEN), F32)
        for s0 in range(0, CMP_LEN, TAPS_PER_DOT):
            taps = [jnp.concatenate(
                [buf_ref[pair, s % CMP_STRIDE, pl.ds(k + s // CMP_STRIDE, nj, stride=SEL_RATIO), :]
                 for k in range(SEL_RATIO)], axis=0).astype(BF16) for s in range(s0, s0 + TAPS_PER_DOT)]
            w = w1_ref[s0:s0 + TAPS_PER_DOT].reshape(TAPS_PER_DOT * LANES, 2 * PHI_HIDDEN)
            acc = acc + _dot(jnp.concatenate(taps, axis=1), w)
        outs.append(_dot(_gelu_tanh(acc + bias).astype(BF16), w2_ref[...]))
    o_ref[0] = jnp.concatenate(outs, axis=1).astype(BF16)


def _compress_sample(page_table, cache, kv4_new, kv_set, pe, w1, w2):
    pe_t = pe.T
    grid_spec = pltpu.PrefetchScalarGridSpec(
        num_scalar_prefetch=1,
        grid=(DEC_BATCH,),
        in_specs=[pl.BlockSpec(memory_space=pl.ANY),
                  pl.BlockSpec((DEC_SEQ, KV_WIDTH), lambda b, pt: (b, kv_set)),
                  pl.BlockSpec(pe_t.shape, lambda b, pt: (0, 0)),
                  pl.BlockSpec(w1.shape, lambda b, pt: (0, 0, 0)),
                  pl.BlockSpec(w2.shape, lambda b, pt: (0, 0))],
        out_specs=pl.BlockSpec((1, S_CMP_BLOCKS, KV_WIDTH), lambda b, pt: (b, 0, 0)),
        scratch_shapes=[pltpu.VMEM((2, N_PAGES * KV_WIDTH, PAGE_SIZE), F32),
                        pltpu.VMEM((HEAD_PAIRS, CMP_STRIDE, S_CHUNKS, LANES), F32),
                        pltpu.SemaphoreType.DMA((2,))],
    )
    return pl.pallas_call(
        functools.partial(_cmp_sample_kernel, kv_set),
        grid_spec=grid_spec,
        out_shape=jax.ShapeDtypeStruct((DEC_BATCH, S_CMP_BLOCKS, KV_WIDTH), BF16),
        compiler_params=_params("arbitrary"),
    )(page_table.reshape(-1), cache, kv4_new, pe_t, w1, w2)


def _row_queries():
    return PAST_LEN + _iota((S_ROWS, 1), 0) % DEC_SEQ


def _cmp_attend_sample(q, slope, kc, vc):
    nc = S_CMP_BLOCKS
    nj = nc // SEL_RATIO
    tq = _row_queries()
    col = _iota((1, nc), 1)
    ends = (SEL_RATIO * (col % nj) + col // nj) * CMP_STRIDE + (CMP_LEN - 1)
    s = _dot_nt(q, kc) - slope * (tq - ends).astype(F32)
    p = _softmax_rows(s, ends <= tq)
    out = _dot(p.astype(BF16), vc)
    rows_g = GROUP * DEC_SEQ
    n_blocks = (PAST_LEN + DEC_SEQ + SEL_LEN - 1) // SEL_LEN
    qblk = (PAST_LEN + _iota((DEC_SEQ, 1), 0)) // SEL_LEN
    scores = []
    for g in range(N_KV_HEADS):
        imp = jnp.zeros((DEC_SEQ, nc), F32)
        for r in range(GROUP):
            imp = imp + p[g * rows_g + r * DEC_SEQ:g * rows_g + (r + 1) * DEC_SEQ]
        scores.append(jnp.concatenate([_block_scores(imp, nj), jnp.zeros((DEC_SEQ, SEL_LANES - nj), F32)], axis=1))
    picked = _select_blocks(jnp.concatenate(scores, axis=0), jnp.concatenate([qblk] * N_KV_HEADS, axis=0), n_blocks)
    sel = jnp.concatenate([picked[g * DEC_SEQ:(g + 1) * DEC_SEQ] for g in range(N_KV_HEADS)
                           for _ in range(GROUP)], axis=0).astype(BF16)
    return out, sel


def _pad_new_rows(x):
    return jnp.concatenate([x, jnp.zeros((LANES - DEC_SEQ, x.shape[1]), F32)], axis=0).astype(BF16)


def _attn_sample_kernel(pt_ref, cache_ref, q_ref, slope_ref, kc_ref, vc_ref, new_ref, cwin_ref, wnew_ref,
                        oc_ref, os_ref, ow_ref, buf_ref, sem_ref):
    def page_copy(page, p, slot, sem):
        return pltpu.make_async_copy(cache_ref.at[page, pl.ds(2, 2)],
                                     buf_ref.at[slot, :, :, pl.ds(p * PAGE_SIZE, PAGE_SIZE)], sem)

    q = q_ref[0]
    slope = slope_ref[...]
    slot, (o_cmp, sel) = _fetch_pages(pt_ref, sem_ref, page_copy,
                                      lambda: _cmp_attend_sample(q, slope, kc_ref[0], vc_ref[0]))
    oc_ref[0] = o_cmp
    tq = _row_queries()
    new_pos = PAST_LEN + _iota((1, LANES), 1)
    new_ok = (tq - new_pos >= 0) & (new_pos < PAST_LEN + DEC_SEQ)
    new_dist = (tq - new_pos).astype(F32)

    m = jnp.full((S_ROWS, 1), NEG_INF, F32)
    l = jnp.zeros((S_ROWS, 1), F32)
    acc = jnp.zeros((S_ROWS, KV_WIDTH), F32)
    for c in range(PAST_LEN // S_CHUNK):
        k_t = buf_ref[slot, 0, :, c * S_CHUNK:(c + 1) * S_CHUNK].astype(BF16)
        v_t = buf_ref[slot, 1, :, c * S_CHUNK:(c + 1) * S_CHUNK].astype(BF16)
        pos = c * S_CHUNK + _iota((1, S_CHUNK), 1)
        expand = jnp.where(_iota((SEL_LANES, S_CHUNK), 0)
                           == (c * S_CHUNK + _iota((SEL_LANES, S_CHUNK), 1)) // SEL_LEN, 1.0, 0.0).astype(BF16)
        valid = (_dot(sel, expand) > 0.5) & (tq - pos >= 0)
        s = _dot(q, k_t) - slope * (tq - pos).astype(F32)
        m, l, acc = _flash_update(s, valid, v_t, m, l, acc, pv=_dot_nt)
    new = _pad_new_rows(new_ref[...])
    in_last = jnp.sum(jnp.where(_iota((S_ROWS, SEL_LANES), 1) == PAST_LEN // SEL_LEN, sel.astype(F32), 0.0),
                      axis=1, keepdims=True) > 0.5
    s = _dot_nt(q, new[:, :KV_WIDTH]) - slope * new_dist
    m, l, acc = _flash_update(s, in_last & new_ok, new[:, KV_WIDTH:], m, l, acc)
    os_ref[0] = acc / jnp.where(l > 0.0, l, 1.0)

    wb = cwin_ref.shape[-1]
    pw = PAST_LEN - wb + _iota((1, wb), 1)
    dw = tq - pw
    s_past = _dot(q, cwin_ref[0, 0].astype(BF16)) - slope * dw.astype(F32)
    wnew = _pad_new_rows(wnew_ref[...])
    s_new = _dot_nt(q, wnew[:, :KV_WIDTH]) - slope * new_dist
    valid = jnp.concatenate([(dw >= 0) & (dw < WINDOW) & (pw >= 0), new_ok & (tq - new_pos < WINDOW)], axis=1)
    p = _softmax_rows(jnp.concatenate([s_past, s_new], axis=1), valid).astype(BF16)
    ow_ref[0] = _dot_nt(p[:, :wb], cwin_ref[0, 1].astype(BF16)) + _dot(p[:, wb:], wnew[:, KV_WIDTH:])


def _attn_sample(page_table, cache, qbd, slope_col, kc, vc, kv4_new, cache_win, win_new):
    wb = cache_win.shape[-1]
    seq_spec = lambda rows, w: pl.BlockSpec((1, rows, w), lambda b, pt: (b, 0, 0))
    grid_spec = pltpu.PrefetchScalarGridSpec(
        num_scalar_prefetch=1,
        grid=(DEC_BATCH,),
        in_specs=[pl.BlockSpec(memory_space=pl.ANY),
                  seq_spec(S_ROWS, KV_WIDTH),
                  pl.BlockSpec((S_ROWS, 1), lambda b, pt: (0, 0)),
                  seq_spec(S_CMP_BLOCKS, KV_WIDTH), seq_spec(S_CMP_BLOCKS, KV_WIDTH),
                  pl.BlockSpec((DEC_SEQ, 2 * KV_WIDTH), lambda b, pt: (b, 1)),
                  pl.BlockSpec((1, 2, KV_WIDTH, wb), lambda b, pt: (b, 0, 0, 0)),
                  pl.BlockSpec((DEC_SEQ, 2 * KV_WIDTH), lambda b, pt: (b, 0))],
        out_specs=[seq_spec(S_ROWS, KV_WIDTH)] * 3,
        scratch_shapes=[pltpu.VMEM((2, 2, KV_WIDTH, PAST_LEN), F32), pltpu.SemaphoreType.DMA((2,))],
    )
    return pl.pallas_call(
        _attn_sample_kernel,
        grid_spec=grid_spec,
        out_shape=[jax.ShapeDtypeStruct((DEC_BATCH, S_ROWS, KV_WIDTH), F32)] * 3,
        compiler_params=_params("arbitrary"),
    )(page_table.reshape(-1), cache, qbd, slope_col, kc, vc, kv4_new, cache_win, win_new)


def _gate_expand():
    rows = jnp.arange(GATE_PAD)[:, None]
    cols = jnp.arange(D_MODEL)[None, :]
    return jnp.stack([(rows == (cols // HEAD_DIM) * N_NSA_BRANCH + n) for n in range(N_NSA_BRANCH)]
                     ).astype(BF16)


def _compress_weights(pe, w1, w2):
    per_tile = LANES // HEAD_DIM
    eye = jnp.eye(per_tile, dtype=F32)
    w1_big = jnp.einsum('sde,gh->sgdhe', w1, eye).reshape(CMP_LEN, LANES, per_tile * PHI_HIDDEN).astype(BF16)
    w2_big = jnp.einsum('ed,gh->gehd', w2, eye).reshape(per_tile * PHI_HIDDEN, LANES).astype(BF16)
    return jnp.tile(pe, (1, per_tile)), w1_big, w2_big


def _rows_from_heads(o):
    o = o.reshape(DEC_BATCH, N_KV_HEADS, GROUP, DEC_SEQ, N_KV_HEADS, HEAD_DIM)
    idx = jnp.arange(N_KV_HEADS)
    own = o[:, idx, :, :, idx]
    return own.transpose(1, 3, 0, 2, 4).reshape(DEC_BATCH * DEC_SEQ, Q_WIDTH)


def kernel(x_prompt, x_sample, c_prompt, c_sample, cache_kv, cache_win, state_conv, page_table, w_ada, b_ada, w_in, conv_dw_w, conv_dw_b, conv_ln_g, conv_ln_b, w_conv_out, cmp_pe_k, cmp_w1_k, cmp_w2_k, cmp_pe_v, cmp_w1_v, cmp_w2_v, w_out, ln1_g, ln1_b, w_router, b_router, w_up, b_up, w_down, b_down, ln2_g, ln2_b):
    l = 0
    n_p = SEQ
    n_s = DEC_BATCH * DEC_SEQ

    c_all = jnp.concatenate([c_prompt, c_sample, jnp.zeros((7, D_MODEL), F32)], axis=0)
    mod = _adaln(c_all, w_ada[l], b_ada[l])
    mod_p = mod[0:1]
    mod_s = jnp.repeat(mod[1:1 + DEC_BATCH], DEC_SEQ, axis=0)

    wi = w_in[l]
    wts = ((wi[:, :OFF_KV] * (HEAD_DIM ** -0.5)).astype(BF16),
           wi[:, OFF_KV:OFF_GLU].astype(BF16),
           wi[:, OFF_GLU:OFF_NSA_G].astype(BF16),
           jnp.pad(wi[:, OFF_NSA_G:OFF_MERGE], ((0, 0), (0, GATE_PAD - N_HEADS * N_NSA_BRANCH))).astype(BF16),
           wi[:, OFF_MERGE:].astype(BF16))
    gate_expand = _gate_expand()
    w_out_b = w_out[l].astype(BF16)
    w_pw_b = w_conv_out[l].astype(BF16)
    w_up_b = w_up[l].astype(BF16)
    w_down_b = w_down[l].astype(BF16)
    w_router_t = w_router[l].T
    cmp_k = _compress_weights(cmp_pe_k[l], cmp_w1_k[l], cmp_w2_k[l])
    cmp_v = _compress_weights(cmp_pe_v[l], cmp_w1_v[l], cmp_w2_v[l])
    conv_w = (conv_dw_w[l], conv_dw_b[l], conv_ln_g[l], conv_ln_b[l], w_pw_b)

    def tail(x, conv_y, o3, g_nsa, merge_g, m):
        x1, h2, wt, rank, cnt = _merge(x, conv_y, *o3, g_nsa, merge_g, m, gate_expand, w_out_b,
                                       ln1_g[l], ln1_b[l], w_router_t, b_router[l])
        moe_tile = min(MOE_TILE, x.shape[0])
        per = moe_tile // MERGE_TILE
        cnt = cnt[:, :, 0].reshape(-1, per, N_EXPERTS)
        before = jnp.cumsum(cnt, axis=1) - cnt
        rank = rank.reshape(-1, per, N_EXPERTS, MERGE_TILE) + before[..., None].astype(F32)
        rank = rank.transpose(0, 2, 1, 3).reshape(-1, N_EXPERTS, moe_tile)
        wt = wt.reshape(-1, per, N_EXPERTS, MERGE_TILE).transpose(0, 2, 1, 3).reshape(-1, N_EXPERTS, moe_tile)
        return _moe(x1, h2, wt, rank, cnt.sum(1).reshape(-1), m, w_up_b, b_up[l], w_down_b, b_down[l],
                    ln2_g[l], ln2_b[l])

    xp = x_prompt.reshape(n_p, D_MODEL)
    q, kv4, win2, kvb, u, g_nsa, merge_g = _project(xp, mod_p, mod_p, wts, 256)
    ct = 512
    u3 = u.reshape(n_p // ct, ct, D_CONV)
    halo = jnp.concatenate([jnp.zeros((1, CONV_HALO, D_CONV), F32), u3[:-1, ct - CONV_HALO:]], axis=0)
    conv_y = _conv_branch(u3, halo, *conv_w, 1).reshape(n_p, D_MODEL)
    kc = _compress_prompt(kv4, 0, *cmp_k)
    vc = _compress_prompt(kv4, 1, *cmp_v)
    o_cmp, sel, any_sel = _cmp_attn_prompt(q, kc, vc)
    o_slc = _slc_attn_prompt(q, kvb, sel, _tile_flags(any_sel))
    kw_pad = jnp.pad(kvb[:, 4 * KV_WIDTH:5 * KV_WIDTH], ((WINDOW, 0), (0, 0)))
    vw_pad = jnp.pad(kvb[:, 5 * KV_WIDTH:], ((WINDOW, 0), (0, 0)))
    o_win = _win_attn_prompt(q, kw_pad, vw_pad)
    y_p = tail(xp, conv_y, (o_cmp, o_slc, o_win), g_nsa, merge_g, mod_p)
    out_kv_p = kv4.reshape(1, 1, n_p, 4, N_KV_HEADS, HEAD_DIM)
    out_win_p = win2[n_p - WINDOW:].reshape(1, 1, WINDOW, 2, N_KV_HEADS, HEAD_DIM)
    out_conv_p = u[n_p - (CONV_WIDTH - 1):].reshape(1, 1, CONV_WIDTH - 1, D_CONV)

    xs = x_sample.reshape(n_s, D_MODEL)
    q, kv4, win2, kvb, u, g_nsa, merge_g = _project(xs, mod_s, mod_s, wts, 256)
    u3 = u.reshape(DEC_BATCH, DEC_SEQ, D_CONV)
    st = state_conv[l]
    halo = jnp.concatenate([jnp.zeros((DEC_BATCH, CONV_HALO - (CONV_WIDTH - 1), D_CONV), F32), st], axis=1)
    conv_y = _conv_branch(u3, halo, *conv_w, DEC_BATCH).reshape(n_s, D_MODEL)
    qh = q.reshape(DEC_BATCH, DEC_SEQ, N_KV_HEADS, GROUP, HEAD_DIM).transpose(0, 2, 3, 1, 4)
    qbd = jnp.einsum('bgrqd,gh->bgrqhd', qh, jnp.eye(N_KV_HEADS, dtype=BF16)).reshape(DEC_BATCH, S_ROWS, KV_WIDTH)
    slope_col = jnp.repeat(jnp.asarray(SLOPES, F32), DEC_SEQ).reshape(S_ROWS, 1)
    cache = cache_kv[l].transpose(0, 2, 3, 4, 1).reshape(-1, 4, KV_WIDTH, PAGE_SIZE)
    cwin = cache_win[l].transpose(0, 2, 3, 4, 1).reshape(DEC_BATCH, 2, KV_WIDTH, -1)
    kc = _compress_sample(page_table, cache, kv4, 0, *cmp_k)
    vc = _compress_sample(page_table, cache, kv4, 1, *cmp_v)
    o_cmp, o_slc, o_win = _attn_sample(page_table, cache, qbd, slope_col, kc, vc, kv4, cwin, win2)
    o3 = tuple(_rows_from_heads(o) for o in (o_cmp, o_slc, o_win))
    y_s = tail(xs, conv_y, o3, g_nsa, merge_g, mod_s)
    out_kv_s = kv4.reshape(1, DEC_BATCH, DEC_SEQ, 4, N_KV_HEADS, HEAD_DIM)
    win_new = win2.reshape(DEC_BATCH, DEC_SEQ, 2, N_KV_HEADS, HEAD_DIM)
    out_win_s = jnp.concatenate([cache_win[l], win_new], axis=1)[:, DEC_SEQ:][None]
    out_conv_s = jnp.concatenate([st, u3], axis=1)[:, DEC_SEQ:][None]

    return (y_p.reshape(1, n_p, D_MODEL), y_s.reshape(DEC_BATCH, DEC_SEQ, D_MODEL),
            out_kv_p, out_kv_s, out_win_p, out_win_s, out_conv_p, out_conv_s)
```
